```python
import jax
import jax.numpy as jnp
from jax import lax

D_MODEL = 1024
BATCH = 2
SEQ = 8192
DEPTH = 2
DEC_BATCH = 128
DEC_SEQ = 8
PAST_LEN = 8192
PAGE_SIZE = 128

H_A = 4
DK = 128
DV = 128
RET_CHUNK = 128
H_B = 8
DH = 64
H_C = 16
NOPE = 64
ROPE_D = 32
V_DIM = 64
Q_LORA = 768
KV_LORA = 256
D_FF = 4 * D_MODEL
BLOCK_Q = 128
ROPE_BASE = 10000.0
LN_EPS = 1e-5
RMS_EPS = 1e-6
GN_EPS = 1e-6
ALPHA = (2 * DEPTH) ** 0.25
BETA = (8 * DEPTH) ** -0.25
N_EVEN = (DEPTH + 1) // 2
N_ODD = DEPTH // 2
AB_SPLITS = (H_A * DK, H_A * DK, H_A * DV, H_A * DV, H_B * DH, H_B * DH, H_B * DH, H_B)
AB_IN = sum(AB_SPLITS)
AB_MIX = H_A * DV + H_B * DH
C_SPLITS = (Q_LORA, KV_LORA, ROPE_D)
C_IN = Q_LORA + KV_LORA + ROPE_D
C_MIX = H_C * V_DIM
F32 = jnp.float32

kernel_name = 'hybrid_retention_fox_mla_decoder_step'


def _split(h, sizes):
    parts, o = [], 0
    for s in sizes:
        parts.append(h[..., o:o + s])
        o += s
    return parts


def layer_norm(x, g, b):
    xf = x.astype(F32)
    mu = jnp.mean(xf, -1, keepdims=True)
    var = jnp.mean(jnp.square(xf - mu), -1, keepdims=True)
    return ((xf - mu) * lax.rsqrt(var + LN_EPS) * g.astype(F32) + b.astype(F32)).astype(x.dtype)


def rms_norm(x, g):
    xf = x.astype(F32)
    return (xf * lax.rsqrt(jnp.mean(xf * xf, -1, keepdims=True) + RMS_EPS) * g.astype(F32)).astype(x.dtype)


def rope(x, pos):
    half = x.shape[-1] // 2
    inv = ROPE_BASE ** (-jnp.arange(half, dtype=F32) / half)
    ang = pos.astype(F32)[:, None] * inv[None, :]
    cos = jnp.cos(ang)[None, :, None, :]
    sin = jnp.sin(ang)[None, :, None, :]
    xf = x.astype(F32)
    x1, x2 = xf[..., :half], xf[..., half:]
    return jnp.concatenate([x1 * cos - x2 * sin, x1 * sin + x2 * cos], -1).astype(x.dtype)


def gather_pages(pool, layer, page_table):
    rows = pool[layer, page_table]
    return rows.reshape((page_table.shape[0], -1) + pool.shape[3:])


def squared_relu_mlp(x, w1, w2):
    h = jax.nn.relu(x @ w1)
    return (h * h) @ w2


def retention_log_decay():
    return jnp.log(1.0 - 2.0 ** (-5.0 - jnp.arange(H_A, dtype=F32)))


def retention_chunk(q, k, v, state):
    q, k, v = q.astype(F32), k.astype(F32), v.astype(F32)
    L = q.shape[1]
    lg = retention_log_decay()
    idx = jnp.arange(L, dtype=F32)
    diff = idx[:, None] - idx[None, :]
    decay = jnp.where(diff >= 0, jnp.exp(lg[:, None, None] * jnp.maximum(diff, 0.0)[None]), 0.0)
    scores = jnp.einsum('bihd,bjhd->bhij', q, k) * decay[None]
    q_dec = q * jnp.exp((idx[:, None] + 1.0) * lg[None, :])[None, :, :, None]
    out = jnp.einsum('bhij,bjhv->bihv', scores, v) + jnp.einsum('bihd,bhdv->bihv', q_dec, state)
    k_dec = k * jnp.exp((L - 1.0 - idx)[:, None] * lg[None, :])[None, :, :, None]
    new_state = jnp.exp(L * lg)[None, :, None, None] * state + jnp.einsum('bjhd,bjhv->bhdv', k_dec, v)
    return out, new_state


def retention_prompt(q, k, v):
    B, S = q.shape[:2]
    nc = S // RET_CHUNK
    qc = q.reshape(B, nc, RET_CHUNK, H_A, DK).swapaxes(0, 1)
    kc = k.reshape(B, nc, RET_CHUNK, H_A, DK).swapaxes(0, 1)
    vc = v.reshape(B, nc, RET_CHUNK, H_A, DV).swapaxes(0, 1)

    def step(st, qkv):
        o, st = retention_chunk(qkv[0], qkv[1], qkv[2], st)
        return st, o

    s0 = jnp.zeros((B, H_A, DK, DV), F32)
    s_fin, o = lax.scan(step, s0, (qc, kc, vc))
    return o.swapaxes(0, 1).reshape(B, S, H_A, DV), s_fin


def causal_block_attention(q, k, v, scale, c=None):
    B, S, H = q.shape[:3]
    nb = S // BLOCK_Q
    qb = q.reshape(B, nb, BLOCK_Q, H, q.shape[-1]).swapaxes(0, 1)
    starts = jnp.arange(nb) * BLOCK_Q
    kpos = jnp.arange(S)
    if c is None:
        xs = (qb, starts)
        cT = None
    else:
        cT = c.transpose(0, 2, 1)
        cb = cT.reshape(B, H, nb, BLOCK_Q).transpose(2, 0, 1, 3)
        xs = (qb, starts, cb)

    def block(args):
        qi, st = args[0], args[1]
        s = jnp.einsum('bqhd,bkhd->bhqk', qi, k).astype(F32) * scale
        if cT is not None:
            s = s + args[2][..., None] - cT[:, :, None, :]
        qpos = st + jnp.arange(BLOCK_Q)
        s = jnp.where(kpos[None, :] <= qpos[:, None], s, -jnp.inf)
        p = jax.nn.softmax(s, axis=-1).astype(v.dtype)
        return jnp.einsum('bhqk,bkhd->bqhd', p, v)

    o = lax.map(block, xs)
    return o.swapaxes(0, 1).reshape(B, S, H, v.shape[-1])


def ab_project(x, pos, w_in, b_f):
    B, S, _ = x.shape
    rq, rk, rv, rg, fq, fk, fv, ff = _split(x @ w_in, AB_SPLITS)
    rq = rope(rq.reshape(B, S, H_A, DK), pos)
    rk = rope(rk.reshape(B, S, H_A, DK), pos) * (DK ** -0.5)
    rv = rv.reshape(B, S, H_A, DV)
    fq = fq.reshape(B, S, H_B, DH)
    fk = fk.reshape(B, S, H_B, DH)
    fv = fv.reshape(B, S, H_B, DH)
    logf = jax.nn.log_sigmoid((ff + b_f).astype(F32))
    return rq, rk, rv, rg, fq, fk, fv, logf


def ab_merge(ret_o, rg, fox_o, gn_g, w_out):
    B, S = ret_o.shape[:2]
    mu = jnp.mean(ret_o, -1, keepdims=True)
    var = jnp.mean(jnp.square(ret_o - mu), -1, keepdims=True)
    y = ((ret_o - mu) * lax.rsqrt(var + GN_EPS)).reshape(B, S, H_A * DV) * gn_g.astype(F32)
    y = (y * jax.nn.silu(rg.astype(F32))).astype(rg.dtype)
    return jnp.concatenate([y, fox_o.reshape(B, S, H_B * DH).astype(rg.dtype)], -1) @ w_out


def fox_sample(q, k, v, logf, k_past, v_past, logf_past):
    DS = q.shape[1]
    P = k_past.shape[1]
    scale = DH ** -0.5
    c_past = jnp.cumsum(logf_past.astype(F32), axis=1)
    c_new = c_past[:, -1:, :] + jnp.cumsum(logf, axis=1)
    cq = c_new.transpose(0, 2, 1)[..., None]
    s_p = jnp.einsum('bqhd,bkhd->bhqk', q, k_past).astype(F32) * scale + cq - c_past.transpose(0, 2, 1)[:, :, None, :]
    s_n = jnp.einsum('bqhd,bkhd->bhqk', q, k).astype(F32) * scale + cq - c_new.transpose(0, 2, 1)[:, :, None, :]
    s_n = jnp.where(jnp.tril(jnp.ones((DS, DS), bool)), s_n, -jnp.inf)
    p = jax.nn.softmax(jnp.concatenate([s_p, s_n], -1), axis=-1).astype(v.dtype)
    return (jnp.einsum('bhqk,bkhd->bqhd', p[..., :P], v_past)
            + jnp.einsum('bhqk,bkhd->bqhd', p[..., P:], v))


def mla_project(x, pos, w_in, q_g, w_qb, kv_g):
    B, S, _ = x.shape
    qa, kva, kpe = _split(x @ w_in, C_SPLITS)
    q = (rms_norm(qa, q_g) @ w_qb).reshape(B, S, H_C, NOPE + ROPE_D)
    q_nope = q[..., :NOPE]
    q_pe = rope(q[..., NOPE:], pos)
    ckv = rms_norm(kva, kv_g)
    kpe = rope(kpe[:, :, None, :], pos)[:, :, 0, :]
    return q_nope, q_pe, ckv, kpe


def mla_prompt(q_nope, q_pe, ckv, kpe, w_kvb):
    B, S = ckv.shape[:2]
    kv = (ckv @ w_kvb).reshape(B, S, H_C, NOPE + V_DIM)
    k = jnp.concatenate([kv[..., :NOPE], jnp.broadcast_to(kpe[:, :, None, :], (B, S, H_C, ROPE_D))], -1)
    q = jnp.concatenate([q_nope, q_pe], -1)
    o = causal_block_attention(q, k, kv[..., NOPE:], (NOPE + ROPE_D) ** -0.5)
    return o.reshape(B, S, C_MIX)


def mla_sample(q_nope, q_pe, ckv, kpe, ckv_past, kpe_past, w_kvb):
    DB, DS = q_nope.shape[:2]
    P = ckv_past.shape[1]
    w = w_kvb.reshape(KV_LORA, H_C, NOPE + V_DIM)
    w_uk, w_uv = w[..., :NOPE], w[..., NOPE:]
    scale = (NOPE + ROPE_D) ** -0.5
    q_lat = jnp.einsum('bqhn,chn->bqhc', q_nope, w_uk)
    s_p = (jnp.einsum('bqhc,bkc->bhqk', q_lat, ckv_past) + jnp.einsum('bqhr,bkr->bhqk', q_pe, kpe_past)).astype(F32) * scale
    s_n = (jnp.einsum('bqhc,bkc->bhqk', q_lat, ckv) + jnp.einsum('bqhr,bkr->bhqk', q_pe, kpe)).astype(F32) * scale
    s_n = jnp.where(jnp.tril(jnp.ones((DS, DS), bool)), s_n, -jnp.inf)
    p = jax.nn.softmax(jnp.concatenate([s_p, s_n], -1), axis=-1).astype(ckv.dtype)
    o_lat = jnp.einsum('bhqk,bkc->bqhc', p[..., :P], ckv_past) + jnp.einsum('bhqk,bkc->bqhc', p[..., P:], ckv)
    return jnp.einsum('bqhc,chv->bqhv', o_lat, w_uv).reshape(DB, DS, C_MIX)


def setup_inputs(seed: int = 0) -> dict:
    key = jax.random.key(seed)
    ks = iter(jax.random.split(key, 32))

    def nrm(shape, scale):
        return jax.random.normal(next(ks), shape, F32) * scale

    n_pages = PAST_LEN // PAGE_SIZE
    n_used = DEC_BATCH * n_pages
    n_pool = n_used + max(1, n_used // 4)
    x_prompt = nrm((BATCH, SEQ, D_MODEL), 1.0)
    x_sample = nrm((DEC_BATCH, DEC_SEQ, D_MODEL), 1.0)
    state_ret = nrm((N_EVEN, DEC_BATCH, H_A, DK, DV), 0.5)
    cache_fox_k = nrm((N_EVEN, n_pool, PAGE_SIZE, H_B, DH), 1.0)
    cache_fox_v = nrm((N_EVEN, n_pool, PAGE_SIZE, H_B, DH), 1.0)
    cache_fox_logf = jax.nn.log_sigmoid(nrm((N_EVEN, n_pool, PAGE_SIZE, H_B), 1.5) + 3.5)
    cache_mla_ckv = nrm((N_ODD, n_pool, PAGE_SIZE, KV_LORA), 1.0)
    cache_mla_kpe = nrm((N_ODD, n_pool, PAGE_SIZE, ROPE_D), 1.0)
    page_table = jax.random.permutation(next(ks), n_pool)[:n_used].reshape(DEC_BATCH, n_pages).astype(jnp.int32)
    w_in_ab = nrm((N_EVEN, D_MODEL, AB_IN), D_MODEL ** -0.5)
    ret_gn_g = 1.0 + nrm((N_EVEN, H_A * DV), 0.02)
    fox_b_f = jax.random.uniform(next(ks), (N_EVEN, H_B), F32, 1.0, 6.0)
    w_out_ab = nrm((N_EVEN, AB_MIX, D_MODEL), BETA * AB_MIX ** -0.5)
    w_in_c = nrm((N_ODD, D_MODEL, C_IN), D_MODEL ** -0.5)
    mla_q_norm_g = 1.0 + nrm((N_ODD, Q_LORA), 0.02)
    mla_w_qb = nrm((N_ODD, Q_LORA, H_C * (NOPE + ROPE_D)), Q_LORA ** -0.5)
    mla_kv_norm_g = 1.0 + nrm((N_ODD, KV_LORA), 0.02)
    mla_w_kvb = nrm((N_ODD, KV_LORA, H_C * (NOPE + V_DIM)), KV_LORA ** -0.5)
    w_out_c = nrm((N_ODD, C_MIX, D_MODEL), BETA * C_MIX ** -0.5)
    ln_mix_g = 1.0 + nrm((DEPTH, D_MODEL), 0.02)
    ln_mix_b = nrm((DEPTH, D_MODEL), 0.02)
    ln_mlp_g = 1.0 + nrm((DEPTH, D_MODEL), 0.02)
    ln_mlp_b = nrm((DEPTH, D_MODEL), 0.02)
    mlp_w1 = nrm((DEPTH, D_MODEL, D_FF), BETA * D_MODEL ** -0.5)
    mlp_w2 = nrm((DEPTH, D_FF, D_MODEL), BETA * D_FF ** -0.5)
    return {'x_prompt': x_prompt, 'x_sample': x_sample, 'state_ret': state_ret,
            'cache_fox_k': cache_fox_k, 'cache_fox_v': cache_fox_v, 'cache_fox_logf': cache_fox_logf,
            'cache_mla_ckv': cache_mla_ckv, 'cache_mla_kpe': cache_mla_kpe, 'page_table': page_table,
            'w_in_ab': w_in_ab, 'ret_gn_g': ret_gn_g, 'fox_b_f': fox_b_f, 'w_out_ab': w_out_ab,
            'w_in_c': w_in_c, 'mla_q_norm_g': mla_q_norm_g, 'mla_w_qb': mla_w_qb,
            'mla_kv_norm_g': mla_kv_norm_g, 'mla_w_kvb': mla_w_kvb, 'w_out_c': w_out_c,
            'ln_mix_g': ln_mix_g, 'ln_mix_b': ln_mix_b, 'ln_mlp_g': ln_mlp_g, 'ln_mlp_b': ln_mlp_b,
            'mlp_w1': mlp_w1, 'mlp_w2': mlp_w2}


def reference(x_prompt, x_sample, state_ret, cache_fox_k, cache_fox_v, cache_fox_logf,
              cache_mla_ckv, cache_mla_kpe, page_table, w_in_ab, ret_gn_g, fox_b_f, w_out_ab,
              w_in_c, mla_q_norm_g, mla_w_qb, mla_kv_norm_g, mla_w_kvb, w_out_c,
              ln_mix_g, ln_mix_b, ln_mlp_g, ln_mlp_b, mlp_w1, mlp_w2):
    past_len = page_table.shape[1] * cache_fox_k.shape[2]
    pos_p = jnp.arange(x_prompt.shape[1])
    pos_s = past_len + jnp.arange(x_sample.shape[1])
    hp, hs = x_prompt, x_sample
    ret_p, ret_s = [], []
    fk_p, fv_p, fl_p, fk_s, fv_s, fl_s = [], [], [], [], [], []
    ck_p, kp_p, ck_s, kp_s = [], [], [], []
    for l in range(DEPTH):
        i = l // 2
        if l % 2 == 0:
            rq, rk, rv, rg, fq, fk, fv, lf = ab_project(hp, pos_p, w_in_ab[i], fox_b_f[i])
            ro, s_fin = retention_prompt(rq, rk, rv)
            fo = causal_block_attention(fq, fk, fv, DH ** -0.5, jnp.cumsum(lf, axis=1))
            mix_p = ab_merge(ro, rg, fo, ret_gn_g[i], w_out_ab[i])
            ret_p.append(s_fin)
            fk_p.append(fk)
            fv_p.append(fv)
            fl_p.append(lf)
            rq, rk, rv, rg, fq, fk, fv, lf = ab_project(hs, pos_s, w_in_ab[i], fox_b_f[i])
            ro, s_new = retention_chunk(rq, rk, rv, state_ret[i].astype(F32))
            fo = fox_sample(fq, fk, fv, lf,
                            gather_pages(cache_fox_k, i, page_table),
                            gather_pages(cache_fox_v, i, page_table),
                            gather_pages(cache_fox_logf, i, page_table))
            mix_s = ab_merge(ro, rg, fo, ret_gn_g[i], w_out_ab[i])
            ret_s.append(s_new)
            fk_s.append(fk)
            fv_s.append(fv)
            fl_s.append(lf)
        else:
            qn, qr, ckv, kpe = mla_project(hp, pos_p, w_in_c[i], mla_q_norm_g[i], mla_w_qb[i], mla_kv_norm_g[i])
            mix_p = mla_prompt(qn, qr, ckv, kpe, mla_w_kvb[i]) @ w_out_c[i]
            ck_p.append(ckv)
            kp_p.append(kpe)
            qn, qr, ckv, kpe = mla_project(hs, pos_s, w_in_c[i], mla_q_norm_g[i], mla_w_qb[i], mla_kv_norm_g[i])
            mix_s = mla_sample(qn, qr, ckv, kpe,
                               gather_pages(cache_mla_ckv, i, page_table),
                               gather_pages(cache_mla_kpe, i, page_table), mla_w_kvb[i]) @ w_out_c[i]
            ck_s.append(ckv)
            kp_s.append(kpe)
        hp = layer_norm(ALPHA * hp + mix_p, ln_mix_g[l], ln_mix_b[l])
        hs = layer_norm(ALPHA * hs + mix_s, ln_mix_g[l], ln_mix_b[l])
        hp = layer_norm(ALPHA * hp + squared_relu_mlp(hp, mlp_w1[l], mlp_w2[l]), ln_mlp_g[l], ln_mlp_b[l])
        hs = layer_norm(ALPHA * hs + squared_relu_mlp(hs, mlp_w1[l], mlp_w2[l]), ln_mlp_g[l], ln_mlp_b[l])
    return (hp, hs,
            jnp.stack(ret_p), jnp.stack(ret_s),
            jnp.stack(fk_p), jnp.stack(fv_p), jnp.stack(fl_p),
            jnp.stack(fk_s), jnp.stack(fv_s), jnp.stack(fl_s),
            jnp.stack(ck_p), jnp.stack(kp_p), jnp.stack(ck_s), jnp.stack(kp_s))
```

```python
import functools
import math

import jax
import jax.numpy as jnp
from jax import lax
from jax.experimental import pallas as pl
from jax.experimental.pallas import tpu as pltpu

F32 = jnp.float32
BF16 = jnp.bfloat16

D_MODEL = 1024
DEPTH = 2
H_A, DK, DV = 4, 128, 128
RET_CHUNK = 128
H_B, DH = 8, 64
H_C, NOPE, ROPE_D, V_DIM = 16, 64, 32, 64
Q_LORA, KV_LORA = 768, 256
D_FF = 4 * D_MODEL
ROPE_BASE = 10000.0
LN_EPS = 1e-5
RMS_EPS = 1e-6
GN_EPS = 1e-6
ALPHA = (2 * DEPTH) ** 0.25
AB_RET = H_A * DK
AB_FOX = H_B * DH
LANES = 128
LOG2E = math.log2(math.e)
VMEM_LIMIT = 56 * 1024 * 1024
NEG_INF = float("-inf")
HIGHEST = lax.Precision.HIGHEST


def _cparams(*sem):
    return pltpu.CompilerParams(dimension_semantics=sem, vmem_limit_bytes=VMEM_LIMIT)


def _dot(a, b):
    return jnp.dot(a, b, preferred_element_type=F32)


def _dot_nt(a, b):
    return lax.dot_general(a, b, (((1,), (1,)), ((), ())), preferred_element_type=F32)


def _layer_norm(z, g, b):
    mu = jnp.mean(z, -1, keepdims=True)
    var = jnp.mean(jnp.square(z - mu), -1, keepdims=True)
    return (z - mu) * lax.rsqrt(var + LN_EPS) * g + b


def _rms_norm(z, g):
    return z * lax.rsqrt(jnp.mean(z * z, -1, keepdims=True) + RMS_EPS) * g


def _log_sigmoid(x):
    return jnp.minimum(x, 0.0) - jnp.log1p(jnp.exp(-jnp.abs(x)))


def _full(shape):
    nd = len(shape)
    return pl.BlockSpec(shape, lambda *_: (0,) * nd)


def _proj_ab_kernel(x_ref, w_ref, wf_ref, wft_ref, cos_ref, sin_ref, bf_ref, bfc_ref,
                    rq_ref, rk_ref, rv_ref, rg_ref, fq_ref, fk_ref, fv_ref, fk16_ref, fv16_ref,
                    lf_ref, lft_ref):
    xb = x_ref[...].astype(BF16)
    cos = cos_ref[...]
    sin = sin_ref[...]

    def slab(j):
        return _dot(xb, w_ref[:, j * AB_RET:(j + 1) * AB_RET])

    def rope(h):
        parts = []
        for g in range(H_A):
            seg = h[:, g * DK:(g + 1) * DK]
            parts.append(seg * cos + pltpu.roll(seg, DK // 2, 1) * sin)
        return jnp.concatenate(parts, axis=1)

    rq_ref[...] = rope(slab(0))
    rk_ref[...] = rope(slab(1)) * (DK ** -0.5)
    rv_ref[...] = slab(2)
    rg_ref[...] = slab(3)
    fq_ref[...] = (slab(4) * (DH ** -0.5 * LOG2E)).astype(BF16)
    fk = slab(5)
    fv = slab(6)
    fk_ref[...] = fk
    fv_ref[...] = fv
    fk16_ref[...] = fk.astype(BF16)
    fv16_ref[...] = fv.astype(BF16)
    ff = _dot(xb, wf_ref[...]) + bf_ref[...]
    lf_ref[...] = _log_sigmoid(ff)[:, :H_B]
    fft = _dot_nt(wft_ref[...], xb)[:H_B, :] + bfc_ref[...]
    lft_ref[...] = _log_sigmoid(fft)


def _proj_ab(x, w_main, w_f, w_ft, cos, sin, bf_row, bf_col, tm):
    m = x.shape[0]
    ntab = cos.shape[0] // tm
    row = lambda i: (i, 0)
    tab = lambda i: (i % ntab, 0)
    wide = lambda dt: jax.ShapeDtypeStruct((m, AB_RET), dt)
    outs = (wide(F32), wide(F32), wide(F32), wide(F32), wide(BF16), wide(F32), wide(F32),
            wide(BF16), wide(BF16),
            jax.ShapeDtypeStruct((m, H_B), F32), jax.ShapeDtypeStruct((H_B, m), F32))
    wspec = pl.BlockSpec((tm, AB_RET), row)
    return pl.pallas_call(
        _proj_ab_kernel,
        grid=(m // tm,),
        in_specs=[pl.BlockSpec((tm, D_MODEL), row), _full(w_main.shape), _full(w_f.shape),
                  _full(w_ft.shape), pl.BlockSpec((tm, LANES), tab), pl.BlockSpec((tm, LANES), tab),
                  _full(bf_row.shape), _full(bf_col.shape)],
        out_specs=[wspec] * 9 + [pl.BlockSpec((tm, H_B), row), pl.BlockSpec((H_B, tm), lambda i: (0, i))],
        out_shape=outs,
        compiler_params=_cparams("parallel"),
    )(x, w_main, w_f, w_ft, cos, sin, bf_row, bf_col)


def _cumsum_kernel(lf_ref, lft_ref, c_ref, ct_ref, car_ref, cart_ref):
    @pl.when(pl.program_id(1) == 0)
    def _():
        car_ref[...] = jnp.zeros_like(car_ref)
        cart_ref[...] = jnp.zeros_like(cart_ref)

    tc = lf_ref.shape[0]
    r = lax.broadcasted_iota(jnp.int32, (tc, tc), 0)
    c = lax.broadcasted_iota(jnp.int32, (tc, tc), 1)
    lower = (c <= r).astype(F32)
    upper = (r <= c).astype(F32)
    cs = jnp.dot(lower, lf_ref[...], precision=HIGHEST, preferred_element_type=F32) + car_ref[...]
    cst = jnp.dot(lft_ref[...], upper, precision=HIGHEST, preferred_element_type=F32) + cart_ref[...]
    c_ref[...] = cs * LOG2E
    ct_ref[...] = cst * LOG2E
    car_ref[...] = cs[tc - 1:tc, :]
    cart_ref[...] = cst[:, tc - 1:tc]


def _cumsum_logf(lf, lft, batch, seq, tc):
    nc = seq // tc
    return pl.pallas_call(
        _cumsum_kernel,
        grid=(batch, nc),
        in_specs=[pl.BlockSpec((tc, H_B), lambda b, j: (b * nc + j, 0)),
                  pl.BlockSpec((H_B, tc), lambda b, j: (0, b * nc + j))],
        out_specs=[pl.BlockSpec((tc, H_B), lambda b, j: (b * nc + j, 0)),
                   pl.BlockSpec((H_B, tc), lambda b, j: (0, b * nc + j))],
        out_shape=(jax.ShapeDtypeStruct(lf.shape, F32), jax.ShapeDtypeStruct(lft.shape, F32)),
        scratch_shapes=[pltpu.VMEM((1, H_B), F32), pltpu.VMEM((H_B, 1), F32)],
        compiler_params=_cparams("arbitrary", "arbitrary"),
    )(lf, lft)


def _gn_gate(o, gate, gn):
    mu = jnp.mean(o, -1, keepdims=True)
    var = jnp.mean(jnp.square(o - mu), -1, keepdims=True)
    y = (o - mu) * lax.rsqrt(var + GN_EPS) * gn
    return (y * (gate / (1.0 + jnp.exp(-gate)))).astype(BF16)


def _ret_prompt_kernel(q_ref, k_ref, v_ref, g_ref, dec_ref, qd_ref, kd_ref, dl_ref, gn_ref,
                       y_ref, sfin_ref, st_ref, *, nsub):
    @pl.when(pl.program_id(1) == 0)
    def _():
        st_ref[...] = jnp.zeros_like(st_ref)

    for s in range(nsub):
        rows = slice(s * RET_CHUNK, (s + 1) * RET_CHUNK)
        for h in range(H_A):
            cols = slice(h * DK, (h + 1) * DK)
            q16 = q_ref[rows, cols].astype(BF16)
            k = k_ref[rows, cols]
            v16 = v_ref[rows, cols].astype(BF16)
            st = st_ref[h]
            sc = _dot_nt(q16, k.astype(BF16)) * dec_ref[h]
            o = _dot(sc.astype(BF16), v16) + qd_ref[h] * _dot(q16, st.astype(BF16))
            kdt = (k * kd_ref[h]).T.astype(BF16)
            st_ref[h] = dl_ref[h] * st + _dot(kdt, v16)
            y_ref[rows, cols] = _gn_gate(o, g_ref[rows, cols], gn_ref[:, cols])
    sfin_ref[0] = st_ref[...]


def _ret_prompt(rq, rk, rv, rg, tabs, gn, batch, seq, nsub):
    tr = nsub * RET_CHUNK
    nc = seq // tr
    row = lambda b, c: (b * nc + c, 0)
    blk = pl.BlockSpec((tr, AB_RET), row)
    dec, qd, kd, dl = tabs
    return pl.pallas_call(
        functools.partial(_ret_prompt_kernel, nsub=nsub),
        grid=(batch, nc),
        in_specs=[blk, blk, blk, blk, _full(dec.shape), _full(qd.shape), _full(kd.shape),
                  _full(dl.shape), _full(gn.shape)],
        out_specs=[blk, pl.BlockSpec((1, H_A, DK, DV), lambda b, c: (b, 0, 0, 0))],
        out_shape=(jax.ShapeDtypeStruct(rq.shape, BF16),
                   jax.ShapeDtypeStruct((batch, H_A, DK, DV), F32)),
        scratch_shapes=[pltpu.VMEM((H_A, DK, DV), F32)],
        compiler_params=_cparams("arbitrary", "arbitrary"),
    )(rq, rk, rv, rg, dec, qd, kd, dl, gn)


def _ret_sample_kernel(q_ref, k_ref, v_ref, g_ref, st_ref, dec_ref, qd_ref, kd_ref, dl_ref, gn_ref,
                       y_ref, snew_ref, *, bb, ds):
    n = bb * ds
    rb = lax.broadcasted_iota(jnp.int32, (n, DV), 0) // ds
    for h in range(H_A):
        cols = slice(h * DK, (h + 1) * DK)
        q = q_ref[:, cols]
        k = k_ref[:, cols]
        v = v_ref[:, cols]
        v16 = v.astype(BF16)
        sc = _dot_nt(q.astype(BF16), k.astype(BF16)) * dec_ref[h]
        o_intra = _dot(sc.astype(BF16), v16)
        kdt = (k * kd_ref[h]).T.astype(BF16)
        dl = dl_ref[h]

        def per_seq(b, o_cross, h=h, q=q, v=v, kdt=kdt, dl=dl):
            st = st_ref[b, h]
            qb = jnp.where(rb == b, q, 0.0).astype(BF16)
            vb = jnp.where(rb == b, v, 0.0).astype(BF16)
            snew_ref[b, h] = dl * st + _dot(kdt, vb)
            return o_cross + _dot(qb, st.astype(BF16))

        o_cross = lax.fori_loop(0, bb, per_seq, jnp.zeros((n, DV), F32))
        o = o_intra + qd_ref[h] * o_cross
        y_ref[:, cols] = _gn_gate(o, g_ref[:, cols], gn_ref[:, cols])


def _ret_sample(rq, rk, rv, rg, state, tabs, gn, bb, ds):
    nb = state.shape[0]
    n = bb * ds
    row = lambda i: (i, 0)
    blk = pl.BlockSpec((n, AB_RET), row)
    sblk = pl.BlockSpec((bb, H_A, DK, DV), lambda i: (i, 0, 0, 0))
    dec, qd, kd, dl = tabs
    return pl.pallas_call(
        functools.partial(_ret_sample_kernel, bb=bb, ds=ds),
        grid=(nb // bb,),
        in_specs=[blk, blk, blk, blk, sblk, _full(dec.shape), _full(qd.shape), _full(kd.shape),
                  _full(dl.shape), _full(gn.shape)],
        out_specs=[blk, sblk],
        out_shape=(jax.ShapeDtypeStruct(rq.shape, BF16), jax.ShapeDtypeStruct(state.shape, F32)),
        compiler_params=_cparams("parallel"),
    )(rq, rk, rv, rg, state, dec, qd, kd, dl, gn)


def _flash_kernel(*refs, hg, dq, dv, tq, tk, bias):
    if bias:
        q_ref, k_ref, v_ref, cq_ref, ck_ref, o_ref = refs
    else:
        q_ref, k_ref, v_ref, o_ref = refs
    i = pl.program_id(2)
    ratio = tq // tk
    row = lax.broadcasted_iota(jnp.int32, (tq, tk), 0)
    col = lax.broadcasted_iota(jnp.int32, (tq, tk), 1)

    for h in range(hg):
        q = q_ref[:, h * dq:(h + 1) * dq]
        if bias:
            cq = cq_ref[0, :, h:h + 1]

        def step(j, carry, diag, h=h, q=q):
            m, l, acc = carry
            ks = pl.ds(pl.multiple_of(j * tk, tk), tk)
            k = k_ref[ks, h * dq:(h + 1) * dq]
            v = v_ref[ks, h * dv:(h + 1) * dv]
            s = _dot_nt(q, k)
            if bias:
                s = s + (cq - ck_ref[0, 0, h:h + 1, ks])
            if diag is not None:
                s = jnp.where(col + diag * tk <= row, s, NEG_INF)
            m_new = jnp.maximum(m, jnp.max(s, -1, keepdims=True))
            alpha = jnp.exp2(m - m_new)
            p = jnp.exp2(s - m_new)
            l = alpha * l + jnp.sum(p, -1, keepdims=True)
            acc = alpha * acc + _dot(p.astype(BF16), v)
            return m_new, l, acc

        carry = (jnp.full((tq, 1), NEG_INF, F32), jnp.zeros((tq, 1), F32), jnp.zeros((tq, dv), F32))
        carry = lax.fori_loop(0, i * ratio, functools.partial(step, diag=None), carry)
        for d in range(ratio):
            carry = step(i * ratio + d, carry, d)
        _, l, acc = carry
        o_ref[:, h * dv:(h + 1) * dv] = (acc / l).astype(BF16)


def _flash(q, k, v, cq, ct, batch, seq, heads, dq, dv, hg, tq, tk):
    bias = cq is not None
    nq = seq // tq
    ngrp = heads // hg
    in_specs = [pl.BlockSpec((tq, hg * dq), lambda b, g, i: (b * nq + i, g)),
                pl.BlockSpec((seq, hg * dq), lambda b, g, i: (b, g)),
                pl.BlockSpec((seq, hg * dv), lambda b, g, i: (b, g))]
    args = [q, k, v]
    if bias:
        in_specs += [pl.BlockSpec((1, tq, hg), lambda b, g, i: (g, b * nq + i, 0)),
                     pl.BlockSpec((1, 1, hg, seq), lambda b, g, i: (b, g, 0, 0))]
        args += [cq, ct]
    return pl.pallas_call(
        functools.partial(_flash_kernel, hg=hg, dq=dq, dv=dv, tq=tq, tk=tk, bias=bias),
        grid=(batch, ngrp, nq),
        in_specs=in_specs,
        out_specs=pl.BlockSpec((tq, hg * dv), lambda b, g, i: (b * nq + i, g)),
        out_shape=jax.ShapeDtypeStruct((batch * seq, heads * dv), BF16),
        compiler_params=_cparams("parallel", "parallel", "arbitrary"),
    )(*args)


def _outproj_ln_kernel(*refs, n_in):
    a_refs = refs[:n_in]
    w_ref, x_ref, g_ref, b_ref, o_ref = refs[n_in:]
    acc = None
    off = 0
    for a_ref in a_refs:
        ka = a_ref.shape[1]
        d = _dot(a_ref[...], w_ref[off:off + ka, :])
        acc = d if acc is None else acc + d
        off += ka
    o_ref[...] = _layer_norm(ALPHA * x_ref[...] + acc, g_ref[...], b_ref[...])


def _outproj_ln(parts, w, x, g, b, tm):
    m = x.shape[0]
    row = lambda i: (i, 0)
    in_specs = [pl.BlockSpec((tm, p.shape[1]), row) for p in parts]
    in_specs += [_full(w.shape), pl.BlockSpec((tm, D_MODEL), row), _full(g.shape), _full(b.shape)]
    return pl.pallas_call(
        functools.partial(_outproj_ln_kernel, n_in=len(parts)),
        grid=(m // tm,),
        in_specs=in_specs,
        out_specs=pl.BlockSpec((tm, D_MODEL), row),
        out_shape=jax.ShapeDtypeStruct((m, D_MODEL), F32),
        compiler_params=_cparams("parallel"),
    )(*parts, w, x, g, b)


def _mlp_ln_kernel(x_ref, w1_ref, w2_ref, g_ref, b_ref, o_ref, xb_ref, acc_ref):
    f = pl.program_id(1)

    @pl.when(f == 0)
    def _():
        xb_ref[...] = x_ref[...].astype(BF16)
        acc_ref[...] = jnp.zeros_like(acc_ref)

    h = jnp.maximum(_dot(xb_ref[...], w1_ref[...]), 0.0)
    acc_ref[...] += _dot((h * h).astype(BF16), w2_ref[...])

    @pl.when(f == pl.num_programs(1) - 1)
    def _():
        o_ref[...] = _layer_norm(ALPHA * x_ref[...] + acc_ref[...], g_ref[...], b_ref[...])


def _mlp_ln(x, w1, w2, g, b, tm, tf):
    m = x.shape[0]
    return pl.pallas_call(
        _mlp_ln_kernel,
        grid=(m // tm, D_FF // tf),
        in_specs=[pl.BlockSpec((tm, D_MODEL), lambda i, f: (i, 0)),
                  pl.BlockSpec((D_MODEL, tf), lambda i, f: (0, f)),
                  pl.BlockSpec((tf, D_MODEL), lambda i, f: (f, 0)),
                  _full(g.shape), _full(b.shape)],
        out_specs=pl.BlockSpec((tm, D_MODEL), lambda i, f: (i, 0)),
        out_shape=jax.ShapeDtypeStruct((m, D_MODEL), F32),
        scratch_shapes=[pltpu.VMEM((tm, D_MODEL), BF16), pltpu.VMEM((tm, D_MODEL), F32)],
        compiler_params=_cparams("parallel", "arbitrary"),
    )(x, w1, w2, g, b)


def _rot_lanes(x, c, s1, s2):
    return x * c + pltpu.roll(x, LANES - ROPE_D // 2, 1) * s1 + pltpu.roll(x, ROPE_D // 2, 1) * s2


def _proj_c_kernel(x_ref, win_ref, wpe_ref, qg_ref, kvg_ref, wqb_ref, wk_ref, wv_ref, place_ref,
                   qc_ref, qs1_ref, qs2_ref, kc_ref, ks1_ref, ks2_ref,
                   q_ref, k_ref, v_ref, ckv_ref, kpe_ref):
    xb = x_ref[...].astype(BF16)
    qa = _dot(xb, win_ref[:, :Q_LORA])
    kva = _dot(xb, win_ref[:, Q_LORA:Q_LORA + KV_LORA])
    kpe = _dot(xb, wpe_ref[...])
    qn = _rms_norm(qa, qg_ref[...]).astype(BF16)
    qc, qs1, qs2 = qc_ref[...], qs1_ref[...], qs2_ref[...]
    for h in range(H_C):
        seg = _dot(qn, wqb_ref[:, h * LANES:(h + 1) * LANES])
        q_ref[:, h * LANES:(h + 1) * LANES] = _rot_lanes(seg, qc, qs1, qs2).astype(BF16)
    ckv = _rms_norm(kva, kvg_ref[...])
    ckv_ref[...] = ckv
    ckv16 = ckv.astype(BF16)
    kpe_rot = _rot_lanes(kpe, kc_ref[...], ks1_ref[...], ks2_ref[...])
    kpe_ref[...] = kpe_rot[:, :ROPE_D]
    kpe16 = kpe_rot.astype(BF16)
    k_ref[...] = (_dot(ckv16, wk_ref[...]) + _dot(kpe16, place_ref[...])).astype(BF16)
    v_ref[...] = _dot(ckv16, wv_ref[...]).astype(BF16)


def _proj_c(x, wts, tabs, tm):
    m = x.shape[0]
    win, wpe, qg, kvg, wqb, wk, wv, place = wts
    ntab = tabs[0].shape[0] // tm
    row = lambda i: (i, 0)
    tab = pl.BlockSpec((tm, LANES), lambda i: (i % ntab, 0))
    return pl.pallas_call(
        _proj_c_kernel,
        grid=(m // tm,),
        in_specs=[pl.BlockSpec((tm, D_MODEL), row)] + [_full(w.shape) for w in wts] + [tab] * 6,
        out_specs=[pl.BlockSpec((tm, H_C * LANES), row), pl.BlockSpec((tm, H_C * LANES), row),
                   pl.BlockSpec((tm, H_C * V_DIM), row), pl.BlockSpec((tm, KV_LORA), row),
                   pl.BlockSpec((tm, ROPE_D), row)],
        out_shape=(jax.ShapeDtypeStruct((m, H_C * LANES), BF16), jax.ShapeDtypeStruct((m, H_C * LANES), BF16),
                   jax.ShapeDtypeStruct((m, H_C * V_DIM), BF16), jax.ShapeDtypeStruct((m, KV_LORA), F32),
                   jax.ShapeDtypeStruct((m, ROPE_D), F32)),
        compiler_params=_cparams("parallel"),
    )(x, *wts, *tabs)


def _head_mm_kernel(x_ref, w_ref, o_ref, *, hp, din, dout):
    for j in range(hp):
        o_ref[:, j * dout:(j + 1) * dout] = _dot(x_ref[:, j * din:(j + 1) * din], w_ref[j]).astype(BF16)


def _head_mm(x, w, hp):
    m = x.shape[0]
    heads, din, dout = w.shape
    return pl.pallas_call(
        functools.partial(_head_mm_kernel, hp=hp, din=din, dout=dout),
        grid=(heads // hp,),
        in_specs=[pl.BlockSpec((m, hp * din), lambda h: (0, h)),
                  pl.BlockSpec((hp, din, dout), lambda h: (h, 0, 0))],
        out_specs=pl.BlockSpec((m, hp * dout), lambda h: (0, h)),
        out_shape=jax.ShapeDtypeStruct((m, heads * dout), BF16),
        compiler_params=_cparams("parallel"),
    )(x, w)


def _fox_decode_kernel(pt_ref, q_ref, kn_ref, vn_ref, lfc_ref, lfr_ref, sel_ref, *rest, gp, ds):
    k_refs = rest[:gp]
    v_refs = rest[gp:2 * gp]
    lf_refs = rest[2 * gp:3 * gp]
    o_ref, m_ref, l_ref, acc_ref, car_ref, cnq_ref = rest[3 * gp:]
    pg = pl.program_id(1)
    nrow = H_B * ds
    page = k_refs[0].shape[2]
    ncol = page * H_B
    q = q_ref[0]

    @pl.when(pg == 0)
    def _():
        n = ds * H_B
        r = lax.broadcasted_iota(jnp.int32, (nrow, n), 0)
        c = lax.broadcasted_iota(jnp.int32, (nrow, n), 1)
        m_col = jnp.where((r % H_B == c % H_B) & (r <= c), 1.0, 0.0)
        cn_row = jnp.sum(lfc_ref[0] * m_col, axis=0, keepdims=True)
        m_row = jnp.where((r // ds == c // ds) & (c <= r), 1.0, 0.0)
        cn_col = jnp.sum(m_row * lfr_ref[0], axis=1, keepdims=True)
        cnq_ref[...] = cn_col * LOG2E
        s = _dot_nt(q, kn_ref[0]) + (cn_col - cn_row) * LOG2E
        s = jnp.where((c % H_B == r // ds) & (c // H_B <= r % ds), s, NEG_INF)
        m = jnp.max(s, -1, keepdims=True)
        p = jnp.exp2(s - m)
        m_ref[...] = m
        l_ref[...] = jnp.sum(p, -1, keepdims=True)
        acc_ref[...] = _dot(p.astype(BF16), vn_ref[0])
        car_ref[...] = jnp.zeros_like(car_ref)

    valid = (lax.broadcasted_iota(jnp.int32, (nrow, ncol), 0) // ds
             == lax.broadcasted_iota(jnp.int32, (nrow, ncol), 1) % H_B)
    sub = lax.broadcasted_iota(jnp.int32, (H_B, LANES), 0)
    for j in range(gp):
        x = lf_refs[j][0]
        x1 = x.astype(BF16).astype(F32)
        x2 = (x - x1).astype(BF16).astype(F32)
        x3 = (x - x1 - x2).astype(BF16).astype(F32)
        xs = jnp.concatenate([x1, x2, x3, jnp.zeros_like(x)], axis=0).astype(BF16)
        yz = _dot(xs, sel_ref[...])
        yz = yz[0:8] + yz[8:16] + yz[16:24]
        same = yz[:, :LANES]
        later = yz[:, LANES:]
        suffix = same
        for sh in (1, 2, 4):
            suffix = suffix + jnp.where(sub + sh < H_B, pltpu.roll(suffix, H_B - sh, 0), 0.0)
        rev = car_ref[...] + (suffix - same) + later
        car_ref[...] += suffix[0:1, :]
        brow = jnp.concatenate([rev[a:a + 1, :] for a in range(H_B)], axis=1) * LOG2E
        k2 = k_refs[j][0, 0].reshape(ncol, DH).astype(BF16)
        v2 = v_refs[j][0, 0].reshape(ncol, DH).astype(BF16)
        s = jnp.where(valid, _dot_nt(q, k2) + brow + cnq_ref[...], NEG_INF)
        m = m_ref[...]
        m_new = jnp.maximum(m, jnp.max(s, -1, keepdims=True))
        alpha = jnp.exp2(m - m_new)
        p = jnp.exp2(s - m_new)
        m_ref[...] = m_new
        l_ref[...] = alpha * l_ref[...] + jnp.sum(p, -1, keepdims=True)
        acc_ref[...] = alpha * acc_ref[...] + _dot(p.astype(BF16), v2)

    @pl.when(pg == pl.num_programs(1) - 1)
    def _():
        o_ref[0] = (acc_ref[...] / l_ref[...]).astype(BF16)


def _fox_decode(page_table, layer, q, kn, vn, lf_col, lf_row, sel, cache_k, cache_v, cache_lf, gp):
    nb, nrow, _ = q.shape
    ds = nrow // H_B
    npages = page_table.shape[1]
    n_pool, page = cache_k.shape[1], cache_k.shape[2]

    def pidx(j):
        return lambda b, pg, pt: (layer, pt[b, npages - 1 - (pg * gp + j)], 0, 0, 0)

    def lidx(j):
        return lambda b, pg, pt: (layer * n_pool + pt[b, npages - 1 - (pg * gp + j)], 0, 0)

    per_b = lambda b, pg, pt: (b, 0, 0)
    in_specs = [pl.BlockSpec((1,) + a.shape[1:], per_b) for a in (q, kn, vn, lf_col, lf_row)]
    in_specs += [pl.BlockSpec(sel.shape, lambda b, pg, pt: (0, 0))]
    in_specs += [pl.BlockSpec((1, 1, page, H_B, DH), pidx(j)) for j in range(gp)]
    in_specs += [pl.BlockSpec((1, 1, page, H_B, DH), pidx(j)) for j in range(gp)]
    in_specs += [pl.BlockSpec((1,) + cache_lf.shape[1:], lidx(j)) for j in range(gp)]
    grid_spec = pltpu.PrefetchScalarGridSpec(
        num_scalar_prefetch=1,
        grid=(nb, npages // gp),
        in_specs=in_specs,
        out_specs=pl.BlockSpec((1, nrow, DH), per_b),
        scratch_shapes=[pltpu.VMEM((nrow, 1), F32), pltpu.VMEM((nrow, 1), F32),
                        pltpu.VMEM((nrow, DH), F32), pltpu.VMEM((1, LANES), F32),
                        pltpu.VMEM((nrow, 1), F32)])
    return pl.pallas_call(
        functools.partial(_fox_decode_kernel, gp=gp, ds=ds),
        grid_spec=grid_spec,
        out_shape=jax.ShapeDtypeStruct((nb, nrow, DH), BF16),
        compiler_params=_cparams("parallel", "arbitrary"),
    )(page_table, q, kn, vn, lf_col, lf_row, sel, *([cache_k] * gp), *([cache_v] * gp), *([cache_lf] * gp))


def _logf_selectors():
    a = jnp.arange(LANES)[:, None]
    b = jnp.arange(LANES)[None, :]
    same = a % H_B == b % H_B
    return jnp.concatenate([same, same & (a > b)], axis=1).astype(BF16)


def _mla_decode_kernel(pt_ref, ql_ref, qp_ref, cn_ref, pn_ref, *rest, gp, ds):
    c_refs = rest[:gp]
    p_refs = rest[gp:2 * gp]
    o_ref, m_ref, l_ref, acc_ref = rest[2 * gp:]
    pg = pl.program_id(1)
    ql = ql_ref[0]
    qp = qp_ref[0]
    nrow = ql.shape[0]

    @pl.when(pg == 0)
    def _():
        cn = cn_ref[0]
        npad = cn.shape[0]
        s = _dot_nt(ql, cn) + _dot_nt(qp, pn_ref[0])
        rq = lax.broadcasted_iota(jnp.int32, (nrow, npad), 0) // H_C
        ck = lax.broadcasted_iota(jnp.int32, (nrow, npad), 1)
        s = jnp.where(ck <= rq, s, NEG_INF)
        m = jnp.max(s, -1, keepdims=True)
        p = jnp.exp2(s - m)
        m_ref[...] = m
        l_ref[...] = jnp.sum(p, -1, keepdims=True)
        acc_ref[...] = _dot(p.astype(BF16), cn)

    for j in range(gp):
        c16 = c_refs[j][0].astype(BF16)
        s = _dot_nt(ql, c16) + _dot_nt(qp, p_refs[j][0].astype(BF16))
        m = m_ref[...]
        m_new = jnp.maximum(m, jnp.max(s, -1, keepdims=True))
        alpha = jnp.exp2(m - m_new)
        p = jnp.exp2(s - m_new)
        m_ref[...] = m_new
        l_ref[...] = alpha * l_ref[...] + jnp.sum(p, -1, keepdims=True)
        acc_ref[...] = alpha * acc_ref[...] + _dot(p.astype(BF16), c16)

    @pl.when(pg == pl.num_programs(1) - 1)
    def _():
        o_ref[0] = (acc_ref[...] / l_ref[...]).astype(BF16)


def _mla_decode(page_table, base, ql, qp, cn, pn, cache_c, cache_p, gp):
    nb, nrow, _ = ql.shape
    npages = page_table.shape[1]
    page = cache_c.shape[1]

    def pidx(j):
        return lambda b, pg, pt: (base + pt[b, pg * gp + j], 0, 0)

    per_b = lambda b, pg, pt: (b, 0, 0)
    in_specs = [pl.BlockSpec((1,) + a.shape[1:], per_b) for a in (ql, qp, cn, pn)]
    in_specs += [pl.BlockSpec((1, page, KV_LORA), pidx(j)) for j in range(gp)]
    in_specs += [pl.BlockSpec((1, page, ROPE_D), pidx(j)) for j in range(gp)]
    grid_spec = pltpu.PrefetchScalarGridSpec(
        num_scalar_prefetch=1,
        grid=(nb, npages // gp),
        in_specs=in_specs,
        out_specs=pl.BlockSpec((1, nrow, KV_LORA), per_b),
        scratch_shapes=[pltpu.VMEM((nrow, 1), F32), pltpu.VMEM((nrow, 1), F32),
                        pltpu.VMEM((nrow, KV_LORA), F32)])
    return pl.pallas_call(
        functools.partial(_mla_decode_kernel, gp=gp, ds=ds_of(nrow)),
        grid_spec=grid_spec,
        out_shape=jax.ShapeDtypeStruct((nb, nrow, KV_LORA), BF16),
        compiler_params=_cparams("parallel", "arbitrary"),
    )(page_table, ql, qp, cn, pn, *([cache_c] * gp), *([cache_p] * gp))


def ds_of(nrow):
    return nrow // H_C


def _ret_rope_tables(pos):
    half = DK // 2
    inv = ROPE_BASE ** (-jnp.arange(half, dtype=F32) / half)
    ang = pos.astype(F32)[:, None] * inv[None, :]
    cos, sin = jnp.cos(ang), jnp.sin(ang)
    return jnp.concatenate([cos, cos], -1), jnp.concatenate([-sin, sin], -1)


def _mla_rope_tables(pos, lane0, passthrough, scale):
    half = ROPE_D // 2
    inv = ROPE_BASE ** (-jnp.arange(half, dtype=F32) / half)
    ang = pos.astype(F32)[:, None] * inv[None, :]
    cos, sin = jnp.cos(ang) * scale, jnp.sin(ang) * scale
    n = pos.shape[0]
    z = lambda w: jnp.zeros((n, w), F32)
    lead = jnp.full((n, lane0), scale if passthrough else 0.0, F32)
    tail = z(LANES - lane0 - ROPE_D)
    c = jnp.concatenate([lead, cos, cos, tail], -1)
    s1 = jnp.concatenate([z(lane0), -sin, z(half), tail], -1)
    s2 = jnp.concatenate([z(lane0), z(half), sin, tail], -1)
    return c, s1, s2


def _retention_tables(length, reps):
    lg = jnp.log(1.0 - 2.0 ** (-5.0 - jnp.arange(H_A, dtype=F32)))
    idx = jnp.arange(length, dtype=F32)
    diff = idx[:, None] - idx[None, :]
    decay = jnp.where(diff >= 0, jnp.exp(lg[:, None, None] * jnp.maximum(diff, 0.0)[None]), 0.0)
    qd = jnp.exp((idx[:, None] + 1.0) * lg[None, :]).T
    kd = jnp.exp((length - 1.0 - idx)[:, None] * lg[None, :]).T
    dl = jnp.exp(length * lg)
    if reps > 1:
        eye = jnp.eye(reps, dtype=F32)
        decay = jnp.einsum('ab,hij->haibj', eye, decay).reshape(H_A, reps * length, reps * length)
        qd = jnp.tile(qd, (1, reps))
        kd = jnp.tile(kd, (1, reps))
    n = reps * length
    qd = jnp.broadcast_to(qd[:, :, None], (H_A, n, DV))
    kd = jnp.broadcast_to(kd[:, :, None], (H_A, n, DK))
    dl = jnp.broadcast_to(dl[:, None, None], (H_A, DK, DV))
    return decay, qd, kd, dl


def _prep_ab_weights(w_in, b_f):
    main = w_in[:, :7 * AB_RET].astype(BF16)
    wf = w_in[:, 7 * AB_RET:]
    w_f = jnp.pad(wf, ((0, 0), (0, LANES - H_B))).astype(BF16)
    w_ft = jnp.pad(wf.T, ((0, 16 - H_B), (0, 0))).astype(BF16)
    bf_row = jnp.pad(b_f, (0, LANES - H_B))[None, :]
    bf_col = b_f[:, None]
    return main, w_f, w_ft, bf_row, bf_col


def _prep_c_weights(w_in, q_g, w_qb, kv_g, w_kvb):
    win = w_in[:, :Q_LORA + KV_LORA].astype(BF16)
    wpe = jnp.pad(w_in[:, Q_LORA + KV_LORA:], ((0, 0), (0, LANES - ROPE_D))).astype(BF16)
    wqb = w_qb.reshape(Q_LORA, H_C, NOPE + ROPE_D)
    wqb = jnp.pad(wqb, ((0, 0), (0, 0), (0, LANES - NOPE - ROPE_D))).reshape(Q_LORA, H_C * LANES).astype(BF16)
    wkv = w_kvb.reshape(KV_LORA, H_C, NOPE + V_DIM)
    w_uk, w_uv = wkv[..., :NOPE], wkv[..., NOPE:]
    wk = jnp.pad(w_uk, ((0, 0), (0, 0), (0, LANES - NOPE))).reshape(KV_LORA, H_C * LANES).astype(BF16)
    wv = w_uv.reshape(KV_LORA, H_C * V_DIM).astype(BF16)
    r = jnp.arange(LANES)[:, None]
    c = jnp.arange(H_C * LANES)[None, :]
    place = ((r < ROPE_D) & (c % LANES == NOPE + r)).astype(BF16)
    w_abs = jnp.pad(w_uk.transpose(1, 2, 0), ((0, 0), (0, LANES - NOPE), (0, 0))).astype(BF16)
    w_val = w_uv.transpose(1, 0, 2).astype(BF16)
    return (win, wpe, q_g[None, :], kv_g[None, :], wqb, wk, wv, place), w_abs, w_val


TM = 512
TM_MLP = 1024
TF_MLP = 1024
TQ, TK = 512, 512
TC = 512
RET_SUB = 4
RET_BB = 16
FOX_HG = 4
MLA_HG = 4
PAGES_PER_STEP = 8
NEW_PAD = 16


def kernel(x_prompt, x_sample, state_ret, cache_fox_k, cache_fox_v, cache_fox_logf, cache_mla_ckv,
           cache_mla_kpe, page_table, w_in_ab, ret_gn_g, fox_b_f, w_out_ab, w_in_c, mla_q_norm_g,
           mla_w_qb, mla_kv_norm_g, mla_w_kvb, w_out_c, ln_mix_g, ln_mix_b, ln_mlp_g, ln_mlp_b,
           mlp_w1, mlp_w2):
    batch, seq, _ = x_prompt.shape
    nb, ds, _ = x_sample.shape
    n_pool, page = cache_fox_k.shape[1], cache_fox_k.shape[2]
    past_len = page_table.shape[1] * page
    mp, ms = batch * seq, nb * ds
    pos_p = jnp.arange(seq)
    pos_s = past_len + jnp.arange(ds)
    pos_s_rows = jnp.tile(pos_s, nb)

    hp = x_prompt.reshape(mp, D_MODEL)
    hs = x_sample.reshape(ms, D_MODEL)
    tms = min(TM, ms)
    outs = {k: [] for k in ("ret_p", "ret_s", "fk_p", "fv_p", "fl_p", "fk_s", "fv_s", "fl_s",
                            "ck_p", "kp_p", "ck_s", "kp_s")}

    for l in range(DEPTH):
        i = l // 2
        row = lambda a: a[None, :]
        if l % 2 == 0:
            wts = _prep_ab_weights(w_in_ab[i], fox_b_f[i])
            w_out = w_out_ab[i].astype(BF16)
            gn = ret_gn_g[i][None, :]
            cos, sin = _ret_rope_tables(pos_p)
            rq, rk, rv, rg, fq, fk, fv, fk16, fv16, lf, lft = _proj_ab(hp, *wts[:3], cos, sin, *wts[3:], TM)
            y, s_fin = _ret_prompt(rq, rk, rv, rg, _retention_tables(RET_CHUNK, 1), gn, batch, seq, RET_SUB)
            c, ct = _cumsum_logf(lf, lft, batch, seq, TC)
            cq = c.reshape(mp, H_B // FOX_HG, FOX_HG).transpose(1, 0, 2)
            ck = ct.reshape(H_B // FOX_HG, FOX_HG, batch, seq).transpose(2, 0, 1, 3)
            fo = _flash(fq, fk16, fv16, cq, ck, batch, seq, H_B, DH, DH, FOX_HG, TQ, TK)
            mix_parts_p = [y, fo]
            outs["ret_p"].append(s_fin)
            outs["fk_p"].append(fk.reshape(batch, seq, H_B, DH))
            outs["fv_p"].append(fv.reshape(batch, seq, H_B, DH))
            outs["fl_p"].append(lf.reshape(batch, seq, H_B))
            cos, sin = _ret_rope_tables(pos_s_rows)
            rq, rk, rv, rg, fq, fk, fv, fk16, fv16, lf, lft = _proj_ab(hs, *wts[:3], cos, sin, *wts[3:], tms)
            y, s_new = _ret_sample(rq, rk, rv, rg, state_ret[i], _retention_tables(ds, RET_BB), gn, RET_BB, ds)
            hq = lambda a: a.reshape(nb, ds, H_B, -1).transpose(0, 2, 1, 3)
            fo = _fox_decode(page_table, i, hq(fq).reshape(nb, H_B * ds, DH),
                             fk16.reshape(nb, ds * H_B, DH), fv16.reshape(nb, ds * H_B, DH),
                             lf.reshape(nb, ds * H_B, 1), hq(lf).reshape(nb, 1, H_B * ds),
                             _logf_selectors(), cache_fox_k, cache_fox_v,
                             cache_fox_logf.reshape(-1, page * H_B // LANES, LANES), PAGES_PER_STEP)
            fo = fo.reshape(nb, H_B, ds, DH).transpose(0, 2, 1, 3).reshape(ms, AB_FOX)
            mix_parts_s = [y, fo]
            outs["ret_s"].append(s_new)
            outs["fk_s"].append(fk.reshape(nb, ds, H_B, DH))
            outs["fv_s"].append(fv.reshape(nb, ds, H_B, DH))
            outs["fl_s"].append(lf.reshape(nb, ds, H_B))
        else:
            wts, w_abs, w_val = _prep_c_weights(w_in_c[i], mla_q_norm_g[i], mla_w_qb[i],
                                                mla_kv_norm_g[i], mla_w_kvb[i])
            w_out = w_out_c[i].astype(BF16)
            qscale = (NOPE + ROPE_D) ** -0.5 * LOG2E
            tabs = _mla_rope_tables(pos_p, NOPE, True, qscale) + _mla_rope_tables(pos_p, 0, False, 1.0)
            q, k, v, ckv, kpe = _proj_c(hp, wts, tabs, TM)
            o = _flash(q, k, v, None, None, batch, seq, H_C, LANES, V_DIM, MLA_HG, TQ, TK)
            mix_parts_p = [o]
            outs["ck_p"].append(ckv.reshape(batch, seq, KV_LORA))
            outs["kp_p"].append(kpe.reshape(batch, seq, ROPE_D))
            tabs = _mla_rope_tables(pos_s_rows, NOPE, True, qscale) + _mla_rope_tables(pos_s_rows, 0, False, 1.0)
            q, _, _, ckv, kpe = _proj_c(hs, wts, tabs, tms)
            q_lat = _head_mm(q, w_abs, 1).reshape(nb, ds * H_C, KV_LORA)
            q_pe = q.reshape(nb, ds * H_C, LANES)[:, :, NOPE:NOPE + ROPE_D]
            padn = lambda a: jnp.pad(a.reshape(nb, ds, -1), ((0, 0), (0, NEW_PAD - ds), (0, 0))).astype(BF16)
            o_lat = _mla_decode(page_table, i * n_pool, q_lat, q_pe, padn(ckv), padn(kpe),
                                cache_mla_ckv.reshape(-1, page, KV_LORA),
                                cache_mla_kpe.reshape(-1, page, ROPE_D), PAGES_PER_STEP)
            o = _head_mm(o_lat.reshape(ms, H_C * KV_LORA), w_val, 2)
            mix_parts_s = [o]
            outs["ck_s"].append(ckv.reshape(nb, ds, KV_LORA))
            outs["kp_s"].append(kpe.reshape(nb, ds, ROPE_D))

        w1 = mlp_w1[l].astype(BF16)
        w2 = mlp_w2[l].astype(BF16)
        hp = _outproj_ln(mix_parts_p, w_out, hp, row(ln_mix_g[l]), row(ln_mix_b[l]), TM)
        hs = _outproj_ln(mix_parts_s, w_out, hs, row(ln_mix_g[l]), row(ln_mix_b[l]), tms)
        hp = _mlp_ln(hp, w1, w2, row(ln_mlp_g[l]), row(ln_mlp_b[l]), TM_MLP, TF_MLP)
        hs = _mlp_ln(hs, w1, w2, row(ln_mlp_g[l]), row(ln_mlp_b[l]), min(TM_MLP, ms), TF_MLP)

    st = lambda k: jnp.stack(outs[k])
    return (hp.reshape(batch, seq, D_MODEL), hs.reshape(nb, ds, D_MODEL),
            st("ret_p"), st("ret_s"), st("fk_p"), st("fv_p"), st("fl_p"),
            st("fk_s"), st("fv_s"), st("fl_s"), st("ck_p"), st("kp_p"), st("ck_s"), st("kp_s"))
```

```python
import functools
import math

import jax
import jax.numpy as jnp
from jax import lax
from jax.experimental import pallas as pl
from jax.experimental.pallas import tpu as pltpu

F32 = jnp.float32
BF16 = jnp.bfloat16

D_MODEL = 1024
DEPTH = 2
H_A, DK, DV = 4, 128, 128
RET_CHUNK = 128
H_B, DH = 8, 64
H_C, NOPE, ROPE_D, V_DIM = 16, 64, 32, 64
Q_LORA, KV_LORA = 768, 256
D_FF = 4 * D_MODEL
ROPE_BASE = 10000.0
LN_EPS = 1e-5
RMS_EPS = 1e-6
GN_EPS = 1e-6
ALPHA = (2 * DEPTH) ** 0.25
AB_RET = H_A * DK
AB_FOX = H_B * DH
LANES = 128
LOG2E = math.log2(math.e)
VMEM_LIMIT = 56 * 1024 * 1024
NEG_INF = float("-inf")
HIGHEST = lax.Precision.HIGHEST


def _cparams(*sem):
    return pltpu.CompilerParams(dimension_semantics=sem, vmem_limit_bytes=VMEM_LIMIT)


def _dot(a, b):
    return jnp.dot(a, b, preferred_element_type=F32)


def _dot_nt(a, b):
    return lax.dot_general(a, b, (((1,), (1,)), ((), ())), preferred_element_type=F32)


def _layer_norm(z, g, b):
    mu = jnp.mean(z, -1, keepdims=True)
    var = jnp.mean(jnp.square(z - mu), -1, keepdims=True)
    return (z - mu) * lax.rsqrt(var + LN_EPS) * g + b


def _rms_norm(z, g):
    return z * lax.rsqrt(jnp.mean(z * z, -1, keepdims=True) + RMS_EPS) * g


def _log_sigmoid(x):
    return jnp.minimum(x, 0.0) - jnp.log1p(jnp.exp(-jnp.abs(x)))


def _full(shape):
    nd = len(shape)
    return pl.BlockSpec(shape, lambda *_: (0,) * nd)


def _proj_ab_kernel(x_ref, w_ref, wf_ref, wft_ref, cos_ref, sin_ref, bf_ref, bfc_ref,
                    rq_ref, rk_ref, rv_ref, rg_ref, fq_ref, fk_ref, fv_ref, fk16_ref, fv16_ref,
                    lf_ref, lft_ref):
    xb = x_ref[...].astype(BF16)
    cos = cos_ref[...]
    sin = sin_ref[...]

    def slab(j):
        return _dot(xb, w_ref[:, j * AB_RET:(j + 1) * AB_RET])

    def rope(h):
        parts = []
        for g in range(H_A):
            seg = h[:, g * DK:(g + 1) * DK]
            parts.append(seg * cos + pltpu.roll(seg, DK // 2, 1) * sin)
        return jnp.concatenate(parts, axis=1)

    rq_ref[...] = rope(slab(0))
    rk_ref[...] = rope(slab(1)) * (DK ** -0.5)
    rv_ref[...] = slab(2)
    rg_ref[...] = slab(3)
    fq_ref[...] = (slab(4) * (DH ** -0.5 * LOG2E)).astype(BF16)
    fk = slab(5)
    fv = slab(6)
    fk_ref[...] = fk
    fv_ref[...] = fv
    fk16_ref[...] = fk.astype(BF16)
    fv16_ref[...] = fv.astype(BF16)
    ff = _dot(xb, wf_ref[...]) + bf_ref[...]
    lf_ref[...] = _log_sigmoid(ff)[:, :H_B]
    fft = _dot_nt(wft_ref[...], xb)[:H_B, :] + bfc_ref[...]
    lft_ref[...] = _log_sigmoid(fft)


def _proj_ab(x, w_main, w_f, w_ft, cos, sin, bf_row, bf_col, tm):
    m = x.shape[0]
    ntab = cos.shape[0] // tm
    row = lambda i: (i, 0)
    tab = lambda i: (i % ntab, 0)
    wide = lambda dt: jax.ShapeDtypeStruct((m, AB_RET), dt)
    outs = (wide(F32), wide(F32), wide(F32), wide(F32), wide(BF16), wide(F32), wide(F32),
            wide(BF16), wide(BF16),
            jax.ShapeDtypeStruct((m, H_B), F32), jax.ShapeDtypeStruct((H_B, m), F32))
    wspec = pl.BlockSpec((tm, AB_RET), row)
    return pl.pallas_call(
        _proj_ab_kernel,
        grid=(m // tm,),
        in_specs=[pl.BlockSpec((tm, D_MODEL), row), _full(w_main.shape), _full(w_f.shape),
                  _full(w_ft.shape), pl.BlockSpec((tm, LANES), tab), pl.BlockSpec((tm, LANES), tab),
                  _full(bf_row.shape), _full(bf_col.shape)],
        out_specs=[wspec] * 9 + [pl.BlockSpec((tm, H_B), row), pl.BlockSpec((H_B, tm), lambda i: (0, i))],
        out_shape=outs,
        compiler_params=_cparams("parallel"),
    )(x, w_main, w_f, w_ft, cos, sin, bf_row, bf_col)


def _cumsum_kernel(lf_ref, lft_ref, c_ref, ct_ref, car_ref, cart_ref):
    @pl.when(pl.program_id(1) == 0)
    def _():
        car_ref[...] = jnp.zeros_like(car_ref)
        cart_ref[...] = jnp.zeros_like(cart_ref)

    tc = lf_ref.shape[0]
    r = lax.broadcasted_iota(jnp.int32, (tc, tc), 0)
    c = lax.broadcasted_iota(jnp.int32, (tc, tc), 1)
    lower = (c <= r).astype(F32)
    upper = (r <= c).astype(F32)
    cs = jnp.dot(lower, lf_ref[...], precision=HIGHEST, preferred_element_type=F32) + car_ref[...]
    cst = jnp.dot(lft_ref[...], upper, precision=HIGHEST, preferred_element_type=F32) + cart_ref[...]
    c_ref[...] = cs * LOG2E
    ct_ref[...] = cst * LOG2E
    car_ref[...] = cs[tc - 1:tc, :]
    cart_ref[...] = cst[:, tc - 1:tc]


def _cumsum_logf(lf, lft, batch, seq, tc):
    nc = seq // tc
    return pl.pallas_call(
        _cumsum_kernel,
        grid=(batch, nc),
        in_specs=[pl.BlockSpec((tc, H_B), lambda b, j: (b * nc + j, 0)),
                  pl.BlockSpec((H_B, tc), lambda b, j: (0, b * nc + j))],
        out_specs=[pl.BlockSpec((tc, H_B), lambda b, j: (b * nc + j, 0)),
                   pl.BlockSpec((H_B, tc), lambda b, j: (0, b * nc + j))],
        out_shape=(jax.ShapeDtypeStruct(lf.shape, F32), jax.ShapeDtypeStruct(lft.shape, F32)),
        scratch_shapes=[pltpu.VMEM((1, H_B), F32), pltpu.VMEM((H_B, 1), F32)],
        compiler_params=_cparams("arbitrary", "arbitrary"),
    )(lf, lft)


def _gn_gate(o, gate, gn):
    mu = jnp.mean(o, -1, keepdims=True)
    var = jnp.mean(jnp.square(o - mu), -1, keepdims=True)
    y = (o - mu) * lax.rsqrt(var + GN_EPS) * gn
    return (y * (gate / (1.0 + jnp.exp(-gate)))).astype(BF16)


def _ret_prompt_kernel(q_ref, k_ref, v_ref, g_ref, dec_ref, qd_ref, kd_ref, dl_ref, gn_ref,
                       y_ref, sfin_ref, st_ref, *, nsub):
    @pl.when(pl.program_id(1) == 0)
    def _():
        st_ref[...] = jnp.zeros_like(st_ref)

    for s in range(nsub):
        rows = slice(s * RET_CHUNK, (s + 1) * RET_CHUNK)
        for h in range(H_A):
            cols = slice(h * DK, (h + 1) * DK)
            q16 = q_ref[rows, cols].astype(BF16)
            k = k_ref[rows, cols]
            v16 = v_ref[rows, cols].astype(BF16)
            st = st_ref[h]
            sc = _dot_nt(q16, k.astype(BF16)) * dec_ref[h]
            o = _dot(sc.astype(BF16), v16) + qd_ref[h] * _dot(q16, st.astype(BF16))
            kdt = (k * kd_ref[h]).T.astype(BF16)
            st_ref[h] = dl_ref[h] * st + _dot(kdt, v16)
            y_ref[rows, cols] = _gn_gate(o, g_ref[rows, cols], gn_ref[:, cols])
    sfin_ref[0] = st_ref[...]


def _ret_prompt(rq, rk, rv, rg, tabs, gn, batch, seq, nsub):
    tr = nsub * RET_CHUNK
    nc = seq // tr
    row = lambda b, c: (b * nc + c, 0)
    blk = pl.BlockSpec((tr, AB_RET), row)
    dec, qd, kd, dl = tabs
    return pl.pallas_call(
        functools.partial(_ret_prompt_kernel, nsub=nsub),
        grid=(batch, nc),
        in_specs=[blk, blk, blk, blk, _full(dec.shape), _full(qd.shape), _full(kd.shape),
                  _full(dl.shape), _full(gn.shape)],
        out_specs=[blk, pl.BlockSpec((1, H_A, DK, DV), lambda b, c: (b, 0, 0, 0))],
        out_shape=(jax.ShapeDtypeStruct(rq.shape, BF16),
                   jax.ShapeDtypeStruct((batch, H_A, DK, DV), F32)),
        scratch_shapes=[pltpu.VMEM((H_A, DK, DV), F32)],
        compiler_params=_cparams("arbitrary", "arbitrary"),
    )(rq, rk, rv, rg, dec, qd, kd, dl, gn)


def _ret_sample_kernel(q_ref, k_ref, v_ref, g_ref, st_ref, dec_ref, qd_ref, kd_ref, dl_ref, gn_ref,
                       y_ref, snew_ref, *, bb, ds):
    n = bb * ds
    rb = lax.broadcasted_iota(jnp.int32, (n, DV), 0) // ds
    for h in range(H_A):
        cols = slice(h * DK, (h + 1) * DK)
        q = q_ref[:, cols]
        k = k_ref[:, cols]
        v = v_ref[:, cols]
        v16 = v.astype(BF16)
        sc = _dot_nt(q.astype(BF16), k.astype(BF16)) * dec_ref[h]
        o_intra = _dot(sc.astype(BF16), v16)
        kdt = (k * kd_ref[h]).T.astype(BF16)
        dl = dl_ref[h]

        def per_seq(b, o_cross, h=h, q=q, v=v, kdt=kdt, dl=dl):
            st = st_ref[b, h]
            qb = jnp.where(rb == b, q, 0.0).astype(BF16)
            vb = jnp.where(rb == b, v, 0.0).astype(BF16)
            snew_ref[b, h] = dl * st + _dot(kdt, vb)
            return o_cross + _dot(qb, st.astype(BF16))

        o_cross = lax.fori_loop(0, bb, per_seq, jnp.zeros((n, DV), F32))
        o = o_intra + qd_ref[h] * o_cross
        y_ref[:, cols] = _gn_gate(o, g_ref[:, cols], gn_ref[:, cols])


def _ret_sample(rq, rk, rv, rg, state, tabs, gn, bb, ds):
    nb = state.shape[0]
    n = bb * ds
    row = lambda i: (i, 0)
    blk = pl.BlockSpec((n, AB_RET), row)
    sblk = pl.BlockSpec((bb, H_A, DK, DV), lambda i: (i, 0, 0, 0))
    dec, qd, kd, dl = tabs
    return pl.pallas_call(
        functools.partial(_ret_sample_kernel, bb=bb, ds=ds),
        grid=(nb // bb,),
        in_specs=[blk, blk, blk, blk, sblk, _full(dec.shape), _full(qd.shape), _full(kd.shape),
                  _full(dl.shape), _full(gn.shape)],
        out_specs=[blk, sblk],
        out_shape=(jax.ShapeDtypeStruct(rq.shape, BF16), jax.ShapeDtypeStruct(state.shape, F32)),
        compiler_params=_cparams("parallel"),
    )(rq, rk, rv, rg, state, dec, qd, kd, dl, gn)


def _flash_kernel(*refs, hg, dq, dv, tq, tk, bias):
    if bias:
        q_ref, k_ref, v_ref, cq_ref, ck_ref, o_ref = refs
    else:
        q_ref, k_ref, v_ref, o_ref = refs
    i = pl.program_id(2)
    ratio = tq // tk
    row = lax.broadcasted_iota(jnp.int32, (tq, tk), 0)
    col = lax.broadcasted_iota(jnp.int32, (tq, tk), 1)

    def step(j, carry, diag):
        ks = pl.ds(pl.multiple_of(j * tk, tk), tk)
        out = []
        for h in range(hg):
            m, l, acc = carry[h]
            s = _dot_nt(q_ref[:, h * dq:(h + 1) * dq], k_ref[ks, h * dq:(h + 1) * dq])
            if bias:
                s = s + (cq_ref[0, :, h:h + 1] - ck_ref[0, 0, h:h + 1, ks])
            if diag is not None:
                s = jnp.where(col + diag * tk <= row, s, NEG_INF)
            m_new = jnp.maximum(m, jnp.max(s, -1, keepdims=True))
            alpha = jnp.exp2(m - m_new)
            p = jnp.exp2(s - m_new)
            l = alpha * l + jnp.sum(p, -1, keepdims=True)
            acc = alpha * acc + _dot(p.astype(BF16), v_ref[ks, h * dv:(h + 1) * dv])
            out.append((m_new, l, acc))
        return tuple(out)

    init = (jnp.full((tq, 1), NEG_INF, F32), jnp.zeros((tq, 1), F32), jnp.zeros((tq, dv), F32))
    carry = lax.fori_loop(0, i * ratio, functools.partial(step, diag=None), (init,) * hg)
    for d in range(ratio):
        carry = step(i * ratio + d, carry, d)
    for h in range(hg):
        _, l, acc = carry[h]
        o_ref[:, h * dv:(h + 1) * dv] = (acc / l).astype(BF16)


def _flash(q, k, v, cq, ct, batch, seq, heads, dq, dv, hg, tq, tk):
    bias = cq is not None
    nq = seq // tq
    ngrp = heads // hg
    in_specs = [pl.BlockSpec((tq, hg * dq), lambda b, g, i: (b * nq + i, g)),
                pl.BlockSpec((seq, hg * dq), lambda b, g, i: (b, g)),
                pl.BlockSpec((seq, hg * dv), lambda b, g, i: (b, g))]
    args = [q, k, v]
    if bias:
        in_specs += [pl.BlockSpec((1, tq, hg), lambda b, g, i: (g, b * nq + i, 0)),
                     pl.BlockSpec((1, 1, hg, seq), lambda b, g, i: (b, g, 0, 0))]
        args += [cq, ct]
    return pl.pallas_call(
        functools.partial(_flash_kernel, hg=hg, dq=dq, dv=dv, tq=tq, tk=tk, bias=bias),
        grid=(batch, ngrp, nq),
        in_specs=in_specs,
        out_specs=pl.BlockSpec((tq, hg * dv), lambda b, g, i: (b * nq + i, g)),
        out_shape=jax.ShapeDtypeStruct((batch * seq, heads * dv), BF16),
        compiler_params=_cparams("parallel", "parallel", "arbitrary"),
    )(*args)


def _outproj_ln_kernel(*refs, n_in):
    a_refs = refs[:n_in]
    w_ref, x_ref, g_ref, b_ref, o_ref = refs[n_in:]
    acc = None
    off = 0
    for a_ref in a_refs:
        ka = a_ref.shape[1]
        d = _dot(a_ref[...], w_ref[off:off + ka, :])
        acc = d if acc is None else acc + d
        off += ka
    o_ref[...] = _layer_norm(ALPHA * x_ref[...] + acc, g_ref[...], b_ref[...])


def _outproj_ln(parts, w, x, g, b, tm):
    m = x.shape[0]
    row = lambda i: (i, 0)
    in_specs = [pl.BlockSpec((tm, p.shape[1]), row) for p in parts]
    in_specs += [_full(w.shape), pl.BlockSpec((tm, D_MODEL), row), _full(g.shape), _full(b.shape)]
    return pl.pallas_call(
        functools.partial(_outproj_ln_kernel, n_in=len(parts)),
        grid=(m // tm,),
        in_specs=in_specs,
        out_specs=pl.BlockSpec((tm, D_MODEL), row),
        out_shape=jax.ShapeDtypeStruct((m, D_MODEL), F32),
        compiler_params=_cparams("parallel"),
    )(*parts, w, x, g, b)


def _mlp_ln_kernel(x_ref, w1_ref, w2_ref, g_ref, b_ref, o_ref, xb_ref, acc_ref):
    f = pl.program_id(1)

    @pl.when(f == 0)
    def _():
        xb_ref[...] = x_ref[...].astype(BF16)
        acc_ref[...] = jnp.zeros_like(acc_ref)

    h = jnp.maximum(_dot(xb_ref[...], w1_ref[...]), 0.0)
    acc_ref[...] += _dot((h * h).astype(BF16), w2_ref[...])

    @pl.when(f == pl.num_programs(1) - 1)
    def _():
        o_ref[...] = _layer_norm(ALPHA * x_ref[...] + acc_ref[...], g_ref[...], b_ref[...])


def _mlp_ln(x, w1, w2, g, b, tm, tf):
    m = x.shape[0]
    return pl.pallas_call(
        _mlp_ln_kernel,
        grid=(m // tm, D_FF // tf),
        in_specs=[pl.BlockSpec((tm, D_MODEL), lambda i, f: (i, 0)),
                  pl.BlockSpec((D_MODEL, tf), lambda i, f: (0, f)),
                  pl.BlockSpec((tf, D_MODEL), lambda i, f: (f, 0)),
                  _full(g.shape), _full(b.shape)],
        out_specs=pl.BlockSpec((tm, D_MODEL), lambda i, f: (i, 0)),
        out_shape=jax.ShapeDtypeStruct((m, D_MODEL), F32),
        scratch_shapes=[pltpu.VMEM((tm, D_MODEL), BF16), pltpu.VMEM((tm, D_MODEL), F32)],
        compiler_params=_cparams("parallel", "arbitrary"),
    )(x, w1, w2, g, b)


def _rot_lanes(x, c, s1, s2):
    return x * c + pltpu.roll(x, LANES - ROPE_D // 2, 1) * s1 + pltpu.roll(x, ROPE_D // 2, 1) * s2


def _proj_c_kernel(x_ref, win_ref, wpe_ref, qg_ref, kvg_ref, wqb_ref, wk_ref, wv_ref, place_ref,
                   qc_ref, qs1_ref, qs2_ref, kc_ref, ks1_ref, ks2_ref,
                   q_ref, k_ref, v_ref, ckv_ref, kpe_ref):
    xb = x_ref[...].astype(BF16)
    qa = _dot(xb, win_ref[:, :Q_LORA])
    kva = _dot(xb, win_ref[:, Q_LORA:Q_LORA + KV_LORA])
    kpe = _dot(xb, wpe_ref[...])
    qn = _rms_norm(qa, qg_ref[...]).astype(BF16)
    qc, qs1, qs2 = qc_ref[...], qs1_ref[...], qs2_ref[...]
    for h in range(H_C):
        seg = _dot(qn, wqb_ref[:, h * LANES:(h + 1) * LANES])
        q_ref[:, h * LANES:(h + 1) * LANES] = _rot_lanes(seg, qc, qs1, qs2).astype(BF16)
    ckv = _rms_norm(kva, kvg_ref[...])
    ckv_ref[...] = ckv
    ckv16 = ckv.astype(BF16)
    kpe_rot = _rot_lanes(kpe, kc_ref[...], ks1_ref[...], ks2_ref[...])
    kpe_ref[...] = kpe_rot[:, :ROPE_D]
    kpe16 = kpe_rot.astype(BF16)
    k_ref[...] = (_dot(ckv16, wk_ref[...]) + _dot(kpe16, place_ref[...])).astype(BF16)
    v_ref[...] = _dot(ckv16, wv_ref[...]).astype(BF16)


def _proj_c(x, wts, tabs, tm):
    m = x.shape[0]
    win, wpe, qg, kvg, wqb, wk, wv, place = wts
    ntab = tabs[0].shape[0] // tm
    row = lambda i: (i, 0)
    tab = pl.BlockSpec((tm, LANES), lambda i: (i % ntab, 0))
    return pl.pallas_call(
        _proj_c_kernel,
        grid=(m // tm,),
        in_specs=[pl.BlockSpec((tm, D_MODEL), row)] + [_full(w.shape) for w in wts] + [tab] * 6,
        out_specs=[pl.BlockSpec((tm, H_C * LANES), row), pl.BlockSpec((tm, H_C * LANES), row),
                   pl.BlockSpec((tm, H_C * V_DIM), row), pl.BlockSpec((tm, KV_LORA), row),
                   pl.BlockSpec((tm, ROPE_D), row)],
        out_shape=(jax.ShapeDtypeStruct((m, H_C * LANES), BF16), jax.ShapeDtypeStruct((m, H_C * LANES), BF16),
                   jax.ShapeDtypeStruct((m, H_C * V_DIM), BF16), jax.ShapeDtypeStruct((m, KV_LORA), F32),
                   jax.ShapeDtypeStruct((m, ROPE_D), F32)),
        compiler_params=_cparams("parallel"),
    )(x, *wts, *tabs)


def _head_mm_kernel(x_ref, w_ref, o_ref, *, hp, din, dout):
    for j in range(hp):
        o_ref[:, j * dout:(j + 1) * dout] = _dot(x_ref[:, j * din:(j + 1) * din], w_ref[j]).astype(BF16)


def _head_mm(x, w, hp):
    m = x.shape[0]
    heads, din, dout = w.shape
    return pl.pallas_call(
        functools.partial(_head_mm_kernel, hp=hp, din=din, dout=dout),
        grid=(heads // hp,),
        in_specs=[pl.BlockSpec((m, hp * din), lambda h: (0, h)),
                  pl.BlockSpec((hp, din, dout), lambda h: (h, 0, 0))],
        out_specs=pl.BlockSpec((m, hp * dout), lambda h: (0, h)),
        out_shape=jax.ShapeDtypeStruct((m, heads * dout), BF16),
        compiler_params=_cparams("parallel"),
    )(x, w)


def _fox_decode_kernel(pt_ref, q_ref, kn_ref, vn_ref, lfr_ref, lft_ref, sel_ref, *rest, gp, ds):
    k_refs = rest[:gp]
    v_refs = rest[gp:2 * gp]
    lf_refs = rest[2 * gp:3 * gp]
    o_ref, qbd_ref, m_ref, l_ref, acc_ref, car_ref, cnq_ref = rest[3 * gp:]
    pg = pl.program_id(1)
    nrow = H_B * ds
    page = k_refs[0].shape[3]

    def expand(x):
        return jnp.concatenate([jnp.broadcast_to(x[h:h + 1, :], (ds, x.shape[1])) for h in range(H_B)], axis=0)

    @pl.when(pg == 0)
    def _():
        q = q_ref[0].astype(F32)
        rh = lax.broadcasted_iota(jnp.int32, (nrow, AB_FOX), 0) // ds
        ch = lax.broadcasted_iota(jnp.int32, (nrow, AB_FOX), 1) // DH
        qbd = jnp.where(rh == ch, jnp.concatenate([q] * H_B, axis=0), 0.0).astype(BF16)
        qbd_ref[...] = qbd
        r = lax.broadcasted_iota(jnp.int32, (nrow, nrow), 0)
        c = lax.broadcasted_iota(jnp.int32, (nrow, nrow), 1)
        m_row = jnp.where((r // ds == c // ds) & (c <= r), 1.0, 0.0)
        cn_col = jnp.sum(m_row * lfr_ref[0], axis=1, keepdims=True)
        cnq_ref[...] = cn_col * LOG2E
        lft = lft_ref[0]
        npad = lft.shape[1]
        kk = lax.broadcasted_iota(jnp.int32, (1, npad), 1)
        cnt = jnp.zeros((H_B, npad), F32)
        for k2 in range(ds):
            cnt = cnt + lft[:, k2:k2 + 1] * jnp.where(kk >= k2, 1.0, 0.0)
        s = _dot_nt(qbd, kn_ref[0]) + (cn_col - expand(cnt)) * LOG2E
        rq = lax.broadcasted_iota(jnp.int32, (nrow, npad), 0) % ds
        ck = lax.broadcasted_iota(jnp.int32, (nrow, npad), 1)
        s = jnp.where(ck <= rq, s, NEG_INF)
        m = jnp.max(s, -1, keepdims=True)
        p = jnp.exp2(s - m)
        m_ref[...] = m
        l_ref[...] = jnp.sum(p, -1, keepdims=True)
        acc_ref[...] = _dot(p.astype(BF16), vn_ref[0])
        car_ref[...] = jnp.zeros_like(car_ref)

    qbd = qbd_ref[...]
    xs = []
    for j in range(gp):
        x = lf_refs[j][0, 0]
        x1 = x.astype(BF16).astype(F32)
        x2 = (x - x1).astype(BF16).astype(F32)
        x3 = (x - x1 - x2).astype(BF16).astype(F32)
        xs += [x1, x2, x3]
    yz = _dot(jnp.concatenate(xs, axis=0).astype(BF16), sel_ref[...])
    car = car_ref[...]
    revs = []
    for j in range(gp):
        o = j * 3 * H_B
        y = yz[o:o + H_B] + yz[o + H_B:o + 2 * H_B] + yz[o + 2 * H_B:o + 3 * H_B]
        revs.append(car + y[:, :page])
        car = car + y[:, page:]
    car_ref[...] = car
    kt_all = jnp.concatenate([k_refs[j][0, 0].astype(BF16) for j in range(gp)], axis=1)
    vt_all = jnp.concatenate([v_refs[j][0, 0].astype(BF16) for j in range(gp)], axis=1)
    s = _dot(qbd, kt_all) + expand(jnp.concatenate(revs, axis=1) * LOG2E) + cnq_ref[...]
    m = m_ref[...]
    m_new = jnp.maximum(m, jnp.max(s, -1, keepdims=True))
    alpha = jnp.exp2(m - m_new)
    p = jnp.exp2(s - m_new)
    m_ref[...] = m_new
    l_ref[...] = alpha * l_ref[...] + jnp.sum(p, -1, keepdims=True)
    acc_ref[...] = alpha * acc_ref[...] + _dot_nt(p.astype(BF16), vt_all)

    @pl.when(pg == pl.num_programs(1) - 1)
    def _():
        acc = acc_ref[...] / l_ref[...]
        ch = lax.broadcasted_iota(jnp.int32, (ds, AB_FOX), 1) // DH
        out = jnp.zeros((ds, AB_FOX), F32)
        for h in range(H_B):
            out = out + jnp.where(ch == h, acc[h * ds:(h + 1) * ds, :], 0.0)
        o_ref[0] = out.astype(BF16)


def _fox_decode(page_table, layer, q, kn, vn, lf_row, lf_t, sel, cache_kt, cache_vt, cache_lft, gp):
    nb, ds, _ = q.shape
    npages = page_table.shape[1]
    page = cache_kt.shape[3]
    nrow = H_B * ds

    def pidx(j):
        return lambda b, pg, pt: (layer, pt[b, npages - 1 - (pg * gp + j)], 0, 0)

    per_b = lambda b, pg, pt: (b, 0, 0)
    in_specs = [pl.BlockSpec((1,) + a.shape[1:], per_b) for a in (q, kn, vn, lf_row, lf_t)]
    in_specs += [pl.BlockSpec(sel.shape, lambda b, pg, pt: (0, 0))]
    in_specs += [pl.BlockSpec((1, 1, AB_FOX, page), pidx(j)) for j in range(gp)]
    in_specs += [pl.BlockSpec((1, 1, AB_FOX, page), pidx(j)) for j in range(gp)]
    in_specs += [pl.BlockSpec((1, 1, H_B, page), pidx(j)) for j in range(gp)]
    grid_spec = pltpu.PrefetchScalarGridSpec(
        num_scalar_prefetch=1,
        grid=(nb, npages // gp),
        in_specs=in_specs,
        out_specs=pl.BlockSpec((1, ds, AB_FOX), per_b),
        scratch_shapes=[pltpu.VMEM((nrow, AB_FOX), BF16), pltpu.VMEM((nrow, 1), F32),
                        pltpu.VMEM((nrow, 1), F32), pltpu.VMEM((nrow, AB_FOX), F32),
                        pltpu.VMEM((H_B, page), F32), pltpu.VMEM((nrow, 1), F32)])
    return pl.pallas_call(
        functools.partial(_fox_decode_kernel, gp=gp, ds=ds),
        grid_spec=grid_spec,
        out_shape=jax.ShapeDtypeStruct((nb, ds, AB_FOX), BF16),
        compiler_params=_cparams("parallel", "arbitrary"),
    )(page_table, q, kn, vn, lf_row, lf_t, sel, *([cache_kt] * gp), *([cache_vt] * gp), *([cache_lft] * gp))


def _logf_selectors(page):
    a = jnp.arange(page)[:, None]
    b = jnp.arange(page)[None, :]
    return jnp.concatenate([a > b, jnp.ones((page, page), bool)], axis=1).astype(BF16)


def _mla_decode_kernel(pt_ref, ql_ref, qp_ref, cn_ref, pn_ref, *rest, gp, ds):
    c_refs = rest[:gp]
    p_refs = rest[gp:2 * gp]
    o_ref, m_ref, l_ref, acc_ref = rest[2 * gp:]
    pg = pl.program_id(1)
    ql = ql_ref[0]
    qp = qp_ref[0]
    nrow = ql.shape[0]

    @pl.when(pg == 0)
    def _():
        cn = cn_ref[0]
        npad = cn.shape[0]
        s = _dot_nt(ql, cn) + _dot_nt(qp, pn_ref[0])
        rq = lax.broadcasted_iota(jnp.int32, (nrow, npad), 0) // H_C
        ck = lax.broadcasted_iota(jnp.int32, (nrow, npad), 1)
        s = jnp.where(ck <= rq, s, NEG_INF)
        m = jnp.max(s, -1, keepdims=True)
        p = jnp.exp2(s - m)
        m_ref[...] = m
        l_ref[...] = jnp.sum(p, -1, keepdims=True)
        acc_ref[...] = _dot(p.astype(BF16), cn)

    c_all = jnp.concatenate([c_refs[j][0].astype(BF16) for j in range(gp)], axis=0)
    kp_all = jnp.concatenate([p_refs[j][0].astype(BF16) for j in range(gp)], axis=1)
    s = _dot_nt(ql, c_all) + _dot(qp, kp_all)
    m = m_ref[...]
    m_new = jnp.maximum(m, jnp.max(s, -1, keepdims=True))
    alpha = jnp.exp2(m - m_new)
    p = jnp.exp2(s - m_new)
    m_ref[...] = m_new
    l_ref[...] = alpha * l_ref[...] + jnp.sum(p, -1, keepdims=True)
    acc_ref[...] = alpha * acc_ref[...] + _dot(p.astype(BF16), c_all)

    @pl.when(pg == pl.num_programs(1) - 1)
    def _():
        o_ref[0] = (acc_ref[...] / l_ref[...]).astype(BF16)


def _mla_decode(page_table, base, ql, qp, cn, pn, cache_c, cache_pt, gp):
    nb, nrow, _ = ql.shape
    npages = page_table.shape[1]
    page = cache_c.shape[1]

    def pidx(j):
        return lambda b, pg, pt: (base + pt[b, pg * gp + j], 0, 0)

    per_b = lambda b, pg, pt: (b, 0, 0)
    in_specs = [pl.BlockSpec((1,) + a.shape[1:], per_b) for a in (ql, qp, cn, pn)]
    in_specs += [pl.BlockSpec((1, page, KV_LORA), pidx(j)) for j in range(gp)]
    in_specs += [pl.BlockSpec((1, ROPE_D, page), pidx(j)) for j in range(gp)]
    grid_spec = pltpu.PrefetchScalarGridSpec(
        num_scalar_prefetch=1,
        grid=(nb, npages // gp),
        in_specs=in_specs,
        out_specs=pl.BlockSpec((1, nrow, KV_LORA), per_b),
        scratch_shapes=[pltpu.VMEM((nrow, 1), F32), pltpu.VMEM((nrow, 1), F32),
                        pltpu.VMEM((nrow, KV_LORA), F32)])
    return pl.pallas_call(
        functools.partial(_mla_decode_kernel, gp=gp, ds=ds_of(nrow)),
        grid_spec=grid_spec,
        out_shape=jax.ShapeDtypeStruct((nb, nrow, KV_LORA), BF16),
        compiler_params=_cparams("parallel", "arbitrary"),
    )(page_table, ql, qp, cn, pn, *([cache_c] * gp), *([cache_pt] * gp))


def ds_of(nrow):
    return nrow // H_C


def _ret_rope_tables(pos):
    half = DK // 2
    inv = ROPE_BASE ** (-jnp.arange(half, dtype=F32) / half)
    ang = pos.astype(F32)[:, None] * inv[None, :]
    cos, sin = jnp.cos(ang), jnp.sin(ang)
    return jnp.concatenate([cos, cos], -1), jnp.concatenate([-sin, sin], -1)


def _mla_rope_tables(pos, lane0, passthrough, scale):
    half = ROPE_D // 2
    inv = ROPE_BASE ** (-jnp.arange(half, dtype=F32) / half)
    ang = pos.astype(F32)[:, None] * inv[None, :]
    cos, sin = jnp.cos(ang) * scale, jnp.sin(ang) * scale
    n = pos.shape[0]
    z = lambda w: jnp.zeros((n, w), F32)
    lead = jnp.full((n, lane0), scale if passthrough else 0.0, F32)
    tail = z(LANES - lane0 - ROPE_D)
    c = jnp.concatenate([lead, cos, cos, tail], -1)
    s1 = jnp.concatenate([z(lane0), -sin, z(half), tail], -1)
    s2 = jnp.concatenate([z(lane0), z(half), sin, tail], -1)
    return c, s1, s2


def _retention_tables(length, reps):
    lg = jnp.log(1.0 - 2.0 ** (-5.0 - jnp.arange(H_A, dtype=F32)))
    idx = jnp.arange(length, dtype=F32)
    diff = idx[:, None] - idx[None, :]
    decay = jnp.where(diff >= 0, jnp.exp(lg[:, None, None] * jnp.maximum(diff, 0.0)[None]), 0.0)
    qd = jnp.exp((idx[:, None] + 1.0) * lg[None, :]).T
    kd = jnp.exp((length - 1.0 - idx)[:, None] * lg[None, :]).T
    dl = jnp.exp(length * lg)
    if reps > 1:
        eye = jnp.eye(reps, dtype=F32)
        decay = jnp.einsum('ab,hij->haibj', eye, decay).reshape(H_A, reps * length, reps * length)
        qd = jnp.tile(qd, (1, reps))
        kd = jnp.tile(kd, (1, reps))
    n = reps * length
    qd = jnp.broadcast_to(qd[:, :, None], (H_A, n, DV))
    kd = jnp.broadcast_to(kd[:, :, None], (H_A, n, DK))
    dl = jnp.broadcast_to(dl[:, None, None], (H_A, DK, DV))
    return decay, qd, kd, dl


def _prep_ab_weights(w_in, b_f):
    main = w_in[:, :7 * AB_RET].astype(BF16)
    wf = w_in[:, 7 * AB_RET:]
    w_f = jnp.pad(wf, ((0, 0), (0, LANES - H_B))).astype(BF16)
    w_ft = jnp.pad(wf.T, ((0, 16 - H_B), (0, 0))).astype(BF16)
    bf_row = jnp.pad(b_f, (0, LANES - H_B))[None, :]
    bf_col = b_f[:, None]
    return main, w_f, w_ft, bf_row, bf_col


def _prep_c_weights(w_in, q_g, w_qb, kv_g, w_kvb):
    win = w_in[:, :Q_LORA + KV_LORA].astype(BF16)
    wpe = jnp.pad(w_in[:, Q_LORA + KV_LORA:], ((0, 0), (0, LANES - ROPE_D))).astype(BF16)
    wqb = w_qb.reshape(Q_LORA, H_C, NOPE + ROPE_D)
    wqb = jnp.pad(wqb, ((0, 0), (0, 0), (0, LANES - NOPE - ROPE_D))).reshape(Q_LORA, H_C * LANES).astype(BF16)
    wkv = w_kvb.reshape(KV_LORA, H_C, NOPE + V_DIM)
    w_uk, w_uv = wkv[..., :NOPE], wkv[..., NOPE:]
    wk = jnp.pad(w_uk, ((0, 0), (0, 0), (0, LANES - NOPE))).reshape(KV_LORA, H_C * LANES).astype(BF16)
    wv = w_uv.reshape(KV_LORA, H_C * V_DIM).astype(BF16)
    r = jnp.arange(LANES)[:, None]
    c = jnp.arange(H_C * LANES)[None, :]
    place = ((r < ROPE_D) & (c % LANES == NOPE + r)).astype(BF16)
    w_abs = jnp.pad(w_uk.transpose(1, 2, 0), ((0, 0), (0, LANES - NOPE), (0, 0))).astype(BF16)
    w_val = w_uv.transpose(1, 0, 2).astype(BF16)
    return (win, wpe, q_g[None, :], kv_g[None, :], wqb, wk, wv, place), w_abs, w_val


TM = 512
TM_MLP = 1024
TF_MLP = 1024
TQ, TK = 512, 512
TC = 512
RET_SUB = 4
RET_BB = 16
FOX_HG = 4
MLA_HG = 4
PAGES_PER_STEP = 16
NEW_PAD = 16


def kernel(x_prompt, x_sample, state_ret, cache_fox_k, cache_fox_v, cache_fox_logf, cache_mla_ckv,
           cache_mla_kpe, page_table, w_in_ab, ret_gn_g, fox_b_f, w_out_ab, w_in_c, mla_q_norm_g,
           mla_w_qb, mla_kv_norm_g, mla_w_kvb, w_out_c, ln_mix_g, ln_mix_b, ln_mlp_g, ln_mlp_b,
           mlp_w1, mlp_w2):
    batch, seq, _ = x_prompt.shape
    nb, ds, _ = x_sample.shape
    n_pool, page = cache_fox_k.shape[1], cache_fox_k.shape[2]
    past_len = page_table.shape[1] * page
    mp, ms = batch * seq, nb * ds
    pos_p = jnp.arange(seq)
    pos_s = past_len + jnp.arange(ds)
    pos_s_rows = jnp.tile(pos_s, nb)

    hp = x_prompt.reshape(mp, D_MODEL)
    hs = x_sample.reshape(ms, D_MODEL)
    tms = min(TM, ms)
    outs = {k: [] for k in ("ret_p", "ret_s", "fk_p", "fv_p", "fl_p", "fk_s", "fv_s", "fl_s",
                            "ck_p", "kp_p", "ck_s", "kp_s")}

    for l in range(DEPTH):
        i = l // 2
        row = lambda a: a[None, :]
        if l % 2 == 0:
            wts = _prep_ab_weights(w_in_ab[i], fox_b_f[i])
            w_out = w_out_ab[i].astype(BF16)
            gn = ret_gn_g[i][None, :]
            cos, sin = _ret_rope_tables(pos_p)
            rq, rk, rv, rg, fq, fk, fv, fk16, fv16, lf, lft = _proj_ab(hp, *wts[:3], cos, sin, *wts[3:], TM)
            y, s_fin = _ret_prompt(rq, rk, rv, rg, _retention_tables(RET_CHUNK, 1), gn, batch, seq, RET_SUB)
            c, ct = _cumsum_logf(lf, lft, batch, seq, TC)
            cq = c.reshape(mp, H_B // FOX_HG, FOX_HG).transpose(1, 0, 2)
            ck = ct.reshape(H_B // FOX_HG, FOX_HG, batch, seq).transpose(2, 0, 1, 3)
            fo = _flash(fq, fk16, fv16, cq, ck, batch, seq, H_B, DH, DH, FOX_HG, TQ, TK)
            mix_parts_p = [y, fo]
            outs["ret_p"].append(s_fin)
            outs["fk_p"].append(fk.reshape(batch, seq, H_B, DH))
            outs["fv_p"].append(fv.reshape(batch, seq, H_B, DH))
            outs["fl_p"].append(lf.reshape(batch, seq, H_B))
            cos, sin = _ret_rope_tables(pos_s_rows)
            rq, rk, rv, rg, fq, fk, fv, fk16, fv16, lf, lft = _proj_ab(hs, *wts[:3], cos, sin, *wts[3:], tms)
            y, s_new = _ret_sample(rq, rk, rv, rg, state_ret[i], _retention_tables(ds, RET_BB), gn, RET_BB, ds)
            padn = lambda a: jnp.pad(a.reshape(nb, ds, AB_FOX), ((0, 0), (0, NEW_PAD - ds), (0, 0)))
            lf_hq = lf.reshape(nb, ds, H_B).transpose(0, 2, 1)
            fo = _fox_decode(page_table, i, fq.reshape(nb, ds, AB_FOX), padn(fk16), padn(fv16),
                             lf_hq.reshape(nb, 1, H_B * ds),
                             jnp.pad(lf_hq, ((0, 0), (0, 0), (0, NEW_PAD - ds))),
                             _logf_selectors(page),
                             cache_fox_k.transpose(0, 1, 3, 4, 2).reshape(-1, n_pool, AB_FOX, page),
                             cache_fox_v.transpose(0, 1, 3, 4, 2).reshape(-1, n_pool, AB_FOX, page),
                             cache_fox_logf.transpose(0, 1, 3, 2), PAGES_PER_STEP)
            mix_parts_s = [y, fo.reshape(ms, AB_FOX)]
            outs["ret_s"].append(s_new)
            outs["fk_s"].append(fk.reshape(nb, ds, H_B, DH))
            outs["fv_s"].append(fv.reshape(nb, ds, H_B, DH))
            outs["fl_s"].append(lf.reshape(nb, ds, H_B))
        else:
            wts, w_abs, w_val = _prep_c_weights(w_in_c[i], mla_q_norm_g[i], mla_w_qb[i],
                                                mla_kv_norm_g[i], mla_w_kvb[i])
            w_out = w_out_c[i].astype(BF16)
            qscale = (NOPE + ROPE_D) ** -0.5 * LOG2E
            tabs = _mla_rope_tables(pos_p, NOPE, True, qscale) + _mla_rope_tables(pos_p, 0, False, 1.0)
            q, k, v, ckv, kpe = _proj_c(hp, wts, tabs, TM)
            o = _flash(q, k, v, None, None, batch, seq, H_C, LANES, V_DIM, MLA_HG, TQ, TK)
            mix_parts_p = [o]
            outs["ck_p"].append(ckv.reshape(batch, seq, KV_LORA))
            outs["kp_p"].append(kpe.reshape(batch, seq, ROPE_D))
            tabs = _mla_rope_tables(pos_s_rows, NOPE, True, qscale) + _mla_rope_tables(pos_s_rows, 0, False, 1.0)
            q, _, _, ckv, kpe = _proj_c(hs, wts, tabs, tms)
            q_lat = _head_mm(q, w_abs, 1).reshape(nb, ds * H_C, KV_LORA)
            q_pe = q.reshape(nb, ds * H_C, LANES)[:, :, NOPE:NOPE + ROPE_D]
            padn = lambda a: jnp.pad(a.reshape(nb, ds, -1), ((0, 0), (0, NEW_PAD - ds), (0, 0))).astype(BF16)
            o_lat = _mla_decode(page_table, i * n_pool, q_lat, q_pe, padn(ckv), padn(kpe),
                                cache_mla_ckv.reshape(-1, page, KV_LORA),
                                cache_mla_kpe.transpose(0, 1, 3, 2).reshape(-1, ROPE_D, page), PAGES_PER_STEP)
            o = _head_mm(o_lat.reshape(ms, H_C * KV_LORA), w_val, 2)
            mix_parts_s = [o]
            outs["ck_s"].append(ckv.reshape(nb, ds, KV_LORA))
            outs["kp_s"].append(kpe.reshape(nb, ds, ROPE_D))

        w1 = mlp_w1[l].astype(BF16)
        w2 = mlp_w2[l].astype(BF16)
        hp = _outproj_ln(mix_parts_p, w_out, hp, row(ln_mix_g[l]), row(ln_mix_b[l]), TM)
        hs = _outproj_ln(mix_parts_s, w_out, hs, row(ln_mix_g[l]), row(ln_mix_b[l]), tms)
        hp = _mlp_ln(hp, w1, w2, row(ln_mlp_g[l]), row(ln_mlp_b[l]), TM_MLP, TF_MLP)
        hs = _mlp_ln(hs, w1, w2, row(ln_mlp_g[l]), row(ln_mlp_b[l]), min(TM_MLP, ms), TF_MLP)

    st = lambda k: jnp.stack(outs[k])
    return (hp.reshape(batch, seq, D_MODEL), hs.reshape(nb, ds, D_MODEL),
            st("ret_p"), st("ret_s"), st("fk_p"), st("fv_p"), st("fl_p"),
            st("fk_s"), st("fv_s"), st("fl_s"), st("ck_p"), st("kp_p"), st("ck_s"), st("kp_s"))
```

```python
import functools
import math

import jax
import jax.numpy as jnp
from jax import lax
from jax.experimental import pallas as pl
from jax.experimental.pallas import tpu as pltpu

F32 = jnp.float32
BF16 = jnp.bfloat16

D_MODEL = 1024
DEPTH = 2
H_A, DK, DV = 4, 128, 128
RET_CHUNK = 128
H_B, DH = 8, 64
H_C, NOPE, ROPE_D, V_DIM = 16, 64, 32, 64
Q_LORA, KV_LORA = 768, 256
D_FF = 4 * D_MODEL
ROPE_BASE = 10000.0
LN_EPS = 1e-5
RMS_EPS = 1e-6
GN_EPS = 1e-6
ALPHA = (2 * DEPTH) ** 0.25
AB_RET = H_A * DK
AB_FOX = H_B * DH
FOX_PAD = H_B * 128
LANES = 128
LOG2E = math.log2(math.e)
VMEM_LIMIT = 56 * 1024 * 1024
NEG_INF = float("-inf")
HIGHEST = lax.Precision.HIGHEST


def _cparams(*sem):
    return pltpu.CompilerParams(dimension_semantics=sem, vmem_limit_bytes=VMEM_LIMIT)


def _dot(a, b):
    return jnp.dot(a, b, preferred_element_type=F32)


def _dot_nt(a, b):
    return lax.dot_general(a, b, (((1,), (1,)), ((), ())), preferred_element_type=F32)


def _layer_norm(z, g, b):
    mu = jnp.mean(z, -1, keepdims=True)
    var = jnp.mean(jnp.square(z - mu), -1, keepdims=True)
    return (z - mu) * lax.rsqrt(var + LN_EPS) * g + b


def _rms_norm(z, g):
    return z * lax.rsqrt(jnp.mean(z * z, -1, keepdims=True) + RMS_EPS) * g


def _log_sigmoid(x):
    return jnp.minimum(x, 0.0) - jnp.log1p(jnp.exp(-jnp.abs(x)))


def _full(shape):
    nd = len(shape)
    return pl.BlockSpec(shape, lambda *_: (0,) * nd)


def _proj_ab_kernel(x_ref, w_ref, wfox_ref, wf_ref, cos_ref, sin_ref, bf_ref, ones_ref,
                    rq_ref, rk_ref, rv_ref, rg_ref, fq_ref, fk_ref, fv_ref, fk16_ref, fv16_ref, lf_ref):
    xb = x_ref[...].astype(BF16)
    cos = cos_ref[...]
    sin = sin_ref[...]

    def slab(j):
        return _dot(xb, w_ref[:, j * AB_RET:(j + 1) * AB_RET])

    def fox_slab(j):
        return _dot(xb, wfox_ref[:, j * FOX_PAD:(j + 1) * FOX_PAD])

    def rope(h):
        parts = []
        for g in range(H_A):
            seg = h[:, g * DK:(g + 1) * DK]
            parts.append(seg * cos + pltpu.roll(seg, DK // 2, 1) * sin)
        return jnp.concatenate(parts, axis=1)

    rq_ref[...] = rope(slab(0))
    rk_ref[...] = rope(slab(1)) * (DK ** -0.5)
    rv_ref[...] = slab(2)
    rg_ref[...] = slab(3)
    fq_ref[...] = (fox_slab(0) * (DH ** -0.5 * LOG2E)).astype(BF16)
    fk = fox_slab(1)
    fv = fox_slab(2)
    fk_ref[...] = fk
    fv_ref[...] = fv
    fk16_ref[...] = fk.astype(BF16)
    fv16_ref[...] = (fv + ones_ref[...]).astype(BF16)
    ff = _dot(xb, wf_ref[...]) + bf_ref[...]
    lf_ref[...] = _log_sigmoid(ff)[:, :H_B]


def _proj_ab(x, w_main, w_fox, w_f, cos, sin, bf_row, ones_row, tm):
    m = x.shape[0]
    ntab = cos.shape[0] // tm
    row = lambda i: (i, 0)
    tab = lambda i: (i % ntab, 0)
    ret = lambda dt: jax.ShapeDtypeStruct((m, AB_RET), dt)
    fox = lambda dt: jax.ShapeDtypeStruct((m, FOX_PAD), dt)
    outs = (ret(F32), ret(F32), ret(F32), ret(F32), fox(BF16), fox(F32), fox(F32), fox(BF16), fox(BF16),
            jax.ShapeDtypeStruct((m, H_B), F32))
    rspec = pl.BlockSpec((tm, AB_RET), row)
    fspec = pl.BlockSpec((tm, FOX_PAD), row)
    return pl.pallas_call(
        _proj_ab_kernel,
        grid=(m // tm,),
        in_specs=[pl.BlockSpec((tm, D_MODEL), row), _full(w_main.shape), _full(w_fox.shape), _full(w_f.shape),
                  pl.BlockSpec((tm, LANES), tab), pl.BlockSpec((tm, LANES), tab),
                  _full(bf_row.shape), _full(ones_row.shape)],
        out_specs=[rspec] * 4 + [fspec] * 5 + [pl.BlockSpec((tm, H_B), row)],
        out_shape=outs,
        compiler_params=_cparams("parallel"),
    )(x, w_main, w_fox, w_f, cos, sin, bf_row, ones_row)


def _split3(x):
    x1 = x.astype(BF16).astype(F32)
    x2 = (x - x1).astype(BF16).astype(F32)
    x3 = (x - x1 - x2).astype(BF16).astype(F32)
    return x1, x2, x3


def _fox_bias_kernel(lf_ref, q_ref, k_ref, qo_ref, ko_ref, car_ref):
    @pl.when(pl.program_id(1) == 0)
    def _():
        car_ref[...] = jnp.zeros_like(car_ref)

    tc = lf_ref.shape[0]
    r = lax.broadcasted_iota(jnp.int32, (tc, tc), 0)
    c = lax.broadcasted_iota(jnp.int32, (tc, tc), 1)
    lower = (c <= r).astype(F32)
    cs = jnp.dot(lower, lf_ref[...], precision=HIGHEST, preferred_element_type=F32) + car_ref[...]
    car_ref[...] = cs[tc - 1:tc, :]
    cs = cs * LOG2E
    lane = lax.broadcasted_iota(jnp.int32, (tc, LANES), 1)
    ones_q = jnp.where((lane >= DH + 3) & (lane < DH + 6), 1.0, 0.0)
    ones_k = jnp.where((lane >= DH) & (lane < DH + 3), 1.0, 0.0)
    for h in range(H_B):
        lanes = slice(h * LANES, (h + 1) * LANES)
        parts = _split3(cs[:, h:h + 1])
        q_add = ones_q
        k_add = ones_k
        for t, part in enumerate(parts):
            q_add = q_add + jnp.where(lane == DH + t, part, 0.0)
            k_add = k_add - jnp.where(lane == DH + 3 + t, part, 0.0)
        qo_ref[:, lanes] = (q_ref[:, lanes].astype(F32) + q_add).astype(BF16)
        ko_ref[:, lanes] = (k_ref[:, lanes].astype(F32) + k_add).astype(BF16)


def _fox_bias(lf, q, k, batch, seq, tc):
    nc = seq // tc
    row = lambda b, j: (b * nc + j, 0)
    wide = pl.BlockSpec((tc, FOX_PAD), row)
    return pl.pallas_call(
        _fox_bias_kernel,
        grid=(batch, nc),
        in_specs=[pl.BlockSpec((tc, H_B), row), wide, wide],
        out_specs=[wide, wide],
        out_shape=(jax.ShapeDtypeStruct(q.shape, BF16), jax.ShapeDtypeStruct(k.shape, BF16)),
        scratch_shapes=[pltpu.VMEM((1, H_B), F32)],
        compiler_params=_cparams("arbitrary", "arbitrary"),
    )(lf, q, k)


def _gn_gate(o, gate, gn):
    mu = jnp.mean(o, -1, keepdims=True)
    var = jnp.mean(jnp.square(o - mu), -1, keepdims=True)
    y = (o - mu) * lax.rsqrt(var + GN_EPS) * gn
    return (y * (gate / (1.0 + jnp.exp(-gate)))).astype(BF16)


def _ret_prompt_kernel(q_ref, k_ref, v_ref, g_ref, dec_ref, qd_ref, kd_ref, dl_ref, gn_ref,
                       y_ref, sfin_ref, st_ref, *, nsub):
    @pl.when(pl.program_id(1) == 0)
    def _():
        st_ref[...] = jnp.zeros_like(st_ref)

    for s in range(nsub):
        rows = slice(s * RET_CHUNK, (s + 1) * RET_CHUNK)
        for h in range(H_A):
            cols = slice(h * DK, (h + 1) * DK)
            q16 = q_ref[rows, cols].astype(BF16)
            k = k_ref[rows, cols]
            v16 = v_ref[rows, cols].astype(BF16)
            st = st_ref[h]
            sc = _dot_nt(q16, k.astype(BF16)) * dec_ref[h]
            o = _dot(sc.astype(BF16), v16) + qd_ref[h] * _dot(q16, st.astype(BF16))
            kdt = (k * kd_ref[h]).T.astype(BF16)
            st_ref[h] = dl_ref[h] * st + _dot(kdt, v16)
            y_ref[rows, cols] = _gn_gate(o, g_ref[rows, cols], gn_ref[:, cols])
    sfin_ref[0] = st_ref[...]


def _ret_prompt(rq, rk, rv, rg, tabs, gn, batch, seq, nsub):
    tr = nsub * RET_CHUNK
    nc = seq // tr
    row = lambda b, c: (b * nc + c, 0)
    blk = pl.BlockSpec((tr, AB_RET), row)
    dec, qd, kd, dl = tabs
    return pl.pallas_call(
        functools.partial(_ret_prompt_kernel, nsub=nsub),
        grid=(batch, nc),
        in_specs=[blk, blk, blk, blk, _full(dec.shape), _full(qd.shape), _full(kd.shape),
                  _full(dl.shape), _full(gn.shape)],
        out_specs=[blk, pl.BlockSpec((1, H_A, DK, DV), lambda b, c: (b, 0, 0, 0))],
        out_shape=(jax.ShapeDtypeStruct(rq.shape, BF16),
                   jax.ShapeDtypeStruct((batch, H_A, DK, DV), F32)),
        scratch_shapes=[pltpu.VMEM((H_A, DK, DV), F32)],
        compiler_params=_cparams("arbitrary", "arbitrary"),
    )(rq, rk, rv, rg, dec, qd, kd, dl, gn)


def _ret_sample_kernel(q_ref, k_ref, v_ref, g_ref, st_ref, dec_ref, qd_ref, kd_ref, dl_ref, gn_ref,
                       y_ref, snew_ref, *, bb, ds):
    n = bb * ds
    rb = lax.broadcasted_iota(jnp.int32, (n, DV), 0) // ds
    for h in range(H_A):
        cols = slice(h * DK, (h + 1) * DK)
        q = q_ref[:, cols]
        k = k_ref[:, cols]
        v = v_ref[:, cols]
        v16 = v.astype(BF16)
        sc = _dot_nt(q.astype(BF16), k.astype(BF16)) * dec_ref[h]
        o_intra = _dot(sc.astype(BF16), v16)
        kdt = (k * kd_ref[h]).T.astype(BF16)
        dl = dl_ref[h]

        def per_seq(b, o_cross, h=h, q=q, v=v, kdt=kdt, dl=dl):
            st = st_ref[b, h]
            qb = jnp.where(rb == b, q, 0.0).astype(BF16)
            vb = jnp.where(rb == b, v, 0.0).astype(BF16)
            snew_ref[b, h] = dl * st + _dot(kdt, vb)
            return o_cross + _dot(qb, st.astype(BF16))

        o_cross = lax.fori_loop(0, bb, per_seq, jnp.zeros((n, DV), F32))
        o = o_intra + qd_ref[h] * o_cross
        y_ref[:, cols] = _gn_gate(o, g_ref[:, cols], gn_ref[:, cols])


def _ret_sample(rq, rk, rv, rg, state, tabs, gn, bb, ds):
    nb = state.shape[0]
    n = bb * ds
    row = lambda i: (i, 0)
    blk = pl.BlockSpec((n, AB_RET), row)
    sblk = pl.BlockSpec((bb, H_A, DK, DV), lambda i: (i, 0, 0, 0))
    dec, qd, kd, dl = tabs
    return pl.pallas_call(
        functools.partial(_ret_sample_kernel, bb=bb, ds=ds),
        grid=(nb // bb,),
        in_specs=[blk, blk, blk, blk, sblk, _full(dec.shape), _full(qd.shape), _full(kd.shape),
                  _full(dl.shape), _full(gn.shape)],
        out_specs=[blk, sblk],
        out_shape=(jax.ShapeDtypeStruct(rq.shape, BF16), jax.ShapeDtypeStruct(state.shape, F32)),
        compiler_params=_cparams("parallel"),
    )(rq, rk, rv, rg, state, dec, qd, kd, dl, gn)


def _flash_kernel(q_ref, k_ref, v_ref, o_ref, *, hg, dout, tq, tk):
    i = pl.program_id(2)
    ratio = tq // tk
    row = lax.broadcasted_iota(jnp.int32, (tq, tk), 0)
    col = lax.broadcasted_iota(jnp.int32, (tq, tk), 1)

    def step(j, carry, diag):
        ks = pl.ds(pl.multiple_of(j * tk, tk), tk)
        out = []
        for h in range(hg):
            lanes = slice(h * LANES, (h + 1) * LANES)
            m, acc = carry[h]
            s = _dot_nt(q_ref[:, lanes], k_ref[ks, lanes])
            if diag is not None:
                s = jnp.where(col + diag * tk <= row, s, NEG_INF)
            m_new = jnp.maximum(m, jnp.max(s, -1, keepdims=True))
            p = jnp.exp2((s - m_new).astype(BF16))
            acc = jnp.exp2(m - m_new) * acc + _dot(p, v_ref[ks, lanes])
            out.append((m_new, acc))
        return tuple(out)

    def two_steps(jj, carry):
        return step(2 * jj + 1, step(2 * jj, carry, None), None)

    init = (jnp.full((tq, 1), NEG_INF, F32), jnp.zeros((tq, LANES), F32))
    nfull = i * ratio
    carry = lax.fori_loop(0, nfull // 2, two_steps, (init,) * hg)
    carry = lax.fori_loop(2 * (nfull // 2), nfull, functools.partial(step, diag=None), carry)
    for d in range(ratio):
        carry = step(i * ratio + d, carry, d)
    for h in range(hg):
        acc = carry[h][1]
        o_ref[:, h * dout:(h + 1) * dout] = (acc[:, :dout] / acc[:, dout:dout + 1]).astype(BF16)


def _flash(q, k, v, batch, seq, heads, dout, hg, tq, tk):
    nq = seq // tq
    ngrp = heads // hg
    return pl.pallas_call(
        functools.partial(_flash_kernel, hg=hg, dout=dout, tq=tq, tk=tk),
        grid=(batch, ngrp, nq),
        in_specs=[pl.BlockSpec((tq, hg * LANES), lambda b, g, i: (b * nq + i, g)),
                  pl.BlockSpec((seq, hg * LANES), lambda b, g, i: (b, g)),
                  pl.BlockSpec((seq, hg * LANES), lambda b, g, i: (b, g))],
        out_specs=pl.BlockSpec((tq, hg * dout), lambda b, g, i: (b * nq + i, g)),
        out_shape=jax.ShapeDtypeStruct((batch * seq, heads * dout), BF16),
        compiler_params=_cparams("parallel", "parallel", "arbitrary"),
    )(q, k, v)


def _outproj_ln_kernel(*refs, n_in):
    a_refs = refs[:n_in]
    w_ref, x_ref, g_ref, b_ref, o_ref = refs[n_in:]
    acc = None
    off = 0
    for a_ref in a_refs:
        ka = a_ref.shape[1]
        d = _dot(a_ref[...], w_ref[off:off + ka, :])
        acc = d if acc is None else acc + d
        off += ka
    o_ref[...] = _layer_norm(ALPHA * x_ref[...] + acc, g_ref[...], b_ref[...])


def _outproj_ln(parts, w, x, g, b, tm):
    m = x.shape[0]
    row = lambda i: (i, 0)
    in_specs = [pl.BlockSpec((tm, p.shape[1]), row) for p in parts]
    in_specs += [_full(w.shape), pl.BlockSpec((tm, D_MODEL), row), _full(g.shape), _full(b.shape)]
    return pl.pallas_call(
        functools.partial(_outproj_ln_kernel, n_in=len(parts)),
        grid=(m // tm,),
        in_specs=in_specs,
        out_specs=pl.BlockSpec((tm, D_MODEL), row),
        out_shape=jax.ShapeDtypeStruct((m, D_MODEL), F32),
        compiler_params=_cparams("parallel"),
    )(*parts, w, x, g, b)


def _mlp_ln_kernel(x_ref, w1_ref, w2_ref, g_ref, b_ref, o_ref, xb_ref, acc_ref):
    f = pl.program_id(1)

    @pl.when(f == 0)
    def _():
        xb_ref[...] = x_ref[...].astype(BF16)
        acc_ref[...] = jnp.zeros_like(acc_ref)

    h = jnp.maximum(_dot(xb_ref[...], w1_ref[...]), 0.0)
    acc_ref[...] += _dot((h * h).astype(BF16), w2_ref[...])

    @pl.when(f == pl.num_programs(1) - 1)
    def _():
        o_ref[...] = _layer_norm(ALPHA * x_ref[...] + acc_ref[...], g_ref[...], b_ref[...])


def _mlp_ln(x, w1, w2, g, b, tm, tf):
    m = x.shape[0]
    return pl.pallas_call(
        _mlp_ln_kernel,
        grid=(m // tm, D_FF // tf),
        in_specs=[pl.BlockSpec((tm, D_MODEL), lambda i, f: (i, 0)),
                  pl.BlockSpec((D_MODEL, tf), lambda i, f: (0, f)),
                  pl.BlockSpec((tf, D_MODEL), lambda i, f: (f, 0)),
                  _full(g.shape), _full(b.shape)],
        out_specs=pl.BlockSpec((tm, D_MODEL), lambda i, f: (i, 0)),
        out_shape=jax.ShapeDtypeStruct((m, D_MODEL), F32),
        scratch_shapes=[pltpu.VMEM((tm, D_MODEL), BF16), pltpu.VMEM((tm, D_MODEL), F32)],
        compiler_params=_cparams("parallel", "arbitrary"),
    )(x, w1, w2, g, b)


def _rot_lanes(x, c, s1, s2):
    return x * c + pltpu.roll(x, LANES - ROPE_D // 2, 1) * s1 + pltpu.roll(x, ROPE_D // 2, 1) * s2


def _proj_c_kernel(x_ref, win_ref, wpe_ref, qg_ref, kvg_ref, wqb_ref, wk_ref, wv_ref, place_ref, ones_ref,
                   qc_ref, qs1_ref, qs2_ref, kc_ref, ks1_ref, ks2_ref,
                   q_ref, k_ref, v_ref, ckv_ref, kpe_ref):
    xb = x_ref[...].astype(BF16)
    qa = _dot(xb, win_ref[:, :Q_LORA])
    kva = _dot(xb, win_ref[:, Q_LORA:Q_LORA + KV_LORA])
    kpe = _dot(xb, wpe_ref[...])
    qn = _rms_norm(qa, qg_ref[...]).astype(BF16)
    qc, qs1, qs2 = qc_ref[...], qs1_ref[...], qs2_ref[...]
    for h in range(H_C):
        seg = _dot(qn, wqb_ref[:, h * LANES:(h + 1) * LANES])
        q_ref[:, h * LANES:(h + 1) * LANES] = _rot_lanes(seg, qc, qs1, qs2).astype(BF16)
    ckv = _rms_norm(kva, kvg_ref[...])
    ckv_ref[...] = ckv
    ckv16 = ckv.astype(BF16)
    kpe_rot = _rot_lanes(kpe, kc_ref[...], ks1_ref[...], ks2_ref[...])
    kpe_ref[...] = kpe_rot[:, :ROPE_D]
    kpe16 = kpe_rot.astype(BF16)
    k_ref[...] = (_dot(ckv16, wk_ref[...]) + _dot(kpe16, place_ref[...])).astype(BF16)
    v_ref[...] = (_dot(ckv16, wv_ref[...]) + ones_ref[...]).astype(BF16)


def _proj_c(x, wts, tabs, tm):
    m = x.shape[0]
    ntab = tabs[0].shape[0] // tm
    row = lambda i: (i, 0)
    tab = pl.BlockSpec((tm, LANES), lambda i: (i % ntab, 0))
    return pl.pallas_call(
        _proj_c_kernel,
        grid=(m // tm,),
        in_specs=[pl.BlockSpec((tm, D_MODEL), row)] + [_full(w.shape) for w in wts] + [tab] * 6,
        out_specs=[pl.BlockSpec((tm, H_C * LANES), row)] * 3
        + [pl.BlockSpec((tm, KV_LORA), row), pl.BlockSpec((tm, ROPE_D), row)],
        out_shape=(jax.ShapeDtypeStruct((m, H_C * LANES), BF16),) * 3
        + (jax.ShapeDtypeStruct((m, KV_LORA), F32), jax.ShapeDtypeStruct((m, ROPE_D), F32)),
        compiler_params=_cparams("parallel"),
    )(x, *wts, *tabs)


def _head_mm_kernel(x_ref, w_ref, o_ref, *, hp, din, dout):
    for j in range(hp):
        o_ref[:, j * dout:(j + 1) * dout] = _dot(x_ref[:, j * din:(j + 1) * din], w_ref[j]).astype(BF16)


def _head_mm(x, w, hp):
    m = x.shape[0]
    heads, din, dout = w.shape
    return pl.pallas_call(
        functools.partial(_head_mm_kernel, hp=hp, din=din, dout=dout),
        grid=(heads // hp,),
        in_specs=[pl.BlockSpec((m, hp * din), lambda h: (0, h)),
                  pl.BlockSpec((hp, din, dout), lambda h: (h, 0, 0))],
        out_specs=pl.BlockSpec((m, hp * dout), lambda h: (0, h)),
        out_shape=jax.ShapeDtypeStruct((m, heads * dout), BF16),
        compiler_params=_cparams("parallel"),
    )(x, w)


def _fox_decode_kernel(pt_ref, q_ref, kn_ref, vn_ref, lfr_ref, lft_ref, sel_ref, *rest, gp, ds):
    k_refs = rest[:gp]
    v_refs = rest[gp:2 * gp]
    lf_refs = rest[2 * gp:3 * gp]
    o_ref, qbd_ref, m_ref, l_ref, acc_ref, car_ref, cnq_ref = rest[3 * gp:]
    pg = pl.program_id(1)
    nrow = H_B * ds
    page = k_refs[0].shape[3]

    def expand(x):
        return jnp.concatenate([jnp.broadcast_to(x[h:h + 1, :], (ds, x.shape[1])) for h in range(H_B)], axis=0)

    @pl.when(pg == 0)
    def _():
        q = q_ref[0].astype(F32)
        rh = lax.broadcasted_iota(jnp.int32, (nrow, AB_FOX), 0) // ds
        ch = lax.broadcasted_iota(jnp.int32, (nrow, AB_FOX), 1) // DH
        qbd = jnp.where(rh == ch, jnp.concatenate([q] * H_B, axis=0), 0.0).astype(BF16)
        qbd_ref[...] = qbd
        r = lax.broadcasted_iota(jnp.int32, (nrow, nrow), 0)
        c = lax.broadcasted_iota(jnp.int32, (nrow, nrow), 1)
        m_row = jnp.where((r // ds == c // ds) & (c <= r), 1.0, 0.0)
        cn_col = jnp.sum(m_row * lfr_ref[0], axis=1, keepdims=True)
        cnq_ref[...] = cn_col * LOG2E
        lft = lft_ref[0]
        npad = lft.shape[1]
        kk = lax.broadcasted_iota(jnp.int32, (1, npad), 1)
        cnt = jnp.zeros((H_B, npad), F32)
        for k2 in range(ds):
            cnt = cnt + lft[:, k2:k2 + 1] * jnp.where(kk >= k2, 1.0, 0.0)
        s = _dot_nt(qbd, kn_ref[0]) + (cn_col - expand(cnt)) * LOG2E
        rq = lax.broadcasted_iota(jnp.int32, (nrow, npad), 0) % ds
        ck = lax.broadcasted_iota(jnp.int32, (nrow, npad), 1)
        s = jnp.where(ck <= rq, s, NEG_INF)
        m = jnp.max(s, -1, keepdims=True)
        p = jnp.exp2(s - m)
        m_ref[...] = m
        l_ref[...] = jnp.sum(p, -1, keepdims=True)
        acc_ref[...] = _dot(p.astype(BF16), vn_ref[0])
        car_ref[...] = jnp.zeros_like(car_ref)

    qbd = qbd_ref[...]
    xs = []
    for j in range(gp):
        x = lf_refs[j][0, 0]
        x1 = x.astype(BF16).astype(F32)
        x2 = (x - x1).astype(BF16).astype(F32)
        x3 = (x - x1 - x2).astype(BF16).astype(F32)
        xs += [x1, x2, x3]
    yz = _dot(jnp.concatenate(xs, axis=0).astype(BF16), sel_ref[...])
    car = car_ref[...]
    revs = []
    for j in range(gp):
        o = j * 3 * H_B
        y = yz[o:o + H_B] + yz[o + H_B:o + 2 * H_B] + yz[o + 2 * H_B:o + 3 * H_B]
        revs.append(car + y[:, :page])
        car = car + y[:, page:]
    car_ref[...] = car
    kt_all = jnp.concatenate([k_refs[j][0, 0].astype(BF16) for j in range(gp)], axis=1)
    vt_all = jnp.concatenate([v_refs[j][0, 0].astype(BF16) for j in range(gp)], axis=1)
    s = _dot(qbd, kt_all) + expand(jnp.concatenate(revs, axis=1) * LOG2E) + cnq_ref[...]
    m = m_ref[...]
    m_new = jnp.maximum(m, jnp.max(s, -1, keepdims=True))
    alpha = jnp.exp2(m - m_new)
    p = jnp.exp2(s - m_new)
    m_ref[...] = m_new
    l_ref[...] = alpha * l_ref[...] + jnp.sum(p, -1, keepdims=True)
    acc_ref[...] = alpha * acc_ref[...] + _dot_nt(p.astype(BF16), vt_all)

    @pl.when(pg == pl.num_programs(1) - 1)
    def _():
        acc = acc_ref[...] / l_ref[...]
        ch = lax.broadcasted_iota(jnp.int32, (ds, AB_FOX), 1) // DH
        out = jnp.zeros((ds, AB_FOX), F32)
        for h in range(H_B):
            out = out + jnp.where(ch == h, acc[h * ds:(h + 1) * ds, :], 0.0)
        o_ref[0] = out.astype(BF16)


def _fox_decode(page_table, layer, q, kn, vn, lf_row, lf_t, sel, cache_kt, cache_vt, cache_lft, gp):
    nb, ds, _ = q.shape
    npages = page_table.shape[1]
    page = cache_kt.shape[3]
    nrow = H_B * ds

    def pidx(j):
        return lambda b, pg, pt: (layer, pt[b, npages - 1 - (pg * gp + j)], 0, 0)

    per_b = lambda b, pg, pt: (b, 0, 0)
    in_specs = [pl.BlockSpec((1,) + a.shape[1:], per_b) for a in (q, kn, vn, lf_row, lf_t)]
    in_specs += [pl.BlockSpec(sel.shape, lambda b, pg, pt: (0, 0))]
    in_specs += [pl.BlockSpec((1, 1, AB_FOX, page), pidx(j)) for j in range(gp)]
    in_specs += [pl.BlockSpec((1, 1, AB_FOX, page), pidx(j)) for j in range(gp)]
    in_specs += [pl.BlockSpec((1, 1, H_B, page), pidx(j)) for j in range(gp)]
    grid_spec = pltpu.PrefetchScalarGridSpec(
        num_scalar_prefetch=1,
        grid=(nb, npages // gp),
        in_specs=in_specs,
        out_specs=pl.BlockSpec((1, ds, AB_FOX), per_b),
        scratch_shapes=[pltpu.VMEM((nrow, AB_FOX), BF16), pltpu.VMEM((nrow, 1), F32),
                        pltpu.VMEM((nrow, 1), F32), pltpu.VMEM((nrow, AB_FOX), F32),
                        pltpu.VMEM((H_B, page), F32), pltpu.VMEM((nrow, 1), F32)])
    return pl.pallas_call(
        functools.partial(_fox_decode_kernel, gp=gp, ds=ds),
        grid_spec=grid_spec,
        out_shape=jax.ShapeDtypeStruct((nb, ds, AB_FOX), BF16),
        compiler_params=_cparams("parallel", "arbitrary"),
    )(page_table, q, kn, vn, lf_row, lf_t, sel, *([cache_kt] * gp), *([cache_vt] * gp), *([cache_lft] * gp))


def _logf_selectors(page):
    a = jnp.arange(page)[:, None]
    b = jnp.arange(page)[None, :]
    return jnp.concatenate([a > b, jnp.ones((page, page), bool)], axis=1).astype(BF16)


def _mla_decode_kernel(pt_ref, ql_ref, qp_ref, cn_ref, pn_ref, *rest, gp, ds):
    c_refs = rest[:gp]
    p_refs = rest[gp:2 * gp]
    o_ref, m_ref, l_ref, acc_ref = rest[2 * gp:]
    pg = pl.program_id(1)
    ql = ql_ref[0]
    qp = qp_ref[0]
    nrow = ql.shape[0]

    @pl.when(pg == 0)
    def _():
        cn = cn_ref[0]
        npad = cn.shape[0]
        s = _dot_nt(ql, cn) + _dot_nt(qp, pn_ref[0])
        rq = lax.broadcasted_iota(jnp.int32, (nrow, npad), 0) // H_C
        ck = lax.broadcasted_iota(jnp.int32, (nrow, npad), 1)
        s = jnp.where(ck <= rq, s, NEG_INF)
        m = jnp.max(s, -1, keepdims=True)
        p = jnp.exp2(s - m)
        m_ref[...] = m
        l_ref[...] = jnp.sum(p, -1, keepdims=True)
        acc_ref[...] = _dot(p.astype(BF16), cn)

    c_all = jnp.concatenate([c_refs[j][0].astype(BF16) for j in range(gp)], axis=0)
    kp_all = jnp.concatenate([p_refs[j][0].astype(BF16) for j in range(gp)], axis=1)
    s = _dot_nt(ql, c_all) + _dot(qp, kp_all)
    m = m_ref[...]
    m_new = jnp.maximum(m, jnp.max(s, -1, keepdims=True))
    alpha = jnp.exp2(m - m_new)
    p = jnp.exp2(s - m_new)
    m_ref[...] = m_new
    l_ref[...] = alpha * l_ref[...] + jnp.sum(p, -1, keepdims=True)
    acc_ref[...] = alpha * acc_ref[...] + _dot(p.astype(BF16), c_all)

    @pl.when(pg == pl.num_programs(1) - 1)
    def _():
        o_ref[0] = (acc_ref[...] / l_ref[...]).astype(BF16)


def _mla_decode(page_table, base, ql, qp, cn, pn, cache_c, cache_pt, gp):
    nb, nrow, _ = ql.shape
    npages = page_table.shape[1]
    page = cache_c.shape[1]

    def pidx(j):
        return lambda b, pg, pt: (base + pt[b, pg * gp + j], 0, 0)

    per_b = lambda b, pg, pt: (b, 0, 0)
    in_specs = [pl.BlockSpec((1,) + a.shape[1:], per_b) for a in (ql, qp, cn, pn)]
    in_specs += [pl.BlockSpec((1, page, KV_LORA), pidx(j)) for j in range(gp)]
    in_specs += [pl.BlockSpec((1, ROPE_D, page), pidx(j)) for j in range(gp)]
    grid_spec = pltpu.PrefetchScalarGridSpec(
        num_scalar_prefetch=1,
        grid=(nb, npages // gp),
        in_specs=in_specs,
        out_specs=pl.BlockSpec((1, nrow, KV_LORA), per_b),
        scratch_shapes=[pltpu.VMEM((nrow, 1), F32), pltpu.VMEM((nrow, 1), F32),
                        pltpu.VMEM((nrow, KV_LORA), F32)])
    return pl.pallas_call(
        functools.partial(_mla_decode_kernel, gp=gp, ds=ds_of(nrow)),
        grid_spec=grid_spec,
        out_shape=jax.ShapeDtypeStruct((nb, nrow, KV_LORA), BF16),
        compiler_params=_cparams("parallel", "arbitrary"),
    )(page_table, ql, qp, cn, pn, *([cache_c] * gp), *([cache_pt] * gp))


def ds_of(nrow):
    return nrow // H_C


def _ret_rope_tables(pos):
    half = DK // 2
    inv = ROPE_BASE ** (-jnp.arange(half, dtype=F32) / half)
    ang = pos.astype(F32)[:, None] * inv[None, :]
    cos, sin = jnp.cos(ang), jnp.sin(ang)
    return jnp.concatenate([cos, cos], -1), jnp.concatenate([-sin, sin], -1)


def _mla_rope_tables(pos, lane0, passthrough, scale):
    half = ROPE_D // 2
    inv = ROPE_BASE ** (-jnp.arange(half, dtype=F32) / half)
    ang = pos.astype(F32)[:, None] * inv[None, :]
    cos, sin = jnp.cos(ang) * scale, jnp.sin(ang) * scale
    n = pos.shape[0]
    z = lambda w: jnp.zeros((n, w), F32)
    lead = jnp.full((n, lane0), scale if passthrough else 0.0, F32)
    tail = z(LANES - lane0 - ROPE_D)
    c = jnp.concatenate([lead, cos, cos, tail], -1)
    s1 = jnp.concatenate([z(lane0), -sin, z(half), tail], -1)
    s2 = jnp.concatenate([z(lane0), z(half), sin, tail], -1)
    return c, s1, s2


def _retention_tables(length, reps):
    lg = jnp.log(1.0 - 2.0 ** (-5.0 - jnp.arange(H_A, dtype=F32)))
    idx = jnp.arange(length, dtype=F32)
    diff = idx[:, None] - idx[None, :]
    decay = jnp.where(diff >= 0, jnp.exp(lg[:, None, None] * jnp.maximum(diff, 0.0)[None]), 0.0)
    qd = jnp.exp((idx[:, None] + 1.0) * lg[None, :]).T
    kd = jnp.exp((length - 1.0 - idx)[:, None] * lg[None, :]).T
    dl = jnp.exp(length * lg)
    if reps > 1:
        eye = jnp.eye(reps, dtype=F32)
        decay = jnp.einsum('ab,hij->haibj', eye, decay).reshape(H_A, reps * length, reps * length)
        qd = jnp.tile(qd, (1, reps))
        kd = jnp.tile(kd, (1, reps))
    n = reps * length
    qd = jnp.broadcast_to(qd[:, :, None], (H_A, n, DV))
    kd = jnp.broadcast_to(kd[:, :, None], (H_A, n, DK))
    dl = jnp.broadcast_to(dl[:, None, None], (H_A, DK, DV))
    return decay, qd, kd, dl


def _ones_row(heads, lane):
    return (jnp.arange(heads * LANES) % LANES == lane).astype(F32)[None, :]


def _prep_ab_weights(w_in, b_f):
    main = w_in[:, :4 * AB_RET].astype(BF16)
    fox = w_in[:, 4 * AB_RET:4 * AB_RET + 3 * AB_FOX].reshape(D_MODEL, 3 * H_B, DH)
    fox = jnp.pad(fox, ((0, 0), (0, 0), (0, LANES - DH))).reshape(D_MODEL, 3 * FOX_PAD).astype(BF16)
    w_f = jnp.pad(w_in[:, 4 * AB_RET + 3 * AB_FOX:], ((0, 0), (0, LANES - H_B))).astype(BF16)
    bf_row = jnp.pad(b_f, (0, LANES - H_B))[None, :]
    return main, fox, w_f, bf_row, _ones_row(H_B, DH)


def _prep_c_weights(w_in, q_g, w_qb, kv_g, w_kvb):
    win = w_in[:, :Q_LORA + KV_LORA].astype(BF16)
    wpe = jnp.pad(w_in[:, Q_LORA + KV_LORA:], ((0, 0), (0, LANES - ROPE_D))).astype(BF16)
    wqb = w_qb.reshape(Q_LORA, H_C, NOPE + ROPE_D)
    wqb = jnp.pad(wqb, ((0, 0), (0, 0), (0, LANES - NOPE - ROPE_D))).reshape(Q_LORA, H_C * LANES).astype(BF16)
    wkv = w_kvb.reshape(KV_LORA, H_C, NOPE + V_DIM)
    w_uk, w_uv = wkv[..., :NOPE], wkv[..., NOPE:]
    wk = jnp.pad(w_uk, ((0, 0), (0, 0), (0, LANES - NOPE))).reshape(KV_LORA, H_C * LANES).astype(BF16)
    wv = jnp.pad(w_uv, ((0, 0), (0, 0), (0, LANES - V_DIM))).reshape(KV_LORA, H_C * LANES).astype(BF16)
    r = jnp.arange(LANES)[:, None]
    c = jnp.arange(H_C * LANES)[None, :]
    place = ((r < ROPE_D) & (c % LANES == NOPE + r)).astype(BF16)
    w_abs = jnp.pad(w_uk.transpose(1, 2, 0), ((0, 0), (0, LANES - NOPE), (0, 0))).astype(BF16)
    w_val = w_uv.transpose(1, 0, 2).astype(BF16)
    return (win, wpe, q_g[None, :], kv_g[None, :], wqb, wk, wv, place, _ones_row(H_C, V_DIM)), w_abs, w_val


TM = 512
TM_MLP = 1024
TF_MLP = 1024
TQ, TK = 512, 512
TC = 512
RET_SUB = 4
RET_BB = 16
FOX_HG = 4
MLA_HG = 4
PAGES_PER_STEP = 16
NEW_PAD = 16


def kernel(x_prompt, x_sample, state_ret, cache_fox_k, cache_fox_v, cache_fox_logf, cache_mla_ckv,
           cache_mla_kpe, page_table, w_in_ab, ret_gn_g, fox_b_f, w_out_ab, w_in_c, mla_q_norm_g,
           mla_w_qb, mla_kv_norm_g, mla_w_kvb, w_out_c, ln_mix_g, ln_mix_b, ln_mlp_g, ln_mlp_b,
           mlp_w1, mlp_w2):
    batch, seq, _ = x_prompt.shape
    nb, ds, _ = x_sample.shape
    n_pool, page = cache_fox_k.shape[1], cache_fox_k.shape[2]
    past_len = page_table.shape[1] * page
    mp, ms = batch * seq, nb * ds
    pos_p = jnp.arange(seq)
    pos_s = past_len + jnp.arange(ds)
    pos_s_rows = jnp.tile(pos_s, nb)

    hp = x_prompt.reshape(mp, D_MODEL)
    hs = x_sample.reshape(ms, D_MODEL)
    tms = min(TM, ms)
    outs = {k: [] for k in ("ret_p", "ret_s", "fk_p", "fv_p", "fl_p", "fk_s", "fv_s", "fl_s",
                            "ck_p", "kp_p", "ck_s", "kp_s")}

    for l in range(DEPTH):
        i = l // 2
        row = lambda a: a[None, :]
        if l % 2 == 0:
            wts = _prep_ab_weights(w_in_ab[i], fox_b_f[i])
            w_out = w_out_ab[i].astype(BF16)
            gn = ret_gn_g[i][None, :]
            cos, sin = _ret_rope_tables(pos_p)
            rq, rk, rv, rg, fq, fk, fv, fk16, fv16, lf = _proj_ab(hp, *wts[:3], cos, sin, *wts[3:], TM)
            y, s_fin = _ret_prompt(rq, rk, rv, rg, _retention_tables(RET_CHUNK, 1), gn, batch, seq, RET_SUB)
            fqb, fkb = _fox_bias(lf, fq, fk16, batch, seq, TC)
            fo = _flash(fqb, fkb, fv16, batch, seq, H_B, DH, FOX_HG, TQ, TK)
            mix_parts_p = [y, fo]
            unpad = lambda a, n: a.reshape(n, H_B, LANES)[:, :, :DH]
            outs["ret_p"].append(s_fin)
            outs["fk_p"].append(unpad(fk, mp).reshape(batch, seq, H_B, DH))
            outs["fv_p"].append(unpad(fv, mp).reshape(batch, seq, H_B, DH))
            outs["fl_p"].append(lf.reshape(batch, seq, H_B))
            cos, sin = _ret_rope_tables(pos_s_rows)
            rq, rk, rv, rg, fq, fk, fv, fk16, fv16, lf = _proj_ab(hs, *wts[:3], cos, sin, *wts[3:], tms)
            y, s_new = _ret_sample(rq, rk, rv, rg, state_ret[i], _retention_tables(ds, RET_BB), gn, RET_BB, ds)
            padn = lambda a: jnp.pad(unpad(a, ms).reshape(nb, ds, AB_FOX), ((0, 0), (0, NEW_PAD - ds), (0, 0)))
            lf_hq = lf.reshape(nb, ds, H_B).transpose(0, 2, 1)
            fo = _fox_decode(page_table, i, unpad(fq, ms).reshape(nb, ds, AB_FOX), padn(fk16), padn(fv16),
                             lf_hq.reshape(nb, 1, H_B * ds),
                             jnp.pad(lf_hq, ((0, 0), (0, 0), (0, NEW_PAD - ds))),
                             _logf_selectors(page),
                             cache_fox_k.transpose(0, 1, 3, 4, 2).reshape(-1, n_pool, AB_FOX, page),
                             cache_fox_v.transpose(0, 1, 3, 4, 2).reshape(-1, n_pool, AB_FOX, page),
                             cache_fox_logf.transpose(0, 1, 3, 2), PAGES_PER_STEP)
            mix_parts_s = [y, fo.reshape(ms, AB_FOX)]
            outs["ret_s"].append(s_new)
            outs["fk_s"].append(unpad(fk, ms).reshape(nb, ds, H_B, DH))
            outs["fv_s"].append(unpad(fv, ms).reshape(nb, ds, H_B, DH))
            outs["fl_s"].append(lf.reshape(nb, ds, H_B))
        else:
            wts, w_abs, w_val = _prep_c_weights(w_in_c[i], mla_q_norm_g[i], mla_w_qb[i],
                                                mla_kv_norm_g[i], mla_w_kvb[i])
            w_out = w_out_c[i].astype(BF16)
            qscale = (NOPE + ROPE_D) ** -0.5 * LOG2E
            tabs = _mla_rope_tables(pos_p, NOPE, True, qscale) + _mla_rope_tables(pos_p, 0, False, 1.0)
            q, k, v, ckv, kpe = _proj_c(hp, wts, tabs, TM)
            o = _flash(q, k, v, batch, seq, H_C, V_DIM, MLA_HG, TQ, TK)
            mix_parts_p = [o]
            outs["ck_p"].append(ckv.reshape(batch, seq, KV_LORA))
            outs["kp_p"].append(kpe.reshape(batch, seq, ROPE_D))
            tabs = _mla_rope_tables(pos_s_rows, NOPE, True, qscale) + _mla_rope_tables(pos_s_rows, 0, False, 1.0)
            q, _, _, ckv, kpe = _proj_c(hs, wts, tabs, tms)
            q_lat = _head_mm(q, w_abs, 1).reshape(nb, ds * H_C, KV_LORA)
            q_pe = q.reshape(nb, ds * H_C, LANES)[:, :, NOPE:NOPE + ROPE_D]
            padn = lambda a: jnp.pad(a.reshape(nb, ds, -1), ((0, 0), (0, NEW_PAD - ds), (0, 0))).astype(BF16)
            o_lat = _mla_decode(page_table, i * n_pool, q_lat, q_pe, padn(ckv), padn(kpe),
                                cache_mla_ckv.reshape(-1, page, KV_LORA),
                                cache_mla_kpe.transpose(0, 1, 3, 2).reshape(-1, ROPE_D, page), PAGES_PER_STEP)
            o = _head_mm(o_lat.reshape(ms, H_C * KV_LORA), w_val, 2)
            mix_parts_s = [o]
            outs["ck_s"].append(ckv.reshape(nb, ds, KV_LORA))
            outs["kp_s"].append(kpe.reshape(nb, ds, ROPE_D))

        w1 = mlp_w1[l].astype(BF16)
        w2 = mlp_w2[l].astype(BF16)
        hp = _outproj_ln(mix_parts_p, w_out, hp, row(ln_mix_g[l]), row(ln_mix_b[l]), TM)
        hs = _outproj_ln(mix_parts_s, w_out, hs, row(ln_mix_g[l]), row(ln_mix_b[l]), tms)
        hp = _mlp_ln(hp, w1, w2, row(ln_mlp_g[l]), row(ln_mlp_b[l]), TM_MLP, TF_MLP)
        hs = _mlp_ln(hs, w1, w2, row(ln_mlp_g[l]), row(ln_mlp_b[l]), min(TM_MLP, ms), TF_MLP)

    st = lambda k: jnp.stack(outs[k])
    return (hp.reshape(batch, seq, D_MODEL), hs.reshape(nb, ds, D_MODEL),
            st("ret_p"), st("ret_s"), st("fk_p"), st("fv_p"), st("fl_p"),
            st("fk_s"), st("fv_s"), st("fl_s"), st("ck_p"), st("kp_p"), st("ck_s"), st("kp_s"))
```

```python
import functools
import math

import jax
import jax.numpy as jnp
from jax import lax
from jax.experimental import pallas as pl
from jax.experimental.pallas import tpu as pltpu

F32 = jnp.float32
BF16 = jnp.bfloat16

D_MODEL = 1024
DEPTH = 2
H_A, DK, DV = 4, 128, 128
RET_CHUNK = 128
H_B, DH = 8, 64
H_C, NOPE, ROPE_D, V_DIM = 16, 64, 32, 64
Q_LORA, KV_LORA = 768, 256
D_FF = 4 * D_MODEL
ROPE_BASE = 10000.0
LN_EPS = 1e-5
RMS_EPS = 1e-6
GN_EPS = 1e-6
ALPHA = (2 * DEPTH) ** 0.25
AB_RET = H_A * DK
AB_FOX = H_B * DH
FOX_PAD = H_B * 128
LANES = 128
LOG2E = math.log2(math.e)
VMEM_LIMIT = 56 * 1024 * 1024
NEG_INF = float("-inf")
HIGHEST = lax.Precision.HIGHEST


def _cparams(*sem):
    return pltpu.CompilerParams(dimension_semantics=sem, vmem_limit_bytes=VMEM_LIMIT)


def _dot(a, b):
    return jnp.dot(a, b, preferred_element_type=F32)


def _dot_nt(a, b):
    return lax.dot_general(a, b, (((1,), (1,)), ((), ())), preferred_element_type=F32)


def _layer_norm(z, g, b):
    mu = jnp.mean(z, -1, keepdims=True)
    var = jnp.mean(jnp.square(z - mu), -1, keepdims=True)
    return (z - mu) * lax.rsqrt(var + LN_EPS) * g + b


def _rms_norm(z, g):
    return z * lax.rsqrt(jnp.mean(z * z, -1, keepdims=True) + RMS_EPS) * g


def _log_sigmoid(x):
    return jnp.minimum(x, 0.0) - jnp.log1p(jnp.exp(-jnp.abs(x)))


def _full(shape):
    nd = len(shape)
    return pl.BlockSpec(shape, lambda *_: (0,) * nd)


def _proj_ab_kernel(x_ref, w_ref, wfox_ref, wf_ref, cos_ref, sin_ref, bf_ref, ones_ref,
                    rq_ref, rk_ref, rv_ref, rg_ref, fq_ref, fk_ref, fv_ref, fk16_ref, fv16_ref, lf_ref):
    xb = x_ref[...].astype(BF16)
    cos = cos_ref[...]
    sin = sin_ref[...]

    def slab(j):
        return _dot(xb, w_ref[:, j * AB_RET:(j + 1) * AB_RET])

    def fox_slab(j):
        return _dot(xb, wfox_ref[:, j * FOX_PAD:(j + 1) * FOX_PAD])

    def rope(h):
        parts = []
        for g in range(H_A):
            seg = h[:, g * DK:(g + 1) * DK]
            parts.append(seg * cos + pltpu.roll(seg, DK // 2, 1) * sin)
        return jnp.concatenate(parts, axis=1)

    rq_ref[...] = rope(slab(0))
    rk_ref[...] = rope(slab(1)) * (DK ** -0.5)
    rv_ref[...] = slab(2)
    rg_ref[...] = slab(3)
    fq_ref[...] = (fox_slab(0) * (DH ** -0.5 * LOG2E)).astype(BF16)
    fk = fox_slab(1)
    fv = fox_slab(2)
    fk_ref[...] = fk
    fv_ref[...] = fv
    fk16_ref[...] = fk.astype(BF16)
    fv16_ref[...] = (fv + ones_ref[...]).astype(BF16)
    ff = _dot(xb, wf_ref[...]) + bf_ref[...]
    lf_ref[...] = _log_sigmoid(ff)[:, :H_B]


def _proj_ab(x, w_main, w_fox, w_f, cos, sin, bf_row, ones_row, tm):
    m = x.shape[0]
    ntab = cos.shape[0] // tm
    row = lambda i: (i, 0)
    tab = lambda i: (i % ntab, 0)
    ret = lambda dt: jax.ShapeDtypeStruct((m, AB_RET), dt)
    fox = lambda dt: jax.ShapeDtypeStruct((m, FOX_PAD), dt)
    outs = (ret(F32), ret(F32), ret(F32), ret(F32), fox(BF16), fox(F32), fox(F32), fox(BF16), fox(BF16),
            jax.ShapeDtypeStruct((m, H_B), F32))
    rspec = pl.BlockSpec((tm, AB_RET), row)
    fspec = pl.BlockSpec((tm, FOX_PAD), row)
    return pl.pallas_call(
        _proj_ab_kernel,
        grid=(m // tm,),
        in_specs=[pl.BlockSpec((tm, D_MODEL), row), _full(w_main.shape), _full(w_fox.shape), _full(w_f.shape),
                  pl.BlockSpec((tm, LANES), tab), pl.BlockSpec((tm, LANES), tab),
                  _full(bf_row.shape), _full(ones_row.shape)],
        out_specs=[rspec] * 4 + [fspec] * 5 + [pl.BlockSpec((tm, H_B), row)],
        out_shape=outs,
        compiler_params=_cparams("parallel"),
    )(x, w_main, w_fox, w_f, cos, sin, bf_row, ones_row)


def _split3(x):
    x1 = x.astype(BF16).astype(F32)
    x2 = (x - x1).astype(BF16).astype(F32)
    x3 = (x - x1 - x2).astype(BF16).astype(F32)
    return x1, x2, x3


def _fox_bias_kernel(lf_ref, q_ref, k_ref, qo_ref, ko_ref, car_ref):
    @pl.when(pl.program_id(1) == 0)
    def _():
        car_ref[...] = jnp.zeros_like(car_ref)

    tc = lf_ref.shape[0]
    r = lax.broadcasted_iota(jnp.int32, (tc, tc), 0)
    c = lax.broadcasted_iota(jnp.int32, (tc, tc), 1)
    lower = (c <= r).astype(F32)
    cs = jnp.dot(lower, lf_ref[...], precision=HIGHEST, preferred_element_type=F32) + car_ref[...]
    car_ref[...] = cs[tc - 1:tc, :]
    cs = cs * LOG2E
    lane = lax.broadcasted_iota(jnp.int32, (tc, LANES), 1)
    ones_q = jnp.where((lane >= DH + 3) & (lane < DH + 6), 1.0, 0.0)
    ones_k = jnp.where((lane >= DH) & (lane < DH + 3), 1.0, 0.0)
    for h in range(H_B):
        lanes = slice(h * LANES, (h + 1) * LANES)
        parts = _split3(cs[:, h:h + 1])
        q_add = ones_q
        k_add = ones_k
        for t, part in enumerate(parts):
            q_add = q_add + jnp.where(lane == DH + t, part, 0.0)
            k_add = k_add - jnp.where(lane == DH + 3 + t, part, 0.0)
        qo_ref[:, lanes] = (q_ref[:, lanes].astype(F32) + q_add).astype(BF16)
        ko_ref[:, lanes] = (k_ref[:, lanes].astype(F32) + k_add).astype(BF16)


def _fox_bias(lf, q, k, batch, seq, tc):
    nc = seq // tc
    row = lambda b, j: (b * nc + j, 0)
    wide = pl.BlockSpec((tc, FOX_PAD), row)
    return pl.pallas_call(
        _fox_bias_kernel,
        grid=(batch, nc),
        in_specs=[pl.BlockSpec((tc, H_B), row), wide, wide],
        out_specs=[wide, wide],
        out_shape=(jax.ShapeDtypeStruct(q.shape, BF16), jax.ShapeDtypeStruct(k.shape, BF16)),
        scratch_shapes=[pltpu.VMEM((1, H_B), F32)],
        compiler_params=_cparams("arbitrary", "arbitrary"),
    )(lf, q, k)


def _gn_gate(o, gate, gn):
    mu = jnp.mean(o, -1, keepdims=True)
    var = jnp.mean(jnp.square(o - mu), -1, keepdims=True)
    y = (o - mu) * lax.rsqrt(var + GN_EPS) * gn
    return (y * (gate / (1.0 + jnp.exp(-gate)))).astype(BF16)


def _ret_prompt_kernel(q_ref, k_ref, v_ref, g_ref, dec_ref, qd_ref, kd_ref, dl_ref, gn_ref,
                       y_ref, sfin_ref, st_ref, *, nsub):
    @pl.when(pl.program_id(1) == 0)
    def _():
        st_ref[...] = jnp.zeros_like(st_ref)

    for s in range(nsub):
        rows = slice(s * RET_CHUNK, (s + 1) * RET_CHUNK)
        for h in range(H_A):
            cols = slice(h * DK, (h + 1) * DK)
            q16 = q_ref[rows, cols].astype(BF16)
            k = k_ref[rows, cols]
            v16 = v_ref[rows, cols].astype(BF16)
            st = st_ref[h]
            sc = _dot_nt(q16, k.astype(BF16)) * dec_ref[h]
            o = _dot(sc.astype(BF16), v16) + qd_ref[h] * _dot(q16, st.astype(BF16))
            kdt = (k * kd_ref[h]).T.astype(BF16)
            st_ref[h] = dl_ref[h] * st + _dot(kdt, v16)
            y_ref[rows, cols] = _gn_gate(o, g_ref[rows, cols], gn_ref[:, cols])
    sfin_ref[0] = st_ref[...]


def _ret_prompt(rq, rk, rv, rg, tabs, gn, batch, seq, nsub):
    tr = nsub * RET_CHUNK
    nc = seq // tr
    row = lambda b, c: (b * nc + c, 0)
    blk = pl.BlockSpec((tr, AB_RET), row)
    dec, qd, kd, dl = tabs
    return pl.pallas_call(
        functools.partial(_ret_prompt_kernel, nsub=nsub),
        grid=(batch, nc),
        in_specs=[blk, blk, blk, blk, _full(dec.shape), _full(qd.shape), _full(kd.shape),
                  _full(dl.shape), _full(gn.shape)],
        out_specs=[blk, pl.BlockSpec((1, H_A, DK, DV), lambda b, c: (b, 0, 0, 0))],
        out_shape=(jax.ShapeDtypeStruct(rq.shape, BF16),
                   jax.ShapeDtypeStruct((batch, H_A, DK, DV), F32)),
        scratch_shapes=[pltpu.VMEM((H_A, DK, DV), F32)],
        compiler_params=_cparams("arbitrary", "arbitrary"),
    )(rq, rk, rv, rg, dec, qd, kd, dl, gn)


def _ret_sample_kernel(q_ref, k_ref, v_ref, g_ref, st_ref, dec_ref, qd_ref, kd_ref, dl_ref, gn_ref,
                       y_ref, snew_ref, *, bb, ds):
    n = bb * ds
    rb = lax.broadcasted_iota(jnp.int32, (n, DV), 0) // ds
    for h in range(H_A):
        cols = slice(h * DK, (h + 1) * DK)
        q = q_ref[:, cols]
        k = k_ref[:, cols]
        v = v_ref[:, cols]
        v16 = v.astype(BF16)
        sc = _dot_nt(q.astype(BF16), k.astype(BF16)) * dec_ref[h]
        o_intra = _dot(sc.astype(BF16), v16)
        kdt = (k * kd_ref[h]).T.astype(BF16)
        dl = dl_ref[h]

        def per_seq(b, o_cross, h=h, q=q, v=v, kdt=kdt, dl=dl):
            st = st_ref[b, h]
            qb = jnp.where(rb == b, q, 0.0).astype(BF16)
            vb = jnp.where(rb == b, v, 0.0).astype(BF16)
            snew_ref[b, h] = dl * st + _dot(kdt, vb)
            return o_cross + _dot(qb, st.astype(BF16))

        o_cross = lax.fori_loop(0, bb, per_seq, jnp.zeros((n, DV), F32))
        o = o_intra + qd_ref[h] * o_cross
        y_ref[:, cols] = _gn_gate(o, g_ref[:, cols], gn_ref[:, cols])


def _ret_sample(rq, rk, rv, rg, state, tabs, gn, bb, ds):
    nb = state.shape[0]
    n = bb * ds
    row = lambda i: (i, 0)
    blk = pl.BlockSpec((n, AB_RET), row)
    sblk = pl.BlockSpec((bb, H_A, DK, DV), lambda i: (i, 0, 0, 0))
    dec, qd, kd, dl = tabs
    return pl.pallas_call(
        functools.partial(_ret_sample_kernel, bb=bb, ds=ds),
        grid=(nb // bb,),
        in_specs=[blk, blk, blk, blk, sblk, _full(dec.shape), _full(qd.shape), _full(kd.shape),
                  _full(dl.shape), _full(gn.shape)],
        out_specs=[blk, sblk],
        out_shape=(jax.ShapeDtypeStruct(rq.shape, BF16), jax.ShapeDtypeStruct(state.shape, F32)),
        compiler_params=_cparams("parallel"),
    )(rq, rk, rv, rg, state, dec, qd, kd, dl, gn)


def _flash_kernel(q_ref, k_ref, v_ref, o_ref, *, hg, dout, tq, tk):
    i = pl.program_id(2)
    ratio = tq // tk
    row = lax.broadcasted_iota(jnp.int32, (tq, tk), 0)
    col = lax.broadcasted_iota(jnp.int32, (tq, tk), 1)

    def step(j, carry, diag):
        ks = pl.ds(pl.multiple_of(j * tk, tk), tk)
        out = []
        for h in range(hg):
            lanes = slice(h * LANES, (h + 1) * LANES)
            m, acc = carry[h]
            s = _dot_nt(q_ref[:, lanes], k_ref[ks, lanes])
            if diag is not None:
                s = jnp.where(col + diag * tk <= row, s, NEG_INF)
            m_new = jnp.maximum(m, jnp.max(s, -1, keepdims=True))
            p = jnp.exp2((s - m_new).astype(BF16))
            acc = jnp.exp2(m - m_new) * acc + _dot(p, v_ref[ks, lanes])
            out.append((m_new, acc))
        return tuple(out)

    def two_steps(jj, carry):
        return step(2 * jj + 1, step(2 * jj, carry, None), None)

    init = (jnp.full((tq, 1), NEG_INF, F32), jnp.zeros((tq, LANES), F32))
    nfull = i * ratio
    carry = lax.fori_loop(0, nfull // 2, two_steps, (init,) * hg)
    carry = lax.fori_loop(2 * (nfull // 2), nfull, functools.partial(step, diag=None), carry)
    for d in range(ratio):
        carry = step(i * ratio + d, carry, d)
    for h in range(hg):
        acc = carry[h][1]
        o_ref[:, h * dout:(h + 1) * dout] = (acc[:, :dout] / acc[:, dout:dout + 1]).astype(BF16)


def _flash(q, k, v, batch, seq, heads, dout, hg, tq, tk):
    nq = seq // tq
    ngrp = heads // hg
    return pl.pallas_call(
        functools.partial(_flash_kernel, hg=hg, dout=dout, tq=tq, tk=tk),
        grid=(batch, ngrp, nq),
        in_specs=[pl.BlockSpec((tq, hg * LANES), lambda b, g, i: (b * nq + i, g)),
                  pl.BlockSpec((seq, hg * LANES), lambda b, g, i: (b, g)),
                  pl.BlockSpec((seq, hg * LANES), lambda b, g, i: (b, g))],
        out_specs=pl.BlockSpec((tq, hg * dout), lambda b, g, i: (b * nq + i, g)),
        out_shape=jax.ShapeDtypeStruct((batch * seq, heads * dout), BF16),
        compiler_params=_cparams("parallel", "parallel", "arbitrary"),
    )(q, k, v)


def _outproj_ln_kernel(*refs, n_in):
    a_refs = refs[:n_in]
    w_ref, x_ref, g_ref, b_ref, o_ref = refs[n_in:]
    acc = None
    off = 0
    for a_ref in a_refs:
        ka = a_ref.shape[1]
        d = _dot(a_ref[...], w_ref[off:off + ka, :])
        acc = d if acc is None else acc + d
        off += ka
    o_ref[...] = _layer_norm(ALPHA * x_ref[...] + acc, g_ref[...], b_ref[...])


def _outproj_ln(parts, w, x, g, b, tm):
    m = x.shape[0]
    row = lambda i: (i, 0)
    in_specs = [pl.BlockSpec((tm, p.shape[1]), row) for p in parts]
    in_specs += [_full(w.shape), pl.BlockSpec((tm, D_MODEL), row), _full(g.shape), _full(b.shape)]
    return pl.pallas_call(
        functools.partial(_outproj_ln_kernel, n_in=len(parts)),
        grid=(m // tm,),
        in_specs=in_specs,
        out_specs=pl.BlockSpec((tm, D_MODEL), row),
        out_shape=jax.ShapeDtypeStruct((m, D_MODEL), F32),
        compiler_params=_cparams("parallel"),
    )(*parts, w, x, g, b)


def _mlp_ln_kernel(x_ref, w1_ref, w2_ref, g_ref, b_ref, o_ref, xb_ref, acc_ref):
    f = pl.program_id(1)

    @pl.when(f == 0)
    def _():
        xb_ref[...] = x_ref[...].astype(BF16)
        acc_ref[...] = jnp.zeros_like(acc_ref)

    h = jnp.maximum(_dot(xb_ref[...], w1_ref[...]), 0.0)
    acc_ref[...] += _dot((h * h).astype(BF16), w2_ref[...])

    @pl.when(f == pl.num_programs(1) - 1)
    def _():
        o_ref[...] = _layer_norm(ALPHA * x_ref[...] + acc_ref[...], g_ref[...], b_ref[...])


def _mlp_ln(x, w1, w2, g, b, tm, tf):
    m = x.shape[0]
    return pl.pallas_call(
        _mlp_ln_kernel,
        grid=(m // tm, D_FF // tf),
        in_specs=[pl.BlockSpec((tm, D_MODEL), lambda i, f: (i, 0)),
                  pl.BlockSpec((D_MODEL, tf), lambda i, f: (0, f)),
                  pl.BlockSpec((tf, D_MODEL), lambda i, f: (f, 0)),
                  _full(g.shape), _full(b.shape)],
        out_specs=pl.BlockSpec((tm, D_MODEL), lambda i, f: (i, 0)),
        out_shape=jax.ShapeDtypeStruct((m, D_MODEL), F32),
        scratch_shapes=[pltpu.VMEM((tm, D_MODEL), BF16), pltpu.VMEM((tm, D_MODEL), F32)],
        compiler_params=_cparams("parallel", "arbitrary"),
    )(x, w1, w2, g, b)


def _rot_lanes(x, c, s1, s2):
    return x * c + pltpu.roll(x, LANES - ROPE_D // 2, 1) * s1 + pltpu.roll(x, ROPE_D // 2, 1) * s2


def _proj_c_kernel(x_ref, win_ref, wpe_ref, qg_ref, kvg_ref, wqb_ref, wk_ref, wv_ref, place_ref, ones_ref,
                   qc_ref, qs1_ref, qs2_ref, kc_ref, ks1_ref, ks2_ref,
                   q_ref, k_ref, v_ref, ckv_ref, kpe_ref):
    xb = x_ref[...].astype(BF16)
    qa = _dot(xb, win_ref[:, :Q_LORA])
    kva = _dot(xb, win_ref[:, Q_LORA:Q_LORA + KV_LORA])
    kpe = _dot(xb, wpe_ref[...])
    qn = _rms_norm(qa, qg_ref[...]).astype(BF16)
    qc, qs1, qs2 = qc_ref[...], qs1_ref[...], qs2_ref[...]
    for h in range(H_C):
        seg = _dot(qn, wqb_ref[:, h * LANES:(h + 1) * LANES])
        q_ref[:, h * LANES:(h + 1) * LANES] = _rot_lanes(seg, qc, qs1, qs2).astype(BF16)
    ckv = _rms_norm(kva, kvg_ref[...])
    ckv_ref[...] = ckv
    ckv16 = ckv.astype(BF16)
    kpe_rot = _rot_lanes(kpe, kc_ref[...], ks1_ref[...], ks2_ref[...])
    kpe_ref[...] = kpe_rot[:, :ROPE_D]
    kpe16 = kpe_rot.astype(BF16)
    k_ref[...] = (_dot(ckv16, wk_ref[...]) + _dot(kpe16, place_ref[...])).astype(BF16)
    v_ref[...] = (_dot(ckv16, wv_ref[...]) + ones_ref[...]).astype(BF16)


def _proj_c(x, wts, tabs, tm):
    m = x.shape[0]
    ntab = tabs[0].shape[0] // tm
    row = lambda i: (i, 0)
    tab = pl.BlockSpec((tm, LANES), lambda i: (i % ntab, 0))
    return pl.pallas_call(
        _proj_c_kernel,
        grid=(m // tm,),
        in_specs=[pl.BlockSpec((tm, D_MODEL), row)] + [_full(w.shape) for w in wts] + [tab] * 6,
        out_specs=[pl.BlockSpec((tm, H_C * LANES), row)] * 3
        + [pl.BlockSpec((tm, KV_LORA), row), pl.BlockSpec((tm, ROPE_D), row)],
        out_shape=(jax.ShapeDtypeStruct((m, H_C * LANES), BF16),) * 3
        + (jax.ShapeDtypeStruct((m, KV_LORA), F32), jax.ShapeDtypeStruct((m, ROPE_D), F32)),
        compiler_params=_cparams("parallel"),
    )(x, *wts, *tabs)


def _head_mm_kernel(x_ref, w_ref, o_ref, *, hp, din, dout):
    for j in range(hp):
        o_ref[:, j * dout:(j + 1) * dout] = _dot(x_ref[:, j * din:(j + 1) * din], w_ref[j]).astype(BF16)


def _head_mm(x, w, hp):
    m = x.shape[0]
    heads, din, dout = w.shape
    return pl.pallas_call(
        functools.partial(_head_mm_kernel, hp=hp, din=din, dout=dout),
        grid=(heads // hp,),
        in_specs=[pl.BlockSpec((m, hp * din), lambda h: (0, h)),
                  pl.BlockSpec((hp, din, dout), lambda h: (h, 0, 0))],
        out_specs=pl.BlockSpec((m, hp * dout), lambda h: (0, h)),
        out_shape=jax.ShapeDtypeStruct((m, heads * dout), BF16),
        compiler_params=_cparams("parallel"),
    )(x, w)


def _fox_decode_kernel(pt_ref, q_ref, kn_ref, vn_ref, lfr_ref, lft_ref, sel_ref, cache_k, cache_v, cache_lf,
                       o_ref, kbuf, vbuf, lbuf, ksem, vsem, lsem,
                       qbd_ref, m_ref, l_ref, acc_ref, car_ref, cnq_ref, *, gp, ds, layer):
    b = pl.program_id(0)
    pg = pl.program_id(1)
    t = b * pl.num_programs(1) + pg
    slot = lax.rem(t, 2)
    nrow = H_B * ds
    page = lbuf.shape[3]
    npages = gp * pl.num_programs(1)

    def gather(bb, gg, sl):
        copies = []
        for j in range(gp):
            pid = pt_ref[bb, npages - 1 - (gg * gp + j)]
            cols = pl.ds(j * page, page)
            copies.append(pltpu.make_async_copy(cache_k.at[layer, pid], kbuf.at[sl, :, cols], ksem.at[sl]))
            copies.append(pltpu.make_async_copy(cache_v.at[layer, pid], vbuf.at[sl, :, cols], vsem.at[sl]))
            copies.append(pltpu.make_async_copy(cache_lf.at[layer, pid], lbuf.at[sl, j], lsem.at[sl]))
        return copies

    @pl.when(t == 0)
    def _():
        for cp in gather(0, 0, 0):
            cp.start()

    @pl.when(t + 1 < pl.num_programs(0) * pl.num_programs(1))
    def _():
        nb_, ng_ = _step_after(b, pg)
        for cp in gather(nb_, ng_, 1 - slot):
            cp.start()

    for cp in gather(b, pg, slot):
        cp.wait()

    def expand(x):
        return jnp.concatenate([jnp.broadcast_to(x[h:h + 1, :], (ds, x.shape[1])) for h in range(H_B)], axis=0)

    @pl.when(pg == 0)
    def _():
        q = q_ref[0].astype(F32)
        rh = lax.broadcasted_iota(jnp.int32, (nrow, AB_FOX), 0) // ds
        ch = lax.broadcasted_iota(jnp.int32, (nrow, AB_FOX), 1) // DH
        qbd = jnp.where(rh == ch, jnp.concatenate([q] * H_B, axis=0), 0.0).astype(BF16)
        qbd_ref[...] = qbd
        r = lax.broadcasted_iota(jnp.int32, (nrow, nrow), 0)
        c = lax.broadcasted_iota(jnp.int32, (nrow, nrow), 1)
        m_row = jnp.where((r // ds == c // ds) & (c <= r), 1.0, 0.0)
        cn_col = jnp.sum(m_row * lfr_ref[0], axis=1, keepdims=True)
        cnq_ref[...] = cn_col * LOG2E
        lft = lft_ref[0]
        npad = lft.shape[1]
        kk = lax.broadcasted_iota(jnp.int32, (1, npad), 1)
        cnt = jnp.zeros((H_B, npad), F32)
        for k2 in range(ds):
            cnt = cnt + lft[:, k2:k2 + 1] * jnp.where(kk >= k2, 1.0, 0.0)
        s = _dot_nt(qbd, kn_ref[0]) + (cn_col - expand(cnt)) * LOG2E
        rq = lax.broadcasted_iota(jnp.int32, (nrow, npad), 0) % ds
        ck = lax.broadcasted_iota(jnp.int32, (nrow, npad), 1)
        s = jnp.where(ck <= rq, s, NEG_INF)
        m = jnp.max(s, -1, keepdims=True)
        p = jnp.exp2(s - m)
        m_ref[...] = m
        l_ref[...] = jnp.sum(p, -1, keepdims=True)
        acc_ref[...] = _dot(p.astype(BF16), vn_ref[0])
        car_ref[...] = jnp.zeros_like(car_ref)

    qbd = qbd_ref[...]
    xs = []
    for j in range(gp):
        xs += list(_split3(lbuf[slot, j]))
    yz = _dot(jnp.concatenate(xs, axis=0).astype(BF16), sel_ref[...])
    car = car_ref[...]
    revs = []
    for j in range(gp):
        o = j * 3 * H_B
        y = yz[o:o + H_B] + yz[o + H_B:o + 2 * H_B] + yz[o + 2 * H_B:o + 3 * H_B]
        revs.append(car + y[:, :page])
        car = car + y[:, page:]
    car_ref[...] = car
    kt_all = kbuf[slot].astype(BF16)
    vt_all = vbuf[slot].astype(BF16)
    s = _dot(qbd, kt_all) + expand(jnp.concatenate(revs, axis=1) * LOG2E) + cnq_ref[...]
    m = m_ref[...]
    m_new = jnp.maximum(m, jnp.max(s, -1, keepdims=True))
    alpha = jnp.exp2(m - m_new)
    p = jnp.exp2(s - m_new)
    m_ref[...] = m_new
    l_ref[...] = alpha * l_ref[...] + jnp.sum(p, -1, keepdims=True)
    acc_ref[...] = alpha * acc_ref[...] + _dot_nt(p.astype(BF16), vt_all)

    @pl.when(pg == pl.num_programs(1) - 1)
    def _():
        acc = acc_ref[...] / l_ref[...]
        ch = lax.broadcasted_iota(jnp.int32, (ds, AB_FOX), 1) // DH
        out = jnp.zeros((ds, AB_FOX), F32)
        for h in range(H_B):
            out = out + jnp.where(ch == h, acc[h * ds:(h + 1) * ds, :], 0.0)
        o_ref[0] = out.astype(BF16)


def _fox_decode(page_table, layer, q, kn, vn, lf_row, lf_t, sel, cache_kt, cache_vt, cache_lft, gp):
    nb, ds, _ = q.shape
    npages = page_table.shape[1]
    page = cache_kt.shape[3]
    nrow = H_B * ds

    per_b = lambda b, pg, pt: (b, 0, 0)
    in_specs = [pl.BlockSpec((1,) + a.shape[1:], per_b) for a in (q, kn, vn, lf_row, lf_t)]
    in_specs += [pl.BlockSpec(sel.shape, lambda b, pg, pt: (0, 0))]
    in_specs += [pl.BlockSpec(memory_space=pl.ANY)] * 3
    grid_spec = pltpu.PrefetchScalarGridSpec(
        num_scalar_prefetch=1,
        grid=(nb, npages // gp),
        in_specs=in_specs,
        out_specs=pl.BlockSpec((1, ds, AB_FOX), per_b),
        scratch_shapes=[pltpu.VMEM((2, AB_FOX, gp * page), F32), pltpu.VMEM((2, AB_FOX, gp * page), F32),
                        pltpu.VMEM((2, gp, H_B, page), F32),
                        pltpu.SemaphoreType.DMA((2,)), pltpu.SemaphoreType.DMA((2,)), pltpu.SemaphoreType.DMA((2,)),
                        pltpu.VMEM((nrow, AB_FOX), BF16), pltpu.VMEM((nrow, 1), F32),
                        pltpu.VMEM((nrow, 1), F32), pltpu.VMEM((nrow, AB_FOX), F32),
                        pltpu.VMEM((H_B, page), F32), pltpu.VMEM((nrow, 1), F32)])
    return pl.pallas_call(
        functools.partial(_fox_decode_kernel, gp=gp, ds=ds, layer=layer),
        grid_spec=grid_spec,
        out_shape=jax.ShapeDtypeStruct((nb, ds, AB_FOX), BF16),
        compiler_params=_cparams("arbitrary", "arbitrary"),
    )(page_table, q, kn, vn, lf_row, lf_t, sel, cache_kt, cache_vt, cache_lft)


def _logf_selectors(page):
    a = jnp.arange(page)[:, None]
    b = jnp.arange(page)[None, :]
    return jnp.concatenate([a > b, jnp.ones((page, page), bool)], axis=1).astype(BF16)


def _step_after(b, g):
    last = g == pl.num_programs(1) - 1
    return jnp.where(last, b + 1, b), jnp.where(last, 0, g + 1)


def _mla_decode_kernel(pt_ref, ql_ref, qp_ref, cn_ref, pn_ref, cache_c, cache_p, o_ref,
                       cbuf, pbuf, csem, psem, m_ref, l_ref, acc_ref, *, gp, base):
    b = pl.program_id(0)
    pg = pl.program_id(1)
    t = b * pl.num_programs(1) + pg
    slot = lax.rem(t, 2)
    page = cbuf.shape[1] // gp
    ql = ql_ref[0]
    qp = qp_ref[0]
    nrow = ql.shape[0]

    def gather(bb, gg, sl):
        copies = []
        for j in range(gp):
            pid = base + pt_ref[bb, gg * gp + j]
            rows = pl.ds(j * page, page)
            copies.append(pltpu.make_async_copy(cache_c.at[pid], cbuf.at[sl, rows, :], csem.at[sl]))
            copies.append(pltpu.make_async_copy(cache_p.at[pid], pbuf.at[sl, :, rows], psem.at[sl]))
        return copies

    @pl.when(t == 0)
    def _():
        for cp in gather(0, 0, 0):
            cp.start()

    @pl.when(t + 1 < pl.num_programs(0) * pl.num_programs(1))
    def _():
        nb_, ng_ = _step_after(b, pg)
        for cp in gather(nb_, ng_, 1 - slot):
            cp.start()

    for cp in gather(b, pg, slot):
        cp.wait()

    @pl.when(pg == 0)
    def _():
        cn = cn_ref[0]
        npad = cn.shape[0]
        s = _dot_nt(ql, cn) + _dot_nt(qp, pn_ref[0])
        rq = lax.broadcasted_iota(jnp.int32, (nrow, npad), 0) // H_C
        ck = lax.broadcasted_iota(jnp.int32, (nrow, npad), 1)
        s = jnp.where(ck <= rq, s, NEG_INF)
        m = jnp.max(s, -1, keepdims=True)
        p = jnp.exp2(s - m)
        m_ref[...] = m
        l_ref[...] = jnp.sum(p, -1, keepdims=True)
        acc_ref[...] = _dot(p.astype(BF16), cn)

    c_all = cbuf[slot].astype(BF16)
    kp_all = pbuf[slot].astype(BF16)
    s = _dot_nt(ql, c_all) + _dot(qp, kp_all)
    m = m_ref[...]
    m_new = jnp.maximum(m, jnp.max(s, -1, keepdims=True))
    alpha = jnp.exp2(m - m_new)
    p = jnp.exp2(s - m_new)
    m_ref[...] = m_new
    l_ref[...] = alpha * l_ref[...] + jnp.sum(p, -1, keepdims=True)
    acc_ref[...] = alpha * acc_ref[...] + _dot(p.astype(BF16), c_all)

    @pl.when(pg == pl.num_programs(1) - 1)
    def _():
        o_ref[0] = (acc_ref[...] / l_ref[...]).astype(BF16)


def _mla_decode(page_table, base, ql, qp, cn, pn, cache_c, cache_pt, gp):
    nb, nrow, _ = ql.shape
    npages = page_table.shape[1]
    page = cache_c.shape[1]

    per_b = lambda b, pg, pt: (b, 0, 0)
    in_specs = [pl.BlockSpec((1,) + a.shape[1:], per_b) for a in (ql, qp, cn, pn)]
    in_specs += [pl.BlockSpec(memory_space=pl.ANY)] * 2
    grid_spec = pltpu.PrefetchScalarGridSpec(
        num_scalar_prefetch=1,
        grid=(nb, npages // gp),
        in_specs=in_specs,
        out_specs=pl.BlockSpec((1, nrow, KV_LORA), per_b),
        scratch_shapes=[pltpu.VMEM((2, gp * page, KV_LORA), F32), pltpu.VMEM((2, ROPE_D, gp * page), F32),
                        pltpu.SemaphoreType.DMA((2,)), pltpu.SemaphoreType.DMA((2,)),
                        pltpu.VMEM((nrow, 1), F32), pltpu.VMEM((nrow, 1), F32),
                        pltpu.VMEM((nrow, KV_LORA), F32)])
    return pl.pallas_call(
        functools.partial(_mla_decode_kernel, gp=gp, base=base),
        grid_spec=grid_spec,
        out_shape=jax.ShapeDtypeStruct((nb, nrow, KV_LORA), BF16),
        compiler_params=_cparams("arbitrary", "arbitrary"),
    )(page_table, ql, qp, cn, pn, cache_c, cache_pt)


def _ret_rope_tables(pos):
    half = DK // 2
    inv = ROPE_BASE ** (-jnp.arange(half, dtype=F32) / half)
    ang = pos.astype(F32)[:, None] * inv[None, :]
    cos, sin = jnp.cos(ang), jnp.sin(ang)
    return jnp.concatenate([cos, cos], -1), jnp.concatenate([-sin, sin], -1)


def _mla_rope_tables(pos, lane0, passthrough, scale):
    half = ROPE_D // 2
    inv = ROPE_BASE ** (-jnp.arange(half, dtype=F32) / half)
    ang = pos.astype(F32)[:, None] * inv[None, :]
    cos, sin = jnp.cos(ang) * scale, jnp.sin(ang) * scale
    n = pos.shape[0]
    z = lambda w: jnp.zeros((n, w), F32)
    lead = jnp.full((n, lane0), scale if passthrough else 0.0, F32)
    tail = z(LANES - lane0 - ROPE_D)
    c = jnp.concatenate([lead, cos, cos, tail], -1)
    s1 = jnp.concatenate([z(lane0), -sin, z(half), tail], -1)
    s2 = jnp.concatenate([z(lane0), z(half), sin, tail], -1)
    return c, s1, s2


def _retention_tables(length, reps):
    lg = jnp.log(1.0 - 2.0 ** (-5.0 - jnp.arange(H_A, dtype=F32)))
    idx = jnp.arange(length, dtype=F32)
    diff = idx[:, None] - idx[None, :]
    decay = jnp.where(diff >= 0, jnp.exp(lg[:, None, None] * jnp.maximum(diff, 0.0)[None]), 0.0)
    qd = jnp.exp((idx[:, None] + 1.0) * lg[None, :]).T
    kd = jnp.exp((length - 1.0 - idx)[:, None] * lg[None, :]).T
    dl = jnp.exp(length * lg)
    if reps > 1:
        eye = jnp.eye(reps, dtype=F32)
        decay = jnp.einsum('ab,hij->haibj', eye, decay).reshape(H_A, reps * length, reps * length)
        qd = jnp.tile(qd, (1, reps))
        kd = jnp.tile(kd, (1, reps))
    n = reps * length
    qd = jnp.broadcast_to(qd[:, :, None], (H_A, n, DV))
    kd = jnp.broadcast_to(kd[:, :, None], (H_A, n, DK))
    dl = jnp.broadcast_to(dl[:, None, None], (H_A, DK, DV))
    return decay, qd, kd, dl


def _ones_row(heads, lane):
    return (jnp.arange(heads * LANES) % LANES == lane).astype(F32)[None, :]


def _prep_ab_weights(w_in, b_f):
    main = w_in[:, :4 * AB_RET].astype(BF16)
    fox = w_in[:, 4 * AB_RET:4 * AB_RET + 3 * AB_FOX].reshape(D_MODEL, 3 * H_B, DH)
    fox = jnp.pad(fox, ((0, 0), (0, 0), (0, LANES - DH))).reshape(D_MODEL, 3 * FOX_PAD).astype(BF16)
    w_f = jnp.pad(w_in[:, 4 * AB_RET + 3 * AB_FOX:], ((0, 0), (0, LANES - H_B))).astype(BF16)
    bf_row = jnp.pad(b_f, (0, LANES - H_B))[None, :]
    return main, fox, w_f, bf_row, _ones_row(H_B, DH)


def _prep_c_weights(w_in, q_g, w_qb, kv_g, w_kvb):
    win = w_in[:, :Q_LORA + KV_LORA].astype(BF16)
    wpe = jnp.pad(w_in[:, Q_LORA + KV_LORA:], ((0, 0), (0, LANES - ROPE_D))).astype(BF16)
    wqb = w_qb.reshape(Q_LORA, H_C, NOPE + ROPE_D)
    wqb = jnp.pad(wqb, ((0, 0), (0, 0), (0, LANES - NOPE - ROPE_D))).reshape(Q_LORA, H_C * LANES).astype(BF16)
    wkv = w_kvb.reshape(KV_LORA, H_C, NOPE + V_DIM)
    w_uk, w_uv = wkv[..., :NOPE], wkv[..., NOPE:]
    wk = jnp.pad(w_uk, ((0, 0), (0, 0), (0, LANES - NOPE))).reshape(KV_LORA, H_C * LANES).astype(BF16)
    wv = jnp.pad(w_uv, ((0, 0), (0, 0), (0, LANES - V_DIM))).reshape(KV_LORA, H_C * LANES).astype(BF16)
    r = jnp.arange(LANES)[:, None]
    c = jnp.arange(H_C * LANES)[None, :]
    place = ((r < ROPE_D) & (c % LANES == NOPE + r)).astype(BF16)
    w_abs = jnp.pad(w_uk.transpose(1, 2, 0), ((0, 0), (0, LANES - NOPE), (0, 0))).astype(BF16)
    w_val = w_uv.transpose(1, 0, 2).astype(BF16)
    return (win, wpe, q_g[None, :], kv_g[None, :], wqb, wk, wv, place, _ones_row(H_C, V_DIM)), w_abs, w_val


TM = 512
TM_MLP = 1024
TF_MLP = 1024
TQ, TK = 512, 512
TC = 512
RET_SUB = 4
RET_BB = 16
FOX_HG = 4
MLA_HG = 4
PAGES_PER_STEP = 16
NEW_PAD = 16


def kernel(x_prompt, x_sample, state_ret, cache_fox_k, cache_fox_v, cache_fox_logf, cache_mla_ckv,
           cache_mla_kpe, page_table, w_in_ab, ret_gn_g, fox_b_f, w_out_ab, w_in_c, mla_q_norm_g,
           mla_w_qb, mla_kv_norm_g, mla_w_kvb, w_out_c, ln_mix_g, ln_mix_b, ln_mlp_g, ln_mlp_b,
           mlp_w1, mlp_w2):
    batch, seq, _ = x_prompt.shape
    nb, ds, _ = x_sample.shape
    n_pool, page = cache_fox_k.shape[1], cache_fox_k.shape[2]
    past_len = page_table.shape[1] * page
    mp, ms = batch * seq, nb * ds
    pos_p = jnp.arange(seq)
    pos_s = past_len + jnp.arange(ds)
    pos_s_rows = jnp.tile(pos_s, nb)

    hp = x_prompt.reshape(mp, D_MODEL)
    hs = x_sample.reshape(ms, D_MODEL)
    tms = min(TM, ms)
    outs = {k: [] for k in ("ret_p", "ret_s", "fk_p", "fv_p", "fl_p", "fk_s", "fv_s", "fl_s",
                            "ck_p", "kp_p", "ck_s", "kp_s")}

    for l in range(DEPTH):
        i = l // 2
        row = lambda a: a[None, :]
        if l % 2 == 0:
            wts = _prep_ab_weights(w_in_ab[i], fox_b_f[i])
            w_out = w_out_ab[i].astype(BF16)
            gn = ret_gn_g[i][None, :]
            cos, sin = _ret_rope_tables(pos_p)
            rq, rk, rv, rg, fq, fk, fv, fk16, fv16, lf = _proj_ab(hp, *wts[:3], cos, sin, *wts[3:], TM)
            y, s_fin = _ret_prompt(rq, rk, rv, rg, _retention_tables(RET_CHUNK, 1), gn, batch, seq, RET_SUB)
            fqb, fkb = _fox_bias(lf, fq, fk16, batch, seq, TC)
            fo = _flash(fqb, fkb, fv16, batch, seq, H_B, DH, FOX_HG, TQ, TK)
            mix_parts_p = [y, fo]
            unpad = lambda a, n: a.reshape(n, H_B, LANES)[:, :, :DH]
            outs["ret_p"].append(s_fin)
            outs["fk_p"].append(unpad(fk, mp).reshape(batch, seq, H_B, DH))
            outs["fv_p"].append(unpad(fv, mp).reshape(batch, seq, H_B, DH))
            outs["fl_p"].append(lf.reshape(batch, seq, H_B))
            cos, sin = _ret_rope_tables(pos_s_rows)
            rq, rk, rv, rg, fq, fk, fv, fk16, fv16, lf = _proj_ab(hs, *wts[:3], cos, sin, *wts[3:], tms)
            y, s_new = _ret_sample(rq, rk, rv, rg, state_ret[i], _retention_tables(ds, RET_BB), gn, RET_BB, ds)
            padn = lambda a: jnp.pad(unpad(a, ms).reshape(nb, ds, AB_FOX), ((0, 0), (0, NEW_PAD - ds), (0, 0)))
            lf_hq = lf.reshape(nb, ds, H_B).transpose(0, 2, 1)
            fo = _fox_decode(page_table, i, unpad(fq, ms).reshape(nb, ds, AB_FOX), padn(fk16), padn(fv16),
                             lf_hq.reshape(nb, 1, H_B * ds),
                             jnp.pad(lf_hq, ((0, 0), (0, 0), (0, NEW_PAD - ds))),
                             _logf_selectors(page),
                             cache_fox_k.transpose(0, 1, 3, 4, 2).reshape(-1, n_pool, AB_FOX, page),
                             cache_fox_v.transpose(0, 1, 3, 4, 2).reshape(-1, n_pool, AB_FOX, page),
                             cache_fox_logf.transpose(0, 1, 3, 2), PAGES_PER_STEP)
            mix_parts_s = [y, fo.reshape(ms, AB_FOX)]
            outs["ret_s"].append(s_new)
            outs["fk_s"].append(unpad(fk, ms).reshape(nb, ds, H_B, DH))
            outs["fv_s"].append(unpad(fv, ms).reshape(nb, ds, H_B, DH))
            outs["fl_s"].append(lf.reshape(nb, ds, H_B))
        else:
            wts, w_abs, w_val = _prep_c_weights(w_in_c[i], mla_q_norm_g[i], mla_w_qb[i],
                                                mla_kv_norm_g[i], mla_w_kvb[i])
            w_out = w_out_c[i].astype(BF16)
            qscale = (NOPE + ROPE_D) ** -0.5 * LOG2E
            tabs = _mla_rope_tables(pos_p, NOPE, True, qscale) + _mla_rope_tables(pos_p, 0, False, 1.0)
            q, k, v, ckv, kpe = _proj_c(hp, wts, tabs, TM)
            o = _flash(q, k, v, batch, seq, H_C, V_DIM, MLA_HG, TQ, TK)
            mix_parts_p = [o]
            outs["ck_p"].append(ckv.reshape(batch, seq, KV_LORA))
            outs["kp_p"].append(kpe.reshape(batch, seq, ROPE_D))
            tabs = _mla_rope_tables(pos_s_rows, NOPE, True, qscale) + _mla_rope_tables(pos_s_rows, 0, False, 1.0)
            q, _, _, ckv, kpe = _proj_c(hs, wts, tabs, tms)
            q_lat = _head_mm(q, w_abs, 1).reshape(nb, ds * H_C, KV_LORA)
            q_pe = q.reshape(nb, ds * H_C, LANES)[:, :, NOPE:NOPE + ROPE_D]
            padn = lambda a: jnp.pad(a.reshape(nb, ds, -1), ((0, 0), (0, NEW_PAD - ds), (0, 0))).astype(BF16)
            o_lat = _mla_decode(page_table, i * n_pool, q_lat, q_pe, padn(ckv), padn(kpe),
                                cache_mla_ckv.reshape(-1, page, KV_LORA),
                                cache_mla_kpe.transpose(0, 1, 3, 2).reshape(-1, ROPE_D, page), PAGES_PER_STEP)
            o = _head_mm(o_lat.reshape(ms, H_C * KV_LORA), w_val, 2)
            mix_parts_s = [o]
            outs["ck_s"].append(ckv.reshape(nb, ds, KV_LORA))
            outs["kp_s"].append(kpe.reshape(nb, ds, ROPE_D))

        w1 = mlp_w1[l].astype(BF16)
        w2 = mlp_w2[l].astype(BF16)
        hp = _outproj_ln(mix_parts_p, w_out, hp, row(ln_mix_g[l]), row(ln_mix_b[l]), TM)
        hs = _outproj_ln(mix_parts_s, w_out, hs, row(ln_mix_g[l]), row(ln_mix_b[l]), tms)
        hp = _mlp_ln(hp, w1, w2, row(ln_mlp_g[l]), row(ln_mlp_b[l]), TM_MLP, TF_MLP)
        hs = _mlp_ln(hs, w1, w2, row(ln_mlp_g[l]), row(ln_mlp_b[l]), min(TM_MLP, ms), TF_MLP)

    st = lambda k: jnp.stack(outs[k])
    return (hp.reshape(batch, seq, D_MODEL), hs.reshape(nb, ds, D_MODEL),
            st("ret_p"), st("ret_s"), st("fk_p"), st("fv_p"), st("fl_p"),
            st("fk_s"), st("fv_s"), st("fl_s"), st("ck_p"), st("kp_p"), st("ck_s"), st("kp_s"))
```

```python
import functools
import math

import jax
import jax.numpy as jnp
from jax import lax
from jax.experimental import pallas as pl
from jax.experimental.pallas import tpu as pltpu

F32 = jnp.float32
BF16 = jnp.bfloat16

D_MODEL = 1024
DEPTH = 2
H_A, DK, DV = 4, 128, 128
RET_CHUNK = 128
H_B, DH = 8, 64
H_C, NOPE, ROPE_D, V_DIM = 16, 64, 32, 64
Q_LORA, KV_LORA = 768, 256
D_FF = 4 * D_MODEL
ROPE_BASE = 10000.0
LN_EPS = 1e-5
RMS_EPS = 1e-6
GN_EPS = 1e-6
ALPHA = (2 * DEPTH) ** 0.25
AB_RET = H_A * DK
AB_FOX = H_B * DH
FOX_PAD = H_B * 128
LANES = 128
LOG2E = math.log2(math.e)
VMEM_LIMIT = 56 * 1024 * 1024
NEG_INF = float("-inf")
HIGHEST = lax.Precision.HIGHEST


def _cparams(*sem):
    return pltpu.CompilerParams(dimension_semantics=sem, vmem_limit_bytes=VMEM_LIMIT)


def _dot(a, b):
    return jnp.dot(a, b, preferred_element_type=F32)


def _dot_nt(a, b):
    return lax.dot_general(a, b, (((1,), (1,)), ((), ())), preferred_element_type=F32)


def _layer_norm(z, g, b):
    mu = jnp.mean(z, -1, keepdims=True)
    var = jnp.mean(jnp.square(z - mu), -1, keepdims=True)
    return (z - mu) * lax.rsqrt(var + LN_EPS) * g + b


def _rms_norm(z, g):
    return z * lax.rsqrt(jnp.mean(z * z, -1, keepdims=True) + RMS_EPS) * g


def _log_sigmoid(x):
    return jnp.minimum(x, 0.0) - jnp.log1p(jnp.exp(-jnp.abs(x)))


def _full(shape):
    nd = len(shape)
    return pl.BlockSpec(shape, lambda *_: (0,) * nd)


def _proj_ab_kernel(x_ref, w_ref, wfox_ref, wf_ref, wkvt_ref, cos_ref, sin_ref, bf_ref, ones_ref,
                    rq_ref, rk_ref, rv_ref, rg_ref, fq_ref, fk16_ref, fv16_ref, lf_ref, fk_ref, fv_ref,
                    *, seq_minor):
    xb = x_ref[...].astype(BF16)
    cos = cos_ref[...]
    sin = sin_ref[...]

    def slab(j):
        return _dot(xb, w_ref[:, j * AB_RET:(j + 1) * AB_RET])

    def fox_slab(j):
        return _dot(xb, wfox_ref[:, j * FOX_PAD:(j + 1) * FOX_PAD])

    def rope(h):
        parts = []
        for g in range(H_A):
            seg = h[:, g * DK:(g + 1) * DK]
            parts.append(seg * cos + pltpu.roll(seg, DK // 2, 1) * sin)
        return jnp.concatenate(parts, axis=1)

    rq_ref[...] = rope(slab(0))
    rk_ref[...] = rope(slab(1)) * (DK ** -0.5)
    rv_ref[...] = slab(2)
    rg_ref[...] = slab(3)
    fq_ref[...] = (fox_slab(0) * (DH ** -0.5 * LOG2E)).astype(BF16)
    fk = fox_slab(1)
    fv = fox_slab(2)
    fk16_ref[...] = fk.astype(BF16)
    fv16_ref[...] = (fv + ones_ref[...]).astype(BF16)
    ff = _dot(xb, wf_ref[...]) + bf_ref[...]
    lf_ref[...] = _log_sigmoid(ff)[:, :H_B]
    if seq_minor:
        kvt = _dot_nt(wkvt_ref[...], xb)
        fk_ref[0] = kvt[:AB_FOX]
        fv_ref[0] = kvt[AB_FOX:]
    else:
        fk_ref[...] = fk
        fv_ref[...] = fv


def _proj_ab(x, w_main, w_fox, w_f, w_kvt, cos, sin, bf_row, ones_row, tm, seq):
    m = x.shape[0]
    ntab = cos.shape[0] // tm
    row = lambda i: (i, 0)
    tab = lambda i: (i % ntab, 0)
    ret = lambda dt: jax.ShapeDtypeStruct((m, AB_RET), dt)
    fox = lambda dt: jax.ShapeDtypeStruct((m, FOX_PAD), dt)
    rspec = pl.BlockSpec((tm, AB_RET), row)
    fspec = pl.BlockSpec((tm, FOX_PAD), row)
    if seq is None:
        kv_shape, kv_spec = fox(F32), fspec
    else:
        per = seq // tm
        kv_shape = jax.ShapeDtypeStruct((m // seq, AB_FOX, seq), F32)
        kv_spec = pl.BlockSpec((1, AB_FOX, tm), lambda i: (i // per, 0, i % per))
    outs = (ret(F32), ret(F32), ret(F32), ret(F32), fox(BF16), fox(BF16), fox(BF16),
            jax.ShapeDtypeStruct((m, H_B), F32), kv_shape, kv_shape)
    return pl.pallas_call(
        functools.partial(_proj_ab_kernel, seq_minor=seq is not None),
        grid=(m // tm,),
        in_specs=[pl.BlockSpec((tm, D_MODEL), row), _full(w_main.shape), _full(w_fox.shape), _full(w_f.shape),
                  _full(w_kvt.shape), pl.BlockSpec((tm, LANES), tab), pl.BlockSpec((tm, LANES), tab),
                  _full(bf_row.shape), _full(ones_row.shape)],
        out_specs=[rspec] * 4 + [fspec] * 3 + [pl.BlockSpec((tm, H_B), row), kv_spec, kv_spec],
        out_shape=outs,
        compiler_params=_cparams("parallel"),
    )(x, w_main, w_fox, w_f, w_kvt, cos, sin, bf_row, ones_row)


def _split3(x):
    x1 = x.astype(BF16).astype(F32)
    x2 = (x - x1).astype(BF16).astype(F32)
    x3 = (x - x1 - x2).astype(BF16).astype(F32)
    return x1, x2, x3


def _fox_bias_kernel(lf_ref, q_ref, k_ref, qo_ref, ko_ref, car_ref):
    @pl.when(pl.program_id(1) == 0)
    def _():
        car_ref[...] = jnp.zeros_like(car_ref)

    tc = lf_ref.shape[0]
    r = lax.broadcasted_iota(jnp.int32, (tc, tc), 0)
    c = lax.broadcasted_iota(jnp.int32, (tc, tc), 1)
    lower = (c <= r).astype(F32)
    cs = jnp.dot(lower, lf_ref[...], precision=HIGHEST, preferred_element_type=F32) + car_ref[...]
    car_ref[...] = cs[tc - 1:tc, :]
    cs = cs * LOG2E
    lane = lax.broadcasted_iota(jnp.int32, (tc, LANES), 1)
    ones_q = jnp.where((lane >= DH + 3) & (lane < DH + 6), 1.0, 0.0)
    ones_k = jnp.where((lane >= DH) & (lane < DH + 3), 1.0, 0.0)
    for h in range(H_B):
        lanes = slice(h * LANES, (h + 1) * LANES)
        parts = _split3(cs[:, h:h + 1])
        q_add = ones_q
        k_add = ones_k
        for t, part in enumerate(parts):
            q_add = q_add + jnp.where(lane == DH + t, part, 0.0)
            k_add = k_add - jnp.where(lane == DH + 3 + t, part, 0.0)
        qo_ref[:, lanes] = (q_ref[:, lanes].astype(F32) + q_add).astype(BF16)
        ko_ref[:, lanes] = (k_ref[:, lanes].astype(F32) + k_add).astype(BF16)


def _fox_bias(lf, q, k, batch, seq, tc):
    nc = seq // tc
    row = lambda b, j: (b * nc + j, 0)
    wide = pl.BlockSpec((tc, FOX_PAD), row)
    return pl.pallas_call(
        _fox_bias_kernel,
        grid=(batch, nc),
        in_specs=[pl.BlockSpec((tc, H_B), row), wide, wide],
        out_specs=[wide, wide],
        out_shape=(jax.ShapeDtypeStruct(q.shape, BF16), jax.ShapeDtypeStruct(k.shape, BF16)),
        scratch_shapes=[pltpu.VMEM((1, H_B), F32)],
        compiler_params=_cparams("arbitrary", "arbitrary"),
    )(lf, q, k)


def _gn_gate(o, gate, gn):
    mu = jnp.mean(o, -1, keepdims=True)
    var = jnp.mean(jnp.square(o - mu), -1, keepdims=True)
    y = (o - mu) * lax.rsqrt(var + GN_EPS) * gn
    return (y * (gate / (1.0 + jnp.exp(-gate)))).astype(BF16)


def _ret_prompt_kernel(q_ref, k_ref, v_ref, g_ref, dec_ref, qd_ref, kd_ref, dl_ref, gn_ref,
                       y_ref, sfin_ref, st_ref, *, nsub):
    @pl.when(pl.program_id(1) == 0)
    def _():
        st_ref[...] = jnp.zeros_like(st_ref)

    for s in range(nsub):
        rows = slice(s * RET_CHUNK, (s + 1) * RET_CHUNK)
        for h in range(H_A):
            cols = slice(h * DK, (h + 1) * DK)
            q16 = q_ref[rows, cols].astype(BF16)
            k = k_ref[rows, cols]
            v16 = v_ref[rows, cols].astype(BF16)
            st = st_ref[h]
            sc = _dot_nt(q16, k.astype(BF16)) * dec_ref[h]
            o = _dot(sc.astype(BF16), v16) + qd_ref[h] * _dot(q16, st.astype(BF16))
            kdt = (k * kd_ref[h]).T.astype(BF16)
            st_ref[h] = dl_ref[h] * st + _dot(kdt, v16)
            y_ref[rows, cols] = _gn_gate(o, g_ref[rows, cols], gn_ref[:, cols])
    sfin_ref[0] = st_ref[...]


def _ret_prompt(rq, rk, rv, rg, tabs, gn, batch, seq, nsub):
    tr = nsub * RET_CHUNK
    nc = seq // tr
    row = lambda b, c: (b * nc + c, 0)
    blk = pl.BlockSpec((tr, AB_RET), row)
    dec, qd, kd, dl = tabs
    return pl.pallas_call(
        functools.partial(_ret_prompt_kernel, nsub=nsub),
        grid=(batch, nc),
        in_specs=[blk, blk, blk, blk, _full(dec.shape), _full(qd.shape), _full(kd.shape),
                  _full(dl.shape), _full(gn.shape)],
        out_specs=[blk, pl.BlockSpec((1, H_A, DK, DV), lambda b, c: (b, 0, 0, 0))],
        out_shape=(jax.ShapeDtypeStruct(rq.shape, BF16),
                   jax.ShapeDtypeStruct((batch, H_A, DK, DV), F32)),
        scratch_shapes=[pltpu.VMEM((H_A, DK, DV), F32)],
        compiler_params=_cparams("arbitrary", "arbitrary"),
    )(rq, rk, rv, rg, dec, qd, kd, dl, gn)


def _ret_sample_kernel(q_ref, k_ref, v_ref, g_ref, st_ref, dec_ref, qd_ref, kd_ref, dl_ref, gn_ref,
                       y_ref, snew_ref, *, bb, ds):
    n = bb * ds
    rb = lax.broadcasted_iota(jnp.int32, (n, DV), 0) // ds
    for h in range(H_A):
        cols = slice(h * DK, (h + 1) * DK)
        q = q_ref[:, cols]
        k = k_ref[:, cols]
        v = v_ref[:, cols]
        v16 = v.astype(BF16)
        sc = _dot_nt(q.astype(BF16), k.astype(BF16)) * dec_ref[h]
        o_intra = _dot(sc.astype(BF16), v16)
        kdt = (k * kd_ref[h]).T.astype(BF16)
        dl = dl_ref[h]

        def per_seq(b, o_cross, h=h, q=q, v=v, kdt=kdt, dl=dl):
            st = st_ref[b, h]
            qb = jnp.where(rb == b, q, 0.0).astype(BF16)
            vb = jnp.where(rb == b, v, 0.0).astype(BF16)
            snew_ref[b, h] = dl * st + _dot(kdt, vb)
            return o_cross + _dot(qb, st.astype(BF16))

        o_cross = lax.fori_loop(0, bb, per_seq, jnp.zeros((n, DV), F32), unroll=4)
        o = o_intra + qd_ref[h] * o_cross
        y_ref[:, cols] = _gn_gate(o, g_ref[:, cols], gn_ref[:, cols])


def _ret_sample(rq, rk, rv, rg, state, tabs, gn, bb, ds):
    nb = state.shape[0]
    n = bb * ds
    row = lambda i: (i, 0)
    blk = pl.BlockSpec((n, AB_RET), row)
    sblk = pl.BlockSpec((bb, H_A, DK, DV), lambda i: (i, 0, 0, 0))
    dec, qd, kd, dl = tabs
    return pl.pallas_call(
        functools.partial(_ret_sample_kernel, bb=bb, ds=ds),
        grid=(nb // bb,),
        in_specs=[blk, blk, blk, blk, sblk, _full(dec.shape), _full(qd.shape), _full(kd.shape),
                  _full(dl.shape), _full(gn.shape)],
        out_specs=[blk, sblk],
        out_shape=(jax.ShapeDtypeStruct(rq.shape, BF16), jax.ShapeDtypeStruct(state.shape, F32)),
        compiler_params=_cparams("parallel"),
    )(rq, rk, rv, rg, state, dec, qd, kd, dl, gn)


def _flash_kernel(q_ref, k_ref, v_ref, o_ref, *, hg, dout, tq, tk):
    i = pl.program_id(2)
    ratio = tq // tk
    ahead = lax.broadcasted_iota(jnp.int32, (tq, tk), 1) - lax.broadcasted_iota(jnp.int32, (tq, tk), 0)

    def step(j, carry):
        ks = pl.ds(pl.multiple_of(j * tk, tk), tk)
        visible = ahead <= i * tq - j * tk
        out = []
        for h in range(hg):
            lanes = slice(h * LANES, (h + 1) * LANES)
            m, acc = carry[h]
            s = jnp.where(visible, _dot_nt(q_ref[:, lanes], k_ref[ks, lanes]), NEG_INF)
            m_new = jnp.maximum(m, jnp.max(s, -1, keepdims=True))
            p = jnp.exp2((s - m_new).astype(BF16))
            acc = jnp.exp2(m - m_new) * acc + _dot(p, v_ref[ks, lanes])
            out.append((m_new, acc))
        return tuple(out)

    def trip(jj, carry):
        for u in range(FLASH_UNROLL):
            carry = step(FLASH_UNROLL * jj + u, carry)
        return carry

    def tail(jj, carry):
        for u in range(ratio):
            carry = step(ratio * jj + u, carry)
        return carry

    init = (jnp.full((tq, 1), NEG_INF, F32), jnp.zeros((tq, LANES), F32))
    nsteps = (i + 1) * ratio
    ntrip = nsteps // FLASH_UNROLL
    carry = lax.fori_loop(0, ntrip, trip, (init,) * hg)
    carry = lax.fori_loop(ntrip * (FLASH_UNROLL // ratio), i + 1, tail, carry)
    for h in range(hg):
        acc = carry[h][1]
        o_ref[:, h * dout:(h + 1) * dout] = (acc[:, :dout] / acc[:, dout:dout + 1]).astype(BF16)


def _flash(q, k, v, batch, seq, heads, dout, hg, tq, tk):
    assert seq % tq == 0 and tq % tk == 0 and FLASH_UNROLL % (tq // tk) == 0 and heads % hg == 0
    nq = seq // tq
    ngrp = heads // hg
    return pl.pallas_call(
        functools.partial(_flash_kernel, hg=hg, dout=dout, tq=tq, tk=tk),
        grid=(batch, ngrp, nq),
        in_specs=[pl.BlockSpec((tq, hg * LANES), lambda b, g, i: (b * nq + i, g)),
                  pl.BlockSpec((seq, hg * LANES), lambda b, g, i: (b, g)),
                  pl.BlockSpec((seq, hg * LANES), lambda b, g, i: (b, g))],
        out_specs=pl.BlockSpec((tq, hg * dout), lambda b, g, i: (b * nq + i, g)),
        out_shape=jax.ShapeDtypeStruct((batch * seq, heads * dout), BF16),
        compiler_params=_cparams("parallel", "parallel", "arbitrary"),
    )(q, k, v)


def _outproj_ln_kernel(*refs, n_in):
    a_refs = refs[:n_in]
    w_ref, x_ref, g_ref, b_ref, o_ref = refs[n_in:]
    acc = None
    off = 0
    for a_ref in a_refs:
        ka = a_ref.shape[1]
        d = _dot(a_ref[...], w_ref[off:off + ka, :])
        acc = d if acc is None else acc + d
        off += ka
    o_ref[...] = _layer_norm(ALPHA * x_ref[...] + acc, g_ref[...], b_ref[...])


def _outproj_ln(parts, w, x, g, b, tm):
    m = x.shape[0]
    row = lambda i: (i, 0)
    in_specs = [pl.BlockSpec((tm, p.shape[1]), row) for p in parts]
    in_specs += [_full(w.shape), pl.BlockSpec((tm, D_MODEL), row), _full(g.shape), _full(b.shape)]
    return pl.pallas_call(
        functools.partial(_outproj_ln_kernel, n_in=len(parts)),
        grid=(m // tm,),
        in_specs=in_specs,
        out_specs=pl.BlockSpec((tm, D_MODEL), row),
        out_shape=jax.ShapeDtypeStruct((m, D_MODEL), F32),
        compiler_params=_cparams("parallel"),
    )(*parts, w, x, g, b)


def _mlp_ln_kernel(x_ref, w1_ref, w2_ref, g_ref, b_ref, o_ref, xb_ref, acc_ref):
    f = pl.program_id(1)

    @pl.when(f == 0)
    def _():
        xb_ref[...] = x_ref[...].astype(BF16)
        acc_ref[...] = jnp.zeros_like(acc_ref)

    h = jnp.maximum(_dot(xb_ref[...], w1_ref[...]), 0.0)
    acc_ref[...] += _dot((h * h).astype(BF16), w2_ref[...])

    @pl.when(f == pl.num_programs(1) - 1)
    def _():
        o_ref[...] = _layer_norm(ALPHA * x_ref[...] + acc_ref[...], g_ref[...], b_ref[...])


def _mlp_ln(x, w1, w2, g, b, tm, tf):
    m = x.shape[0]
    return pl.pallas_call(
        _mlp_ln_kernel,
        grid=(m // tm, D_FF // tf),
        in_specs=[pl.BlockSpec((tm, D_MODEL), lambda i, f: (i, 0)),
                  pl.BlockSpec((D_MODEL, tf), lambda i, f: (0, f)),
                  pl.BlockSpec((tf, D_MODEL), lambda i, f: (f, 0)),
                  _full(g.shape), _full(b.shape)],
        out_specs=pl.BlockSpec((tm, D_MODEL), lambda i, f: (i, 0)),
        out_shape=jax.ShapeDtypeStruct((m, D_MODEL), F32),
        scratch_shapes=[pltpu.VMEM((tm, D_MODEL), BF16), pltpu.VMEM((tm, D_MODEL), F32)],
        compiler_params=_cparams("parallel", "arbitrary"),
    )(x, w1, w2, g, b)


def _rot_lanes(x, c, s1, s2):
    return x * c + pltpu.roll(x, LANES - ROPE_D // 2, 1) * s1 + pltpu.roll(x, ROPE_D // 2, 1) * s2


def _proj_c_kernel(x_ref, win_ref, wpe_ref, qg_ref, kvg_ref, wqb_ref, wk_ref, wv_ref, place_ref, ones_ref,
                   qc_ref, qs1_ref, qs2_ref, kc_ref, ks1_ref, ks2_ref,
                   q_ref, k_ref, v_ref, ckv_ref, kpe_ref):
    xb = x_ref[...].astype(BF16)
    qa = _dot(xb, win_ref[:, :Q_LORA])
    kva = _dot(xb, win_ref[:, Q_LORA:Q_LORA + KV_LORA])
    kpe = _dot(xb, wpe_ref[...])
    qn = _rms_norm(qa, qg_ref[...]).astype(BF16)
    qc, qs1, qs2 = qc_ref[...], qs1_ref[...], qs2_ref[...]
    for h in range(H_C):
        seg = _dot(qn, wqb_ref[:, h * LANES:(h + 1) * LANES])
        q_ref[:, h * LANES:(h + 1) * LANES] = _rot_lanes(seg, qc, qs1, qs2).astype(BF16)
    ckv = _rms_norm(kva, kvg_ref[...])
    ckv_ref[...] = ckv
    ckv16 = ckv.astype(BF16)
    kpe_rot = _rot_lanes(kpe, kc_ref[...], ks1_ref[...], ks2_ref[...])
    kpe_ref[...] = kpe_rot[:, :ROPE_D]
    kpe16 = kpe_rot.astype(BF16)
    k_ref[...] = (_dot(ckv16, wk_ref[...]) + _dot(kpe16, place_ref[...])).astype(BF16)
    v_ref[...] = (_dot(ckv16, wv_ref[...]) + ones_ref[...]).astype(BF16)


def _proj_c(x, wts, tabs, tm):
    m = x.shape[0]
    ntab = tabs[0].shape[0] // tm
    row = lambda i: (i, 0)
    tab = pl.BlockSpec((tm, LANES), lambda i: (i % ntab, 0))
    return pl.pallas_call(
        _proj_c_kernel,
        grid=(m // tm,),
        in_specs=[pl.BlockSpec((tm, D_MODEL), row)] + [_full(w.shape) for w in wts] + [tab] * 6,
        out_specs=[pl.BlockSpec((tm, H_C * LANES), row)] * 3
        + [pl.BlockSpec((tm, KV_LORA), row), pl.BlockSpec((tm, ROPE_D), row)],
        out_shape=(jax.ShapeDtypeStruct((m, H_C * LANES), BF16),) * 3
        + (jax.ShapeDtypeStruct((m, KV_LORA), F32), jax.ShapeDtypeStruct((m, ROPE_D), F32)),
        compiler_params=_cparams("parallel"),
    )(x, *wts, *tabs)


def _head_mm_kernel(x_ref, w_ref, o_ref, *, hp, din, dout):
    for j in range(hp):
        o_ref[:, j * dout:(j + 1) * dout] = _dot(x_ref[:, j * din:(j + 1) * din], w_ref[j]).astype(BF16)


def _head_mm(x, w, hp):
    m = x.shape[0]
    heads, din, dout = w.shape
    return pl.pallas_call(
        functools.partial(_head_mm_kernel, hp=hp, din=din, dout=dout),
        grid=(heads // hp,),
        in_specs=[pl.BlockSpec((m, hp * din), lambda h: (0, h)),
                  pl.BlockSpec((hp, din, dout), lambda h: (h, 0, 0))],
        out_specs=pl.BlockSpec((m, hp * dout), lambda h: (0, h)),
        out_shape=jax.ShapeDtypeStruct((m, heads * dout), BF16),
        compiler_params=_cparams("parallel"),
    )(x, w)


def _fox_decode_kernel(pt_ref, q_ref, kn_ref, vn_ref, lfr_ref, lft_ref, sel_ref, cache_k, cache_v, cache_lf,
                       o_ref, kbuf, vbuf, lbuf, ksem, vsem, lsem,
                       qbd_ref, m_ref, l_ref, acc_ref, car_ref, cnq_ref, *, gp, ds, layer):
    b = pl.program_id(0)
    pg = pl.program_id(1)
    t = b * pl.num_programs(1) + pg
    slot = lax.rem(t, 2)
    nrow = H_B * ds
    page = lbuf.shape[3]
    npages = gp * pl.num_programs(1)

    def gather(bb, gg, sl):
        copies = []
        for j in range(gp):
            pid = pt_ref[bb, npages - 1 - (gg * gp + j)]
            cols = pl.ds(j * page, page)
            copies.append(pltpu.make_async_copy(cache_k.at[layer, pid], kbuf.at[sl, :, cols], ksem.at[sl]))
            copies.append(pltpu.make_async_copy(cache_v.at[layer, pid], vbuf.at[sl, :, cols], vsem.at[sl]))
            copies.append(pltpu.make_async_copy(cache_lf.at[layer, pid], lbuf.at[sl, j], lsem.at[sl]))
        return copies

    @pl.when(t == 0)
    def _():
        for cp in gather(0, 0, 0):
            cp.start()

    @pl.when(t + 1 < pl.num_programs(0) * pl.num_programs(1))
    def _():
        nb_, ng_ = _step_after(b, pg)
        for cp in gather(nb_, ng_, 1 - slot):
            cp.start()

    for cp in gather(b, pg, slot):
        cp.wait()

    def expand(x):
        return jnp.concatenate([jnp.broadcast_to(x[h:h + 1, :], (ds, x.shape[1])) for h in range(H_B)], axis=0)

    @pl.when(pg == 0)
    def _():
        q = q_ref[0].astype(F32)
        rh = lax.broadcasted_iota(jnp.int32, (nrow, AB_FOX), 0) // ds
        ch = lax.broadcasted_iota(jnp.int32, (nrow, AB_FOX), 1) // DH
        qbd = jnp.where(rh == ch, jnp.concatenate([q] * H_B, axis=0), 0.0).astype(BF16)
        qbd_ref[...] = qbd
        r = lax.broadcasted_iota(jnp.int32, (nrow, nrow), 0)
        c = lax.broadcasted_iota(jnp.int32, (nrow, nrow), 1)
        m_row = jnp.where((r // ds == c // ds) & (c <= r), 1.0, 0.0)
        cn_col = jnp.sum(m_row * lfr_ref[0], axis=1, keepdims=True)
        cnq_ref[...] = cn_col * LOG2E
        lft = lft_ref[0]
        npad = lft.shape[1]
        kk = lax.broadcasted_iota(jnp.int32, (1, npad), 1)
        cnt = jnp.zeros((H_B, npad), F32)
        for k2 in range(ds):
            cnt = cnt + lft[:, k2:k2 + 1] * jnp.where(kk >= k2, 1.0, 0.0)
        s = _dot_nt(qbd, kn_ref[0]) + (cn_col - expand(cnt)) * LOG2E
        rq = lax.broadcasted_iota(jnp.int32, (nrow, npad), 0) % ds
        ck = lax.broadcasted_iota(jnp.int32, (nrow, npad), 1)
        s = jnp.where(ck <= rq, s, NEG_INF)
        m = jnp.max(s, -1, keepdims=True)
        p = jnp.exp2(s - m)
        m_ref[...] = m
        l_ref[...] = jnp.sum(p, -1, keepdims=True)
        acc_ref[...] = _dot(p.astype(BF16), vn_ref[0])
        car_ref[...] = jnp.zeros_like(car_ref)

    qbd = qbd_ref[...]
    xs = []
    for j in range(gp):
        xs += list(_split3(lbuf[slot, j]))
    yz = _dot(jnp.concatenate(xs, axis=0).astype(BF16), sel_ref[...])
    car = car_ref[...]
    revs = []
    for j in range(gp):
        o = j * 3 * H_B
        y = yz[o:o + H_B] + yz[o + H_B:o + 2 * H_B] + yz[o + 2 * H_B:o + 3 * H_B]
        revs.append(car + y[:, :page])
        car = car + y[:, page:]
    car_ref[...] = car
    kt_all = kbuf[slot].astype(BF16)
    vt_all = vbuf[slot].astype(BF16)
    s = _dot(qbd, kt_all) + expand(jnp.concatenate(revs, axis=1) * LOG2E) + cnq_ref[...]
    m = m_ref[...]
    m_new = jnp.maximum(m, jnp.max(s, -1, keepdims=True))
    alpha = jnp.exp2(m - m_new)
    p = jnp.exp2(s - m_new)
    m_ref[...] = m_new
    l_ref[...] = alpha * l_ref[...] + jnp.sum(p, -1, keepdims=True)
    acc_ref[...] = alpha * acc_ref[...] + _dot_nt(p.astype(BF16), vt_all)

    @pl.when(pg == pl.num_programs(1) - 1)
    def _():
        acc = acc_ref[...] / l_ref[...]
        ch = lax.broadcasted_iota(jnp.int32, (ds, AB_FOX), 1) // DH
        out = jnp.zeros((ds, AB_FOX), F32)
        for h in range(H_B):
            out = out + jnp.where(ch == h, acc[h * ds:(h + 1) * ds, :], 0.0)
        o_ref[0] = out.astype(BF16)


def _fox_decode(page_table, layer, q, kn, vn, lf_row, lf_t, sel, cache_kt, cache_vt, cache_lft, gp):
    nb, ds, _ = q.shape
    npages = page_table.shape[1]
    page = cache_kt.shape[3]
    nrow = H_B * ds

    per_b = lambda b, pg, pt: (b, 0, 0)
    in_specs = [pl.BlockSpec((1,) + a.shape[1:], per_b) for a in (q, kn, vn, lf_row, lf_t)]
    in_specs += [pl.BlockSpec(sel.shape, lambda b, pg, pt: (0, 0))]
    in_specs += [pl.BlockSpec(memory_space=pl.ANY)] * 3
    grid_spec = pltpu.PrefetchScalarGridSpec(
        num_scalar_prefetch=1,
        grid=(nb, npages // gp),
        in_specs=in_specs,
        out_specs=pl.BlockSpec((1, ds, AB_FOX), per_b),
        scratch_shapes=[pltpu.VMEM((2, AB_FOX, gp * page), F32), pltpu.VMEM((2, AB_FOX, gp * page), F32),
                        pltpu.VMEM((2, gp, H_B, page), F32),
                        pltpu.SemaphoreType.DMA((2,)), pltpu.SemaphoreType.DMA((2,)), pltpu.SemaphoreType.DMA((2,)),
                        pltpu.VMEM((nrow, AB_FOX), BF16), pltpu.VMEM((nrow, 1), F32),
                        pltpu.VMEM((nrow, 1), F32), pltpu.VMEM((nrow, AB_FOX), F32),
                        pltpu.VMEM((H_B, page), F32), pltpu.VMEM((nrow, 1), F32)])
    return pl.pallas_call(
        functools.partial(_fox_decode_kernel, gp=gp, ds=ds, layer=layer),
        grid_spec=grid_spec,
        out_shape=jax.ShapeDtypeStruct((nb, ds, AB_FOX), BF16),
        compiler_params=_cparams("arbitrary", "arbitrary"),
    )(page_table, q, kn, vn, lf_row, lf_t, sel, cache_kt, cache_vt, cache_lft)


def _logf_selectors(page):
    a = jnp.arange(page)[:, None]
    b = jnp.arange(page)[None, :]
    return jnp.concatenate([a > b, jnp.ones((page, page), bool)], axis=1).astype(BF16)


def _step_after(b, g):
    last = g == pl.num_programs(1) - 1
    return jnp.where(last, b + 1, b), jnp.where(last, 0, g + 1)


def _mla_decode_kernel(pt_ref, ql_ref, qp_ref, cn_ref, pn_ref, cache_c, cache_p, o_ref,
                       cbuf, pbuf, csem, psem, m_ref, l_ref, acc_ref, *, gp, base):
    b = pl.program_id(0)
    pg = pl.program_id(1)
    t = b * pl.num_programs(1) + pg
    slot = lax.rem(t, 2)
    page = cbuf.shape[1] // gp
    ql = ql_ref[0]
    qp = qp_ref[0]
    nrow = ql.shape[0]

    def gather(bb, gg, sl):
        copies = []
        for j in range(gp):
            pid = base + pt_ref[bb, gg * gp + j]
            rows = pl.ds(j * page, page)
            copies.append(pltpu.make_async_copy(cache_c.at[pid], cbuf.at[sl, rows, :], csem.at[sl]))
            copies.append(pltpu.make_async_copy(cache_p.at[pid], pbuf.at[sl, :, rows], psem.at[sl]))
        return copies

    @pl.when(t == 0)
    def _():
        for cp in gather(0, 0, 0):
            cp.start()

    @pl.when(t + 1 < pl.num_programs(0) * pl.num_programs(1))
    def _():
        nb_, ng_ = _step_after(b, pg)
        for cp in gather(nb_, ng_, 1 - slot):
            cp.start()

    for cp in gather(b, pg, slot):
        cp.wait()

    @pl.when(pg == 0)
    def _():
        cn = cn_ref[0]
        npad = cn.shape[0]
        s = _dot_nt(ql, cn) + _dot_nt(qp, pn_ref[0])
        rq = lax.broadcasted_iota(jnp.int32, (nrow, npad), 0) // H_C
        ck = lax.broadcasted_iota(jnp.int32, (nrow, npad), 1)
        s = jnp.where(ck <= rq, s, NEG_INF)
        m = jnp.max(s, -1, keepdims=True)
        p = jnp.exp2(s - m)
        m_ref[...] = m
        l_ref[...] = jnp.sum(p, -1, keepdims=True)
        acc_ref[...] = _dot(p.astype(BF16), cn)

    c_all = cbuf[slot].astype(BF16)
    kp_all = pbuf[slot].astype(BF16)
    s = _dot_nt(ql, c_all) + _dot(qp, kp_all)
    m = m_ref[...]
    m_new = jnp.maximum(m, jnp.max(s, -1, keepdims=True))
    alpha = jnp.exp2(m - m_new)
    p = jnp.exp2(s - m_new)
    m_ref[...] = m_new
    l_ref[...] = alpha * l_ref[...] + jnp.sum(p, -1, keepdims=True)
    acc_ref[...] = alpha * acc_ref[...] + _dot(p.astype(BF16), c_all)

    @pl.when(pg == pl.num_programs(1) - 1)
    def _():
        o_ref[0] = (acc_ref[...] / l_ref[...]).astype(BF16)


def _mla_decode(page_table, base, ql, qp, cn, pn, cache_c, cache_pt, gp):
    nb, nrow, _ = ql.shape
    npages = page_table.shape[1]
    page = cache_c.shape[1]

    per_b = lambda b, pg, pt: (b, 0, 0)
    in_specs = [pl.BlockSpec((1,) + a.shape[1:], per_b) for a in (ql, qp, cn, pn)]
    in_specs += [pl.BlockSpec(memory_space=pl.ANY)] * 2
    grid_spec = pltpu.PrefetchScalarGridSpec(
        num_scalar_prefetch=1,
        grid=(nb, npages // gp),
        in_specs=in_specs,
        out_specs=pl.BlockSpec((1, nrow, KV_LORA), per_b),
        scratch_shapes=[pltpu.VMEM((2, gp * page, KV_LORA), F32), pltpu.VMEM((2, ROPE_D, gp * page), F32),
                        pltpu.SemaphoreType.DMA((2,)), pltpu.SemaphoreType.DMA((2,)),
                        pltpu.VMEM((nrow, 1), F32), pltpu.VMEM((nrow, 1), F32),
                        pltpu.VMEM((nrow, KV_LORA), F32)])
    return pl.pallas_call(
        functools.partial(_mla_decode_kernel, gp=gp, base=base),
        grid_spec=grid_spec,
        out_shape=jax.ShapeDtypeStruct((nb, nrow, KV_LORA), BF16),
        compiler_params=_cparams("arbitrary", "arbitrary"),
    )(page_table, ql, qp, cn, pn, cache_c, cache_pt)


def _ret_rope_tables(pos):
    half = DK // 2
    inv = ROPE_BASE ** (-jnp.arange(half, dtype=F32) / half)
    ang = pos.astype(F32)[:, None] * inv[None, :]
    cos, sin = jnp.cos(ang), jnp.sin(ang)
    return jnp.concatenate([cos, cos], -1), jnp.concatenate([-sin, sin], -1)


def _mla_rope_tables(pos, lane0, passthrough, scale):
    half = ROPE_D // 2
    inv = ROPE_BASE ** (-jnp.arange(half, dtype=F32) / half)
    ang = pos.astype(F32)[:, None] * inv[None, :]
    cos, sin = jnp.cos(ang) * scale, jnp.sin(ang) * scale
    n = pos.shape[0]
    z = lambda w: jnp.zeros((n, w), F32)
    lead = jnp.full((n, lane0), scale if passthrough else 0.0, F32)
    tail = z(LANES - lane0 - ROPE_D)
    c = jnp.concatenate([lead, cos, cos, tail], -1)
    s1 = jnp.concatenate([z(lane0), -sin, z(half), tail], -1)
    s2 = jnp.concatenate([z(lane0), z(half), sin, tail], -1)
    return c, s1, s2


def _retention_tables(length, reps):
    lg = jnp.log(1.0 - 2.0 ** (-5.0 - jnp.arange(H_A, dtype=F32)))
    idx = jnp.arange(length, dtype=F32)
    diff = idx[:, None] - idx[None, :]
    decay = jnp.where(diff >= 0, jnp.exp(lg[:, None, None] * jnp.maximum(diff, 0.0)[None]), 0.0)
    qd = jnp.exp((idx[:, None] + 1.0) * lg[None, :]).T
    kd = jnp.exp((length - 1.0 - idx)[:, None] * lg[None, :]).T
    dl = jnp.exp(length * lg)
    if reps > 1:
        eye = jnp.eye(reps, dtype=F32)
        decay = jnp.einsum('ab,hij->haibj', eye, decay).reshape(H_A, reps * length, reps * length)
        qd = jnp.tile(qd, (1, reps))
        kd = jnp.tile(kd, (1, reps))
    n = reps * length
    qd = jnp.broadcast_to(qd[:, :, None], (H_A, n, DV))
    kd = jnp.broadcast_to(kd[:, :, None], (H_A, n, DK))
    dl = jnp.broadcast_to(dl[:, None, None], (H_A, DK, DV))
    return decay, qd, kd, dl


def _ones_row(heads, lane):
    return (jnp.arange(heads * LANES) % LANES == lane).astype(F32)[None, :]


def _prep_ab_weights(w_in, b_f):
    main = w_in[:, :4 * AB_RET].astype(BF16)
    fox = w_in[:, 4 * AB_RET:4 * AB_RET + 3 * AB_FOX].reshape(D_MODEL, 3 * H_B, DH)
    fox = jnp.pad(fox, ((0, 0), (0, 0), (0, LANES - DH))).reshape(D_MODEL, 3 * FOX_PAD).astype(BF16)
    w_f = jnp.pad(w_in[:, 4 * AB_RET + 3 * AB_FOX:], ((0, 0), (0, LANES - H_B))).astype(BF16)
    w_kvt = w_in[:, 4 * AB_RET + AB_FOX:4 * AB_RET + 3 * AB_FOX].T.astype(BF16)
    bf_row = jnp.pad(b_f, (0, LANES - H_B))[None, :]
    return main, fox, w_f, w_kvt, bf_row, _ones_row(H_B, DH)


def _prep_c_weights(w_in, q_g, w_qb, kv_g, w_kvb):
    win = w_in[:, :Q_LORA + KV_LORA].astype(BF16)
    wpe = jnp.pad(w_in[:, Q_LORA + KV_LORA:], ((0, 0), (0, LANES - ROPE_D))).astype(BF16)
    wqb = w_qb.reshape(Q_LORA, H_C, NOPE + ROPE_D)
    wqb = jnp.pad(wqb, ((0, 0), (0, 0), (0, LANES - NOPE - ROPE_D))).reshape(Q_LORA, H_C * LANES).astype(BF16)
    wkv = w_kvb.reshape(KV_LORA, H_C, NOPE + V_DIM)
    w_uk, w_uv = wkv[..., :NOPE], wkv[..., NOPE:]
    wk = jnp.pad(w_uk, ((0, 0), (0, 0), (0, LANES - NOPE))).reshape(KV_LORA, H_C * LANES).astype(BF16)
    wv = jnp.pad(w_uv, ((0, 0), (0, 0), (0, LANES - V_DIM))).reshape(KV_LORA, H_C * LANES).astype(BF16)
    r = jnp.arange(LANES)[:, None]
    c = jnp.arange(H_C * LANES)[None, :]
    place = ((r < ROPE_D) & (c % LANES == NOPE + r)).astype(BF16)
    w_abs = jnp.pad(w_uk.transpose(1, 2, 0), ((0, 0), (0, LANES - NOPE), (0, 0))).astype(BF16)
    w_val = w_uv.transpose(1, 0, 2).astype(BF16)
    return (win, wpe, q_g[None, :], kv_g[None, :], wqb, wk, wv, place, _ones_row(H_C, V_DIM)), w_abs, w_val


TM = 512
TM_MLP = 1024
TF_MLP = 1024
TQ, TK = 512, 256
FLASH_UNROLL = 4
TC = 512
RET_SUB = 4
RET_BB = 16
FOX_HG = 4
MLA_HG = 4
PAGES_PER_STEP = 16
NEW_PAD = 16


def kernel(x_prompt, x_sample, state_ret, cache_fox_k, cache_fox_v, cache_fox_logf, cache_mla_ckv,
           cache_mla_kpe, page_table, w_in_ab, ret_gn_g, fox_b_f, w_out_ab, w_in_c, mla_q_norm_g,
           mla_w_qb, mla_kv_norm_g, mla_w_kvb, w_out_c, ln_mix_g, ln_mix_b, ln_mlp_g, ln_mlp_b,
           mlp_w1, mlp_w2):
    batch, seq, _ = x_prompt.shape
    nb, ds, _ = x_sample.shape
    n_pool, page = cache_fox_k.shape[1], cache_fox_k.shape[2]
    past_len = page_table.shape[1] * page
    mp, ms = batch * seq, nb * ds
    pos_p = jnp.arange(seq)
    pos_s = past_len + jnp.arange(ds)
    pos_s_rows = jnp.tile(pos_s, nb)

    hp = x_prompt.reshape(mp, D_MODEL)
    hs = x_sample.reshape(ms, D_MODEL)
    tms = min(TM, ms)
    outs = {k: [] for k in ("ret_p", "ret_s", "fk_p", "fv_p", "fl_p", "fk_s", "fv_s", "fl_s",
                            "ck_p", "kp_p", "ck_s", "kp_s")}

    for l in range(DEPTH):
        i = l // 2
        row = lambda a: a[None, :]
        if l % 2 == 0:
            wts = _prep_ab_weights(w_in_ab[i], fox_b_f[i])
            w_out = w_out_ab[i].astype(BF16)
            gn = ret_gn_g[i][None, :]
            cos, sin = _ret_rope_tables(pos_p)
            rq, rk, rv, rg, fq, fk16, fv16, lf, fkt, fvt = _proj_ab(hp, *wts[:4], cos, sin, *wts[4:], TM, seq)
            y, s_fin = _ret_prompt(rq, rk, rv, rg, _retention_tables(RET_CHUNK, 1), gn, batch, seq, RET_SUB)
            fqb, fkb = _fox_bias(lf, fq, fk16, batch, seq, TC)
            fo = _flash(fqb, fkb, fv16, batch, seq, H_B, DH, FOX_HG, TQ, TK)
            mix_parts_p = [y, fo]
            unpad = lambda a, n: a.reshape(n, H_B, LANES)[:, :, :DH]
            seq_major = lambda a: a.reshape(batch, H_B, DH, seq).transpose(0, 3, 1, 2)
            outs["ret_p"].append(s_fin)
            outs["fk_p"].append(seq_major(fkt))
            outs["fv_p"].append(seq_major(fvt))
            outs["fl_p"].append(lf.reshape(batch, seq, H_B))
            cos, sin = _ret_rope_tables(pos_s_rows)
            rq, rk, rv, rg, fq, fk16, fv16, lf, fk, fv = _proj_ab(hs, *wts[:4], cos, sin, *wts[4:], tms, None)
            y, s_new = _ret_sample(rq, rk, rv, rg, state_ret[i], _retention_tables(ds, RET_BB), gn, RET_BB, ds)
            padn = lambda a: jnp.pad(unpad(a, ms).reshape(nb, ds, AB_FOX), ((0, 0), (0, NEW_PAD - ds), (0, 0)))
            lf_hq = lf.reshape(nb, ds, H_B).transpose(0, 2, 1)
            fo = _fox_decode(page_table, i, unpad(fq, ms).reshape(nb, ds, AB_FOX), padn(fk16), padn(fv16),
                             lf_hq.reshape(nb, 1, H_B * ds),
                             jnp.pad(lf_hq, ((0, 0), (0, 0), (0, NEW_PAD - ds))),
                             _logf_selectors(page),
                             cache_fox_k.transpose(0, 1, 3, 4, 2).reshape(-1, n_pool, AB_FOX, page),
                             cache_fox_v.transpose(0, 1, 3, 4, 2).reshape(-1, n_pool, AB_FOX, page),
                             cache_fox_logf.transpose(0, 1, 3, 2), PAGES_PER_STEP)
            mix_parts_s = [y, fo.reshape(ms, AB_FOX)]
            outs["ret_s"].append(s_new)
            outs["fk_s"].append(unpad(fk, ms).reshape(nb, ds, H_B, DH))
            outs["fv_s"].append(unpad(fv, ms).reshape(nb, ds, H_B, DH))
            outs["fl_s"].append(lf.reshape(nb, ds, H_B))
        else:
            wts, w_abs, w_val = _prep_c_weights(w_in_c[i], mla_q_norm_g[i], mla_w_qb[i],
                                                mla_kv_norm_g[i], mla_w_kvb[i])
            w_out = w_out_c[i].astype(BF16)
            qscale = (NOPE + ROPE_D) ** -0.5 * LOG2E
            tabs = _mla_rope_tables(pos_p, NOPE, True, qscale) + _mla_rope_tables(pos_p, 0, False, 1.0)
            q, k, v, ckv, kpe = _proj_c(hp, wts, tabs, TM)
            o = _flash(q, k, v, batch, seq, H_C, V_DIM, MLA_HG, TQ, TK)
            mix_parts_p = [o]
            outs["ck_p"].append(ckv.reshape(batch, seq, KV_LORA))
            outs["kp_p"].append(kpe.reshape(batch, seq, ROPE_D))
            tabs = _mla_rope_tables(pos_s_rows, NOPE, True, qscale) + _mla_rope_tables(pos_s_rows, 0, False, 1.0)
            q, _, _, ckv, kpe = _proj_c(hs, wts, tabs, tms)
            q_lat = _head_mm(q, w_abs, 1).reshape(nb, ds * H_C, KV_LORA)
            q_pe = q.reshape(nb, ds * H_C, LANES)[:, :, NOPE:NOPE + ROPE_D]
            padn = lambda a: jnp.pad(a.reshape(nb, ds, -1), ((0, 0), (0, NEW_PAD - ds), (0, 0))).astype(BF16)
            o_lat = _mla_decode(page_table, i * n_pool, q_lat, q_pe, padn(ckv), padn(kpe),
                                cache_mla_ckv.reshape(-1, page, KV_LORA),
                                cache_mla_kpe.transpose(0, 1, 3, 2).reshape(-1, ROPE_D, page), PAGES_PER_STEP)
            o = _head_mm(o_lat.reshape(ms, H_C * KV_LORA), w_val, 2)
            mix_parts_s = [o]
            outs["ck_s"].append(ckv.reshape(nb, ds, KV_LORA))
            outs["kp_s"].append(kpe.reshape(nb, ds, ROPE_D))

        w1 = mlp_w1[l].astype(BF16)
        w2 = mlp_w2[l].astype(BF16)
        hp = _outproj_ln(mix_parts_p, w_out, hp, row(ln_mix_g[l]), row(ln_mix_b[l]), TM)
        hs = _outproj_ln(mix_parts_s, w_out, hs, row(ln_mix_g[l]), row(ln_mix_b[l]), tms)
        hp = _mlp_ln(hp, w1, w2, row(ln_mlp_g[l]), row(ln_mlp_b[l]), TM_MLP, TF_MLP)
        hs = _mlp_ln(hs, w1, w2, row(ln_mlp_g[l]), row(ln_mlp_b[l]), min(TM_MLP, ms), TF_MLP)

    st = lambda k: jnp.stack(outs[k])
    return (hp.reshape(batch, seq, D_MODEL), hs.reshape(nb, ds, D_MODEL),
            st("ret_p"), st("ret_s"), st("fk_p"), st("fv_p"), st("fl_p"),
            st("fk_s"), st("fv_s"), st("fl_s"), st("ck_p"), st("kp_p"), st("ck_s"), st("kp_s"))
```

```python
import functools
import math

import jax
import jax.numpy as jnp
from jax import lax
from jax.experimental import pallas as pl
from jax.experimental.pallas import tpu as pltpu

F32 = jnp.float32
BF16 = jnp.bfloat16

D_MODEL = 1024
DEPTH = 2
H_A, DK, DV = 4, 128, 128
RET_CHUNK = 128
H_B, DH = 8, 64
H_C, NOPE, ROPE_D, V_DIM = 16, 64, 32, 64
Q_LORA, KV_LORA = 768, 256
D_FF = 4 * D_MODEL
ROPE_BASE = 10000.0
LN_EPS = 1e-5
RMS_EPS = 1e-6
GN_EPS = 1e-6
ALPHA = (2 * DEPTH) ** 0.25
AB_RET = H_A * DK
AB_FOX = H_B * DH
FOX_PAD = H_B * 128
LANES = 128
LOG2E = math.log2(math.e)
VMEM_LIMIT = 56 * 1024 * 1024
NEG_INF = float("-inf")


def _cparams(*sem):
    return pltpu.CompilerParams(dimension_semantics=sem, vmem_limit_bytes=VMEM_LIMIT)


def _dot(a, b):
    return jnp.dot(a, b, preferred_element_type=F32)


def _dot_nt(a, b):
    return lax.dot_general(a, b, (((1,), (1,)), ((), ())), preferred_element_type=F32)


def _layer_norm(z, g, b):
    mu = jnp.mean(z, -1, keepdims=True)
    var = jnp.mean(jnp.square(z - mu), -1, keepdims=True)
    return (z - mu) * lax.rsqrt(var + LN_EPS) * g + b


def _rms_norm(z, g):
    return z * lax.rsqrt(jnp.mean(z * z, -1, keepdims=True) + RMS_EPS) * g


def _log_sigmoid(x):
    return jnp.minimum(x, 0.0) - jnp.log1p(jnp.exp(-jnp.abs(x)))


def _full(shape):
    nd = len(shape)
    return pl.BlockSpec(shape, lambda *_: (0,) * nd)


def _proj_ab_kernel(x_ref, w_ref, wfox_ref, wf_ref, wkvt_ref, cos_ref, sin_ref, bf_ref, ones_ref,
                    rq_ref, rk_ref, rv_ref, rg_ref, fq_ref, fk16_ref, fv16_ref, lf_ref, fk_ref, fv_ref,
                    *, seq_minor):
    xb = x_ref[...].astype(BF16)
    cos = cos_ref[...]
    sin = sin_ref[...]

    def slab(j):
        return _dot(xb, w_ref[:, j * AB_RET:(j + 1) * AB_RET])

    def fox_slab(j):
        return _dot(xb, wfox_ref[:, j * FOX_PAD:(j + 1) * FOX_PAD])

    def rope(h):
        parts = []
        for g in range(H_A):
            seg = h[:, g * DK:(g + 1) * DK]
            parts.append(seg * cos + pltpu.roll(seg, DK // 2, 1) * sin)
        return jnp.concatenate(parts, axis=1)

    rq_ref[...] = rope(slab(0))
    rk_ref[...] = rope(slab(1)) * (DK ** -0.5)
    rv_ref[...] = slab(2)
    rg_ref[...] = slab(3)
    fq_ref[...] = (fox_slab(0) * (DH ** -0.5 * LOG2E)).astype(BF16)
    fk = fox_slab(1)
    fv = fox_slab(2)
    fk16_ref[...] = fk.astype(BF16)
    fv16_ref[...] = (fv + ones_ref[...]).astype(BF16)
    ff = _dot(xb, wf_ref[...]) + bf_ref[...]
    lf_ref[...] = _log_sigmoid(ff)[:, :H_B]
    if seq_minor:
        kvt = _dot_nt(wkvt_ref[...], xb)
        fk_ref[0] = kvt[:AB_FOX]
        fv_ref[0] = kvt[AB_FOX:]
    else:
        fk_ref[...] = fk
        fv_ref[...] = fv


def _proj_ab(x, w_main, w_fox, w_f, w_kvt, cos, sin, bf_row, ones_row, tm, seq):
    m = x.shape[0]
    ntab = cos.shape[0] // tm
    row = lambda i: (i, 0)
    tab = lambda i: (i % ntab, 0)
    ret = lambda dt: jax.ShapeDtypeStruct((m, AB_RET), dt)
    fox = lambda dt: jax.ShapeDtypeStruct((m, FOX_PAD), dt)
    rspec = pl.BlockSpec((tm, AB_RET), row)
    fspec = pl.BlockSpec((tm, FOX_PAD), row)
    if seq is None:
        kv_shape, kv_spec = fox(F32), fspec
    else:
        per = seq // tm
        kv_shape = jax.ShapeDtypeStruct((m // seq, AB_FOX, seq), F32)
        kv_spec = pl.BlockSpec((1, AB_FOX, tm), lambda i: (i // per, 0, i % per))
    outs = (ret(F32), ret(F32), ret(F32), ret(F32), fox(BF16), fox(BF16), fox(BF16),
            jax.ShapeDtypeStruct((m, H_B), F32), kv_shape, kv_shape)
    return pl.pallas_call(
        functools.partial(_proj_ab_kernel, seq_minor=seq is not None),
        grid=(m // tm,),
        in_specs=[pl.BlockSpec((tm, D_MODEL), row), _full(w_main.shape), _full(w_fox.shape), _full(w_f.shape),
                  _full(w_kvt.shape), pl.BlockSpec((tm, LANES), tab), pl.BlockSpec((tm, LANES), tab),
                  _full(bf_row.shape), _full(ones_row.shape)],
        out_specs=[rspec] * 4 + [fspec] * 3 + [pl.BlockSpec((tm, H_B), row), kv_spec, kv_spec],
        out_shape=outs,
        compiler_params=_cparams("parallel"),
    )(x, w_main, w_fox, w_f, w_kvt, cos, sin, bf_row, ones_row)


def _split3(x):
    x1 = x.astype(BF16).astype(F32)
    x2 = (x - x1).astype(BF16).astype(F32)
    x3 = (x - x1 - x2).astype(BF16).astype(F32)
    return x1, x2, x3


def _fox_bias_kernel(lf_ref, q_ref, k_ref, qo_ref, ko_ref, car_ref):
    @pl.when(pl.program_id(1) == 0)
    def _():
        car_ref[...] = jnp.zeros_like(car_ref)

    tc = lf_ref.shape[0]
    r = lax.broadcasted_iota(jnp.int32, (tc, tc), 0)
    c = lax.broadcasted_iota(jnp.int32, (tc, tc), 1)
    lower = jnp.where(c <= r, 1.0, 0.0).astype(BF16)
    cs = car_ref[...]
    for part in _split3(lf_ref[...]):
        cs = cs + _dot(lower, part.astype(BF16))
    car_ref[...] = cs[tc - 1:tc, :]
    cs = cs * LOG2E
    lane = lax.broadcasted_iota(jnp.int32, (tc, LANES), 1)
    ones_q = jnp.where((lane >= DH + 3) & (lane < DH + 6), 1.0, 0.0)
    ones_k = jnp.where((lane >= DH) & (lane < DH + 3), 1.0, 0.0)
    for h in range(H_B):
        lanes = slice(h * LANES, (h + 1) * LANES)
        parts = _split3(cs[:, h:h + 1])
        q_add = ones_q
        k_add = ones_k
        for t, part in enumerate(parts):
            q_add = q_add + jnp.where(lane == DH + t, part, 0.0)
            k_add = k_add - jnp.where(lane == DH + 3 + t, part, 0.0)
        qo_ref[:, lanes] = (q_ref[:, lanes].astype(F32) + q_add).astype(BF16)
        ko_ref[:, lanes] = (k_ref[:, lanes].astype(F32) + k_add).astype(BF16)


def _fox_bias(lf, q, k, batch, seq, tc):
    nc = seq // tc
    row = lambda b, j: (b * nc + j, 0)
    wide = pl.BlockSpec((tc, FOX_PAD), row)
    return pl.pallas_call(
        _fox_bias_kernel,
        grid=(batch, nc),
        in_specs=[pl.BlockSpec((tc, H_B), row), wide, wide],
        out_specs=[wide, wide],
        out_shape=(jax.ShapeDtypeStruct(q.shape, BF16), jax.ShapeDtypeStruct(k.shape, BF16)),
        scratch_shapes=[pltpu.VMEM((1, H_B), F32)],
        compiler_params=_cparams("arbitrary", "arbitrary"),
    )(lf, q, k)


def _gn_gate(o, gate, gn):
    mu = jnp.mean(o, -1, keepdims=True)
    var = jnp.mean(jnp.square(o - mu), -1, keepdims=True)
    y = (o - mu) * lax.rsqrt(var + GN_EPS) * gn
    return (y * (gate / (1.0 + jnp.exp(-gate)))).astype(BF16)


def _ret_prompt_kernel(q_ref, k_ref, v_ref, g_ref, dec_ref, qd_ref, kd_ref, dl_ref, gn_ref,
                       y_ref, sfin_ref, st_ref, *, nsub):
    @pl.when(pl.program_id(1) == 0)
    def _():
        st_ref[...] = jnp.zeros_like(st_ref)

    for s in range(nsub):
        rows = slice(s * RET_CHUNK, (s + 1) * RET_CHUNK)
        for h in range(H_A):
            cols = slice(h * DK, (h + 1) * DK)
            q16 = q_ref[rows, cols].astype(BF16)
            k = k_ref[rows, cols]
            v16 = v_ref[rows, cols].astype(BF16)
            st = st_ref[h]
            sc = _dot_nt(q16, k.astype(BF16)) * dec_ref[h]
            o = _dot(sc.astype(BF16), v16) + qd_ref[h] * _dot(q16, st.astype(BF16))
            kdt = (k * kd_ref[h]).T.astype(BF16)
            st_ref[h] = dl_ref[h] * st + _dot(kdt, v16)
            y_ref[rows, cols] = _gn_gate(o, g_ref[rows, cols], gn_ref[:, cols])
    sfin_ref[0] = st_ref[...]


def _ret_prompt(rq, rk, rv, rg, tabs, gn, batch, seq, nsub):
    tr = nsub * RET_CHUNK
    nc = seq // tr
    row = lambda b, c: (b * nc + c, 0)
    blk = pl.BlockSpec((tr, AB_RET), row)
    dec, qd, kd, dl = tabs
    return pl.pallas_call(
        functools.partial(_ret_prompt_kernel, nsub=nsub),
        grid=(batch, nc),
        in_specs=[blk, blk, blk, blk, _full(dec.shape), _full(qd.shape), _full(kd.shape),
                  _full(dl.shape), _full(gn.shape)],
        out_specs=[blk, pl.BlockSpec((1, H_A, DK, DV), lambda b, c: (b, 0, 0, 0))],
        out_shape=(jax.ShapeDtypeStruct(rq.shape, BF16),
                   jax.ShapeDtypeStruct((batch, H_A, DK, DV), F32)),
        scratch_shapes=[pltpu.VMEM((H_A, DK, DV), F32)],
        compiler_params=_cparams("arbitrary", "arbitrary"),
    )(rq, rk, rv, rg, dec, qd, kd, dl, gn)


def _ret_sample_kernel(q_ref, k_ref, v_ref, g_ref, st_ref, dec_ref, qd_ref, kd_ref, dl_ref, gn_ref,
                       y_ref, snew_ref, *, bb, ds):
    n = bb * ds
    rb = lax.broadcasted_iota(jnp.int32, (n, DV), 0) // ds
    for h in range(H_A):
        cols = slice(h * DK, (h + 1) * DK)
        q = q_ref[:, cols]
        k = k_ref[:, cols]
        v = v_ref[:, cols]
        v16 = v.astype(BF16)
        sc = _dot_nt(q.astype(BF16), k.astype(BF16)) * dec_ref[h]
        o_intra = _dot(sc.astype(BF16), v16)
        kdt = (k * kd_ref[h]).T.astype(BF16)
        dl = dl_ref[h]

        def per_seq(b, o_cross, h=h, q=q, v=v, kdt=kdt, dl=dl):
            st = st_ref[b, h]
            qb = jnp.where(rb == b, q, 0.0).astype(BF16)
            vb = jnp.where(rb == b, v, 0.0).astype(BF16)
            snew_ref[b, h] = dl * st + _dot(kdt, vb)
            return o_cross + _dot(qb, st.astype(BF16))

        o_cross = lax.fori_loop(0, bb, per_seq, jnp.zeros((n, DV), F32), unroll=4)
        o = o_intra + qd_ref[h] * o_cross
        y_ref[:, cols] = _gn_gate(o, g_ref[:, cols], gn_ref[:, cols])


def _ret_sample(rq, rk, rv, rg, state, tabs, gn, bb, ds):
    nb = state.shape[0]
    n = bb * ds
    row = lambda i: (i, 0)
    blk = pl.BlockSpec((n, AB_RET), row)
    sblk = pl.BlockSpec((bb, H_A, DK, DV), lambda i: (i, 0, 0, 0))
    dec, qd, kd, dl = tabs
    return pl.pallas_call(
        functools.partial(_ret_sample_kernel, bb=bb, ds=ds),
        grid=(nb // bb,),
        in_specs=[blk, blk, blk, blk, sblk, _full(dec.shape), _full(qd.shape), _full(kd.shape),
                  _full(dl.shape), _full(gn.shape)],
        out_specs=[blk, sblk],
        out_shape=(jax.ShapeDtypeStruct(rq.shape, BF16), jax.ShapeDtypeStruct(state.shape, F32)),
        compiler_params=_cparams("parallel"),
    )(rq, rk, rv, rg, state, dec, qd, kd, dl, gn)


def _flash_kernel(q_ref, k_ref, v_ref, o_ref, *, hg, dout, tq, tk):
    i = pl.program_id(2)
    ratio = tq // tk
    ahead = lax.broadcasted_iota(jnp.int32, (tq, tk), 1) - lax.broadcasted_iota(jnp.int32, (tq, tk), 0)

    def step(j, carry):
        ks = pl.ds(pl.multiple_of(j * tk, tk), tk)
        visible = ahead <= i * tq - j * tk
        out = []
        for h in range(hg):
            lanes = slice(h * LANES, (h + 1) * LANES)
            m, acc = carry[h]
            s = jnp.where(visible, _dot_nt(q_ref[:, lanes], k_ref[ks, lanes]), NEG_INF)
            m_new = jnp.maximum(m, jnp.max(s, -1, keepdims=True))
            p = jnp.exp2((s - m_new).astype(BF16))
            acc = jnp.exp2(m - m_new) * acc + _dot(p, v_ref[ks, lanes])
            out.append((m_new, acc))
        return tuple(out)

    def trip(jj, carry):
        for u in range(FLASH_UNROLL):
            carry = step(FLASH_UNROLL * jj + u, carry)
        return carry

    def tail(jj, carry):
        for u in range(ratio):
            carry = step(ratio * jj + u, carry)
        return carry

    init = (jnp.full((tq, 1), NEG_INF, F32), jnp.zeros((tq, LANES), F32))
    nsteps = (i + 1) * ratio
    ntrip = nsteps // FLASH_UNROLL
    carry = lax.fori_loop(0, ntrip, trip, (init,) * hg)
    carry = lax.fori_loop(ntrip * (FLASH_UNROLL // ratio), i + 1, tail, carry)
    for h in range(hg):
        acc = carry[h][1]
        o_ref[:, h * dout:(h + 1) * dout] = (acc[:, :dout] / acc[:, dout:dout + 1]).astype(BF16)


def _flash(q, k, v, batch, seq, heads, dout, hg, tq, tk):
    assert seq % tq == 0 and tq % tk == 0 and FLASH_UNROLL % (tq // tk) == 0 and heads % hg == 0
    nq = seq // tq
    ngrp = heads // hg
    return pl.pallas_call(
        functools.partial(_flash_kernel, hg=hg, dout=dout, tq=tq, tk=tk),
        grid=(batch, ngrp, nq),
        in_specs=[pl.BlockSpec((tq, hg * LANES), lambda b, g, i: (b * nq + i, g)),
                  pl.BlockSpec((seq, hg * LANES), lambda b, g, i: (b, g)),
                  pl.BlockSpec((seq, hg * LANES), lambda b, g, i: (b, g))],
        out_specs=pl.BlockSpec((tq, hg * dout), lambda b, g, i: (b * nq + i, g)),
        out_shape=jax.ShapeDtypeStruct((batch * seq, heads * dout), BF16),
        compiler_params=_cparams("parallel", "parallel", "arbitrary"),
    )(q, k, v)


def _outproj_ln_kernel(*refs, n_in):
    a_refs = refs[:n_in]
    w_ref, x_ref, g_ref, b_ref, o_ref = refs[n_in:]
    tm = x_ref.shape[0]
    sub = min(tm, OUT_SUB)
    for r in range(tm // sub):
        rows = slice(r * sub, (r + 1) * sub)
        acc = None
        off = 0
        for a_ref in a_refs:
            ka = a_ref.shape[1]
            d = _dot(a_ref[rows, :], w_ref[off:off + ka, :])
            acc = d if acc is None else acc + d
            off += ka
        o_ref[rows, :] = _layer_norm(ALPHA * x_ref[rows, :] + acc, g_ref[...], b_ref[...])


def _outproj_ln(parts, w, x, g, b, tm):
    m = x.shape[0]
    row = lambda i: (i, 0)
    in_specs = [pl.BlockSpec((tm, p.shape[1]), row) for p in parts]
    in_specs += [_full(w.shape), pl.BlockSpec((tm, D_MODEL), row), _full(g.shape), _full(b.shape)]
    return pl.pallas_call(
        functools.partial(_outproj_ln_kernel, n_in=len(parts)),
        grid=(m // tm,),
        in_specs=in_specs,
        out_specs=pl.BlockSpec((tm, D_MODEL), row),
        out_shape=jax.ShapeDtypeStruct((m, D_MODEL), F32),
        compiler_params=_cparams("parallel"),
    )(*parts, w, x, g, b)


def _mlp_ln_kernel(x_ref, w1_ref, w2_ref, g_ref, b_ref, o_ref, xb_ref, acc_ref):
    f = pl.program_id(1)

    @pl.when(f == 0)
    def _():
        xb_ref[...] = x_ref[...].astype(BF16)
        acc_ref[...] = jnp.zeros_like(acc_ref)

    h = jnp.maximum(_dot(xb_ref[...], w1_ref[...]), 0.0)
    acc_ref[...] += _dot((h * h).astype(BF16), w2_ref[...])

    @pl.when(f == pl.num_programs(1) - 1)
    def _():
        o_ref[...] = _layer_norm(ALPHA * x_ref[...] + acc_ref[...], g_ref[...], b_ref[...])


def _mlp_ln(x, w1, w2, g, b, tm, tf):
    m = x.shape[0]
    return pl.pallas_call(
        _mlp_ln_kernel,
        grid=(m // tm, D_FF // tf),
        in_specs=[pl.BlockSpec((tm, D_MODEL), lambda i, f: (i, 0)),
                  pl.BlockSpec((D_MODEL, tf), lambda i, f: (0, f)),
                  pl.BlockSpec((tf, D_MODEL), lambda i, f: (f, 0)),
                  _full(g.shape), _full(b.shape)],
        out_specs=pl.BlockSpec((tm, D_MODEL), lambda i, f: (i, 0)),
        out_shape=jax.ShapeDtypeStruct((m, D_MODEL), F32),
        scratch_shapes=[pltpu.VMEM((tm, D_MODEL), BF16), pltpu.VMEM((tm, D_MODEL), F32)],
        compiler_params=_cparams("parallel", "arbitrary"),
    )(x, w1, w2, g, b)


def _rot_lanes(x, c, s1, s2):
    return x * c + pltpu.roll(x, LANES - ROPE_D // 2, 1) * s1 + pltpu.roll(x, ROPE_D // 2, 1) * s2


def _proj_c_kernel(x_ref, win_ref, wpe_ref, qg_ref, kvg_ref, wqb_ref, wk_ref, wv_ref, ones_ref,
                   qc_ref, qs1_ref, qs2_ref, kc_ref, ks1_ref, ks2_ref,
                   q_ref, k_ref, v_ref, ckv_ref, kpe_ref):
    xb = x_ref[...].astype(BF16)
    qa = _dot(xb, win_ref[:, :Q_LORA])
    kva = _dot(xb, win_ref[:, Q_LORA:Q_LORA + KV_LORA])
    kpe = _dot(xb, wpe_ref[...])
    qn = _rms_norm(qa, qg_ref[...]).astype(BF16)
    qc, qs1, qs2 = qc_ref[...], qs1_ref[...], qs2_ref[...]
    for pair in range(H_C // 2):
        both = _dot(qn, wqb_ref[:, pair * 2 * LANES:(pair + 1) * 2 * LANES])
        for e in range(2):
            h = 2 * pair + e
            seg = both[:, e * LANES:(e + 1) * LANES]
            q_ref[:, h * LANES:(h + 1) * LANES] = _rot_lanes(seg, qc, qs1, qs2).astype(BF16)
    ckv = _rms_norm(kva, kvg_ref[...])
    ckv_ref[...] = ckv
    ckv16 = ckv.astype(BF16)
    kpe_rot = _rot_lanes(kpe, kc_ref[...], ks1_ref[...], ks2_ref[...])
    kpe_ref[...] = kpe_rot[:, :ROPE_D]
    kshift = pltpu.roll(kpe_rot, NOPE, 1)
    k_ref[...] = (_dot(ckv16, wk_ref[...]) + jnp.concatenate([kshift] * H_C, axis=1)).astype(BF16)
    v_ref[...] = (_dot(ckv16, wv_ref[...]) + ones_ref[...]).astype(BF16)


def _proj_c(x, wts, tabs, tm):
    m = x.shape[0]
    ntab = tabs[0].shape[0] // tm
    row = lambda i: (i, 0)
    tab = pl.BlockSpec((tm, LANES), lambda i: (i % ntab, 0))
    return pl.pallas_call(
        _proj_c_kernel,
        grid=(m // tm,),
        in_specs=[pl.BlockSpec((tm, D_MODEL), row)] + [_full(w.shape) for w in wts] + [tab] * 6,
        out_specs=[pl.BlockSpec((tm, H_C * LANES), row)] * 3
        + [pl.BlockSpec((tm, KV_LORA), row), pl.BlockSpec((tm, ROPE_D), row)],
        out_shape=(jax.ShapeDtypeStruct((m, H_C * LANES), BF16),) * 3
        + (jax.ShapeDtypeStruct((m, KV_LORA), F32), jax.ShapeDtypeStruct((m, ROPE_D), F32)),
        compiler_params=_cparams("parallel"),
    )(x, *wts, *tabs)


def _head_mm_kernel(x_ref, w_ref, o_ref, *, hp, din, dout):
    for j in range(hp):
        o_ref[:, j * dout:(j + 1) * dout] = _dot(x_ref[:, j * din:(j + 1) * din], w_ref[j]).astype(BF16)


def _head_mm(x, w, hp):
    m = x.shape[0]
    heads, din, dout = w.shape
    return pl.pallas_call(
        functools.partial(_head_mm_kernel, hp=hp, din=din, dout=dout),
        grid=(heads // hp,),
        in_specs=[pl.BlockSpec((m, hp * din), lambda h: (0, h)),
                  pl.BlockSpec((hp, din, dout), lambda h: (h, 0, 0))],
        out_specs=pl.BlockSpec((m, hp * dout), lambda h: (0, h)),
        out_shape=jax.ShapeDtypeStruct((m, heads * dout), BF16),
        compiler_params=_cparams("parallel"),
    )(x, w)


def _fox_decode_kernel(pt_ref, q_ref, kn_ref, vn_ref, lfr_ref, lft_ref, sel_ref, cache_k, cache_v, cache_lf,
                       o_ref, kbuf, vbuf, lbuf, ksem, vsem, lsem,
                       qbd_ref, m_ref, l_ref, acc_ref, car_ref, cnq_ref, *, gp, ds, layer):
    b = pl.program_id(0)
    pg = pl.program_id(1)
    t = b * pl.num_programs(1) + pg
    slot = lax.rem(t, 2)
    nrow = H_B * ds
    page = lbuf.shape[3]
    npages = gp * pl.num_programs(1)

    def gather(bb, gg, sl):
        copies = []
        for j in range(gp):
            pid = pt_ref[bb, npages - 1 - (gg * gp + j)]
            cols = pl.ds(j * page, page)
            copies.append(pltpu.make_async_copy(cache_k.at[layer, pid], kbuf.at[sl, :, cols], ksem.at[sl]))
            copies.append(pltpu.make_async_copy(cache_v.at[layer, pid], vbuf.at[sl, :, cols], vsem.at[sl]))
            copies.append(pltpu.make_async_copy(cache_lf.at[layer, pid], lbuf.at[sl, j], lsem.at[sl]))
        return copies

    @pl.when(t == 0)
    def _():
        for cp in gather(0, 0, 0):
            cp.start()

    @pl.when(t + 1 < pl.num_programs(0) * pl.num_programs(1))
    def _():
        nb_, ng_ = _step_after(b, pg)
        for cp in gather(nb_, ng_, 1 - slot):
            cp.start()

    for cp in gather(b, pg, slot):
        cp.wait()

    def expand(x):
        return jnp.concatenate([jnp.broadcast_to(x[h:h + 1, :], (ds, x.shape[1])) for h in range(H_B)], axis=0)

    @pl.when(pg == 0)
    def _():
        q = q_ref[0].astype(F32)
        rh = lax.broadcasted_iota(jnp.int32, (nrow, AB_FOX), 0) // ds
        ch = lax.broadcasted_iota(jnp.int32, (nrow, AB_FOX), 1) // DH
        qbd = jnp.where(rh == ch, jnp.concatenate([q] * H_B, axis=0), 0.0).astype(BF16)
        qbd_ref[...] = qbd
        r = lax.broadcasted_iota(jnp.int32, (nrow, nrow), 0)
        c = lax.broadcasted_iota(jnp.int32, (nrow, nrow), 1)
        m_row = jnp.where((r // ds == c // ds) & (c <= r), 1.0, 0.0)
        cn_col = jnp.sum(m_row * lfr_ref[0], axis=1, keepdims=True)
        cnq_ref[...] = cn_col * LOG2E
        lft = lft_ref[0]
        npad = lft.shape[1]
        kk = lax.broadcasted_iota(jnp.int32, (1, npad), 1)
        cnt = jnp.zeros((H_B, npad), F32)
        for k2 in range(ds):
            cnt = cnt + lft[:, k2:k2 + 1] * jnp.where(kk >= k2, 1.0, 0.0)
        s = _dot_nt(qbd, kn_ref[0]) + (cn_col - expand(cnt)) * LOG2E
        rq = lax.broadcasted_iota(jnp.int32, (nrow, npad), 0) % ds
        ck = lax.broadcasted_iota(jnp.int32, (nrow, npad), 1)
        s = jnp.where(ck <= rq, s, NEG_INF)
        m = jnp.max(s, -1, keepdims=True)
        p = jnp.exp2(s - m)
        m_ref[...] = m
        l_ref[...] = jnp.sum(p, -1, keepdims=True)
        acc_ref[...] = _dot(p.astype(BF16), vn_ref[0])
        car_ref[...] = jnp.zeros_like(car_ref)

    qbd = qbd_ref[...]
    xs = []
    for j in range(gp):
        xs += list(_split3(lbuf[slot, j]))
    yz = _dot(jnp.concatenate(xs, axis=0).astype(BF16), sel_ref[...])
    car = car_ref[...]
    revs = []
    for j in range(gp):
        o = j * 3 * H_B
        y = yz[o:o + H_B] + yz[o + H_B:o + 2 * H_B] + yz[o + 2 * H_B:o + 3 * H_B]
        revs.append(car + y[:, :page])
        car = car + y[:, page:]
    car_ref[...] = car
    kt_all = kbuf[slot].astype(BF16)
    vt_all = vbuf[slot].astype(BF16)
    s = _dot(qbd, kt_all) + expand(jnp.concatenate(revs, axis=1) * LOG2E) + cnq_ref[...]
    m = m_ref[...]
    m_new = jnp.maximum(m, jnp.max(s, -1, keepdims=True))
    alpha = jnp.exp2(m - m_new)
    p = jnp.exp2(s - m_new)
    m_ref[...] = m_new
    l_ref[...] = alpha * l_ref[...] + jnp.sum(p, -1, keepdims=True)
    acc_ref[...] = alpha * acc_ref[...] + _dot_nt(p.astype(BF16), vt_all)

    @pl.when(pg == pl.num_programs(1) - 1)
    def _():
        acc = acc_ref[...] / l_ref[...]
        ch = lax.broadcasted_iota(jnp.int32, (ds, AB_FOX), 1) // DH
        out = jnp.zeros((ds, AB_FOX), F32)
        for h in range(H_B):
            out = out + jnp.where(ch == h, acc[h * ds:(h + 1) * ds, :], 0.0)
        o_ref[0] = out.astype(BF16)


def _fox_decode(page_table, layer, q, kn, vn, lf_row, lf_t, sel, cache_kt, cache_vt, cache_lft, gp):
    nb, ds, _ = q.shape
    npages = page_table.shape[1]
    page = cache_kt.shape[3]
    nrow = H_B * ds

    per_b = lambda b, pg, pt: (b, 0, 0)
    in_specs = [pl.BlockSpec((1,) + a.shape[1:], per_b) for a in (q, kn, vn, lf_row, lf_t)]
    in_specs += [pl.BlockSpec(sel.shape, lambda b, pg, pt: (0, 0))]
    in_specs += [pl.BlockSpec(memory_space=pl.ANY)] * 3
    grid_spec = pltpu.PrefetchScalarGridSpec(
        num_scalar_prefetch=1,
        grid=(nb, npages // gp),
        in_specs=in_specs,
        out_specs=pl.BlockSpec((1, ds, AB_FOX), per_b),
        scratch_shapes=[pltpu.VMEM((2, AB_FOX, gp * page), F32), pltpu.VMEM((2, AB_FOX, gp * page), F32),
                        pltpu.VMEM((2, gp, H_B, page), F32),
                        pltpu.SemaphoreType.DMA((2,)), pltpu.SemaphoreType.DMA((2,)), pltpu.SemaphoreType.DMA((2,)),
                        pltpu.VMEM((nrow, AB_FOX), BF16), pltpu.VMEM((nrow, 1), F32),
                        pltpu.VMEM((nrow, 1), F32), pltpu.VMEM((nrow, AB_FOX), F32),
                        pltpu.VMEM((H_B, page), F32), pltpu.VMEM((nrow, 1), F32)])
    return pl.pallas_call(
        functools.partial(_fox_decode_kernel, gp=gp, ds=ds, layer=layer),
        grid_spec=grid_spec,
        out_shape=jax.ShapeDtypeStruct((nb, ds, AB_FOX), BF16),
        compiler_params=_cparams("arbitrary", "arbitrary"),
    )(page_table, q, kn, vn, lf_row, lf_t, sel, cache_kt, cache_vt, cache_lft)


def _logf_selectors(page):
    a = jnp.arange(page)[:, None]
    b = jnp.arange(page)[None, :]
    return jnp.concatenate([a > b, jnp.ones((page, page), bool)], axis=1).astype(BF16)


def _step_after(b, g):
    last = g == pl.num_programs(1) - 1
    return jnp.where(last, b + 1, b), jnp.where(last, 0, g + 1)


def _mla_decode_kernel(pt_ref, ql_ref, qp_ref, cn_ref, pn_ref, cache_c, cache_p, o_ref,
                       cbuf, pbuf, csem, psem, m_ref, l_ref, acc_ref, *, gp, base):
    b = pl.program_id(0)
    pg = pl.program_id(1)
    t = b * pl.num_programs(1) + pg
    slot = lax.rem(t, 2)
    page = cbuf.shape[1] // gp
    ql = ql_ref[0]
    qp = qp_ref[0]
    nrow = ql.shape[0]

    def gather(bb, gg, sl):
        copies = []
        for j in range(gp):
            pid = base + pt_ref[bb, gg * gp + j]
            rows = pl.ds(j * page, page)
            copies.append(pltpu.make_async_copy(cache_c.at[pid], cbuf.at[sl, rows, :], csem.at[sl]))
            copies.append(pltpu.make_async_copy(cache_p.at[pid], pbuf.at[sl, :, rows], psem.at[sl]))
        return copies

    @pl.when(t == 0)
    def _():
        for cp in gather(0, 0, 0):
            cp.start()

    @pl.when(t + 1 < pl.num_programs(0) * pl.num_programs(1))
    def _():
        nb_, ng_ = _step_after(b, pg)
        for cp in gather(nb_, ng_, 1 - slot):
            cp.start()

    for cp in gather(b, pg, slot):
        cp.wait()

    @pl.when(pg == 0)
    def _():
        cn = cn_ref[0]
        npad = cn.shape[0]
        s = _dot_nt(ql, cn) + _dot_nt(qp, pn_ref[0])
        rq = lax.broadcasted_iota(jnp.int32, (nrow, npad), 0) // H_C
        ck = lax.broadcasted_iota(jnp.int32, (nrow, npad), 1)
        s = jnp.where(ck <= rq, s, NEG_INF)
        m = jnp.max(s, -1, keepdims=True)
        p = jnp.exp2(s - m)
        m_ref[...] = m
        l_ref[...] = jnp.sum(p, -1, keepdims=True)
        acc_ref[...] = _dot(p.astype(BF16), cn)

    c_all = cbuf[slot].astype(BF16)
    kp_all = pbuf[slot].astype(BF16)
    s = _dot_nt(ql, c_all) + _dot(qp, kp_all)
    m = m_ref[...]
    m_new = jnp.maximum(m, jnp.max(s, -1, keepdims=True))
    alpha = jnp.exp2(m - m_new)
    p = jnp.exp2(s - m_new)
    m_ref[...] = m_new
    l_ref[...] = alpha * l_ref[...] + jnp.sum(p, -1, keepdims=True)
    acc_ref[...] = alpha * acc_ref[...] + _dot(p.astype(BF16), c_all)

    @pl.when(pg == pl.num_programs(1) - 1)
    def _():
        o_ref[0] = (acc_ref[...] / l_ref[...]).astype(BF16)


def _mla_decode(page_table, base, ql, qp, cn, pn, cache_c, cache_pt, gp):
    nb, nrow, _ = ql.shape
    npages = page_table.shape[1]
    page = cache_c.shape[1]

    per_b = lambda b, pg, pt: (b, 0, 0)
    in_specs = [pl.BlockSpec((1,) + a.shape[1:], per_b) for a in (ql, qp, cn, pn)]
    in_specs += [pl.BlockSpec(memory_space=pl.ANY)] * 2
    grid_spec = pltpu.PrefetchScalarGridSpec(
        num_scalar_prefetch=1,
        grid=(nb, npages // gp),
        in_specs=in_specs,
        out_specs=pl.BlockSpec((1, nrow, KV_LORA), per_b),
        scratch_shapes=[pltpu.VMEM((2, gp * page, KV_LORA), F32), pltpu.VMEM((2, ROPE_D, gp * page), F32),
                        pltpu.SemaphoreType.DMA((2,)), pltpu.SemaphoreType.DMA((2,)),
                        pltpu.VMEM((nrow, 1), F32), pltpu.VMEM((nrow, 1), F32),
                        pltpu.VMEM((nrow, KV_LORA), F32)])
    return pl.pallas_call(
        functools.partial(_mla_decode_kernel, gp=gp, base=base),
        grid_spec=grid_spec,
        out_shape=jax.ShapeDtypeStruct((nb, nrow, KV_LORA), BF16),
        compiler_params=_cparams("arbitrary", "arbitrary"),
    )(page_table, ql, qp, cn, pn, cache_c, cache_pt)


def _ret_rope_tables(pos):
    half = DK // 2
    inv = ROPE_BASE ** (-jnp.arange(half, dtype=F32) / half)
    ang = pos.astype(F32)[:, None] * inv[None, :]
    cos, sin = jnp.cos(ang), jnp.sin(ang)
    return jnp.concatenate([cos, cos], -1), jnp.concatenate([-sin, sin], -1)


def _mla_rope_tables(pos, lane0, passthrough, scale):
    half = ROPE_D // 2
    inv = ROPE_BASE ** (-jnp.arange(half, dtype=F32) / half)
    ang = pos.astype(F32)[:, None] * inv[None, :]
    cos, sin = jnp.cos(ang) * scale, jnp.sin(ang) * scale
    n = pos.shape[0]
    z = lambda w: jnp.zeros((n, w), F32)
    lead = jnp.full((n, lane0), scale if passthrough else 0.0, F32)
    tail = z(LANES - lane0 - ROPE_D)
    c = jnp.concatenate([lead, cos, cos, tail], -1)
    s1 = jnp.concatenate([z(lane0), -sin, z(half), tail], -1)
    s2 = jnp.concatenate([z(lane0), z(half), sin, tail], -1)
    return c, s1, s2


def _retention_tables(length, reps):
    lg = jnp.log(1.0 - 2.0 ** (-5.0 - jnp.arange(H_A, dtype=F32)))
    idx = jnp.arange(length, dtype=F32)
    diff = idx[:, None] - idx[None, :]
    decay = jnp.where(diff >= 0, jnp.exp(lg[:, None, None] * jnp.maximum(diff, 0.0)[None]), 0.0)
    qd = jnp.exp((idx[:, None] + 1.0) * lg[None, :]).T
    kd = jnp.exp((length - 1.0 - idx)[:, None] * lg[None, :]).T
    dl = jnp.exp(length * lg)
    if reps > 1:
        eye = jnp.eye(reps, dtype=F32)
        decay = jnp.einsum('ab,hij->haibj', eye, decay).reshape(H_A, reps * length, reps * length)
        qd = jnp.tile(qd, (1, reps))
        kd = jnp.tile(kd, (1, reps))
    n = reps * length
    qd = jnp.broadcast_to(qd[:, :, None], (H_A, n, DV))
    kd = jnp.broadcast_to(kd[:, :, None], (H_A, n, DK))
    dl = jnp.broadcast_to(dl[:, None, None], (H_A, DK, DV))
    return decay, qd, kd, dl


def _ones_row(heads, lane):
    return (jnp.arange(heads * LANES) % LANES == lane).astype(F32)[None, :]


def _prep_ab_weights(w_in, b_f):
    main = w_in[:, :4 * AB_RET].astype(BF16)
    fox = w_in[:, 4 * AB_RET:4 * AB_RET + 3 * AB_FOX].reshape(D_MODEL, 3 * H_B, DH)
    fox = jnp.pad(fox, ((0, 0), (0, 0), (0, LANES - DH))).reshape(D_MODEL, 3 * FOX_PAD).astype(BF16)
    w_f = jnp.pad(w_in[:, 4 * AB_RET + 3 * AB_FOX:], ((0, 0), (0, LANES - H_B))).astype(BF16)
    w_kvt = w_in[:, 4 * AB_RET + AB_FOX:4 * AB_RET + 3 * AB_FOX].T.astype(BF16)
    bf_row = jnp.pad(b_f, (0, LANES - H_B))[None, :]
    return main, fox, w_f, w_kvt, bf_row, _ones_row(H_B, DH)


def _prep_c_weights(w_in, q_g, w_qb, kv_g, w_kvb):
    win = w_in[:, :Q_LORA + KV_LORA].astype(BF16)
    wpe = jnp.pad(w_in[:, Q_LORA + KV_LORA:], ((0, 0), (0, LANES - ROPE_D))).astype(BF16)
    wqb = w_qb.reshape(Q_LORA, H_C, NOPE + ROPE_D)
    wqb = jnp.pad(wqb, ((0, 0), (0, 0), (0, LANES - NOPE - ROPE_D))).reshape(Q_LORA, H_C * LANES).astype(BF16)
    wkv = w_kvb.reshape(KV_LORA, H_C, NOPE + V_DIM)
    w_uk, w_uv = wkv[..., :NOPE], wkv[..., NOPE:]
    wk = jnp.pad(w_uk, ((0, 0), (0, 0), (0, LANES - NOPE))).reshape(KV_LORA, H_C * LANES).astype(BF16)
    wv = jnp.pad(w_uv, ((0, 0), (0, 0), (0, LANES - V_DIM))).reshape(KV_LORA, H_C * LANES).astype(BF16)
    w_abs = jnp.pad(w_uk.transpose(1, 2, 0), ((0, 0), (0, LANES - NOPE), (0, 0))).astype(BF16)
    w_val = w_uv.transpose(1, 0, 2).astype(BF16)
    return (win, wpe, q_g[None, :], kv_g[None, :], wqb, wk, wv, _ones_row(H_C, V_DIM)), w_abs, w_val


TM = 512
TM_MLP = 1024
TF_MLP = 2048
TQ, TK = 512, 256
OUT_SUB = 128
FLASH_UNROLL = 4
TC = 512
RET_SUB = 4
RET_BB = 16
FOX_HG = 4
MLA_HG = 4
PAGES_PER_STEP = 16
NEW_PAD = 16


def kernel(x_prompt, x_sample, state_ret, cache_fox_k, cache_fox_v, cache_fox_logf, cache_mla_ckv,
           cache_mla_kpe, page_table, w_in_ab, ret_gn_g, fox_b_f, w_out_ab, w_in_c, mla_q_norm_g,
           mla_w_qb, mla_kv_norm_g, mla_w_kvb, w_out_c, ln_mix_g, ln_mix_b, ln_mlp_g, ln_mlp_b,
           mlp_w1, mlp_w2):
    batch, seq, _ = x_prompt.shape
    nb, ds, _ = x_sample.shape
    n_pool, page = cache_fox_k.shape[1], cache_fox_k.shape[2]
    past_len = page_table.shape[1] * page
    mp, ms = batch * seq, nb * ds
    pos_p = jnp.arange(seq)
    pos_s = past_len + jnp.arange(ds)
    pos_s_rows = jnp.tile(pos_s, nb)

    hp = x_prompt.reshape(mp, D_MODEL)
    hs = x_sample.reshape(ms, D_MODEL)
    tms = min(TM, ms)
    outs = {k: [] for k in ("ret_p", "ret_s", "fk_p", "fv_p", "fl_p", "fk_s", "fv_s", "fl_s",
                            "ck_p", "kp_p", "ck_s", "kp_s")}

    for l in range(DEPTH):
        i = l // 2
        row = lambda a: a[None, :]
        if l % 2 == 0:
            wts = _prep_ab_weights(w_in_ab[i], fox_b_f[i])
            w_out = w_out_ab[i].astype(BF16)
            gn = ret_gn_g[i][None, :]
            cos, sin = _ret_rope_tables(pos_p)
            rq, rk, rv, rg, fq, fk16, fv16, lf, fkt, fvt = _proj_ab(hp, *wts[:4], cos, sin, *wts[4:], TM, seq)
            y, s_fin = _ret_prompt(rq, rk, rv, rg, _retention_tables(RET_CHUNK, 1), gn, batch, seq, RET_SUB)
            fqb, fkb = _fox_bias(lf, fq, fk16, batch, seq, TC)
            fo = _flash(fqb, fkb, fv16, batch, seq, H_B, DH, FOX_HG, TQ, TK)
            mix_parts_p = [y, fo]
            unpad = lambda a, n: a.reshape(n, H_B, LANES)[:, :, :DH]
            seq_major = lambda a: a.reshape(batch, H_B, DH, seq).transpose(0, 3, 1, 2)
            outs["ret_p"].append(s_fin)
            outs["fk_p"].append(seq_major(fkt))
            outs["fv_p"].append(seq_major(fvt))
            outs["fl_p"].append(lf.reshape(batch, seq, H_B))
            cos, sin = _ret_rope_tables(pos_s_rows)
            rq, rk, rv, rg, fq, fk16, fv16, lf, fk, fv = _proj_ab(hs, *wts[:4], cos, sin, *wts[4:], tms, None)
            y, s_new = _ret_sample(rq, rk, rv, rg, state_ret[i], _retention_tables(ds, RET_BB), gn, RET_BB, ds)
            padn = lambda a: jnp.pad(unpad(a, ms).reshape(nb, ds, AB_FOX), ((0, 0), (0, NEW_PAD - ds), (0, 0)))
            lf_hq = lf.reshape(nb, ds, H_B).transpose(0, 2, 1)
            fo = _fox_decode(page_table, i, unpad(fq, ms).reshape(nb, ds, AB_FOX), padn(fk16), padn(fv16),
                             lf_hq.reshape(nb, 1, H_B * ds),
                             jnp.pad(lf_hq, ((0, 0), (0, 0), (0, NEW_PAD - ds))),
                             _logf_selectors(page),
                             cache_fox_k.transpose(0, 1, 3, 4, 2).reshape(-1, n_pool, AB_FOX, page),
                             cache_fox_v.transpose(0, 1, 3, 4, 2).reshape(-1, n_pool, AB_FOX, page),
                             cache_fox_logf.transpose(0, 1, 3, 2), PAGES_PER_STEP)
            mix_parts_s = [y, fo.reshape(ms, AB_FOX)]
            outs["ret_s"].append(s_new)
            outs["fk_s"].append(unpad(fk, ms).reshape(nb, ds, H_B, DH))
            outs["fv_s"].append(unpad(fv, ms).reshape(nb, ds, H_B, DH))
            outs["fl_s"].append(lf.reshape(nb, ds, H_B))
        else:
            wts, w_abs, w_val = _prep_c_weights(w_in_c[i], mla_q_norm_g[i], mla_w_qb[i],
                                                mla_kv_norm_g[i], mla_w_kvb[i])
            w_out = w_out_c[i].astype(BF16)
            qscale = (NOPE + ROPE_D) ** -0.5 * LOG2E
            tabs = _mla_rope_tables(pos_p, NOPE, True, qscale) + _mla_rope_tables(pos_p, 0, False, 1.0)
            q, k, v, ckv, kpe = _proj_c(hp, wts, tabs, TM)
            o = _flash(q, k, v, batch, seq, H_C, V_DIM, MLA_HG, TQ, TK)
            mix_parts_p = [o]
            outs["ck_p"].append(ckv.reshape(batch, seq, KV_LORA))
            outs["kp_p"].append(kpe.reshape(batch, seq, ROPE_D))
            tabs = _mla_rope_tables(pos_s_rows, NOPE, True, qscale) + _mla_rope_tables(pos_s_rows, 0, False, 1.0)
            q, _, _, ckv, kpe = _proj_c(hs, wts, tabs, tms)
            q_lat = _head_mm(q, w_abs, 1).reshape(nb, ds * H_C, KV_LORA)
            q_pe = q.reshape(nb, ds * H_C, LANES)[:, :, NOPE:NOPE + ROPE_D]
            padn = lambda a: jnp.pad(a.reshape(nb, ds, -1), ((0, 0), (0, NEW_PAD - ds), (0, 0))).astype(BF16)
            o_lat = _mla_decode(page_table, i * n_pool, q_lat, q_pe, padn(ckv), padn(kpe),
                                cache_mla_ckv.reshape(-1, page, KV_LORA),
                                cache_mla_kpe.transpose(0, 1, 3, 2).reshape(-1, ROPE_D, page), PAGES_PER_STEP)
            o = _head_mm(o_lat.reshape(ms, H_C * KV_LORA), w_val, 2)
            mix_parts_s = [o]
            outs["ck_s"].append(ckv.reshape(nb, ds, KV_LORA))
            outs["kp_s"].append(kpe.reshape(nb, ds, ROPE_D))

        w1 = mlp_w1[l].astype(BF16)
        w2 = mlp_w2[l].astype(BF16)
        hp = _outproj_ln(mix_parts_p, w_out, hp, row(ln_mix_g[l]), row(ln_mix_b[l]), TM)
        hs = _outproj_ln(mix_parts_s, w_out, hs, row(ln_mix_g[l]), row(ln_mix_b[l]), tms)
        hp = _mlp_ln(hp, w1, w2, row(ln_mlp_g[l]), row(ln_mlp_b[l]), TM_MLP, TF_MLP)
        hs = _mlp_ln(hs, w1, w2, row(ln_mlp_g[l]), row(ln_mlp_b[l]), min(TM_MLP, ms), TF_MLP)

    st = lambda k: jnp.stack(outs[k])
    return (hp.reshape(batch, seq, D_MODEL), hs.reshape(nb, ds, D_MODEL),
            st("ret_p"), st("ret_s"), st("fk_p"), st("fv_p"), st("fl_p"),
            st("fk_s"), st("fv_s"), st("fl_s"), st("ck_p"), st("kp_p"), st("ck_s"), st("kp_s"))
```

```python
import functools
import math

import jax
import jax.numpy as jnp
from jax import lax
from jax.experimental import pallas as pl
from jax.experimental.pallas import tpu as pltpu

F32 = jnp.float32
BF16 = jnp.bfloat16

D_MODEL = 1024
DEPTH = 2
H_A, DK, DV = 4, 128, 128
RET_CHUNK = 128
H_B, DH = 8, 64
H_C, NOPE, ROPE_D, V_DIM = 16, 64, 32, 64
Q_LORA, KV_LORA = 768, 256
D_FF = 4 * D_MODEL
ROPE_BASE = 10000.0
LN_EPS = 1e-5
RMS_EPS = 1e-6
GN_EPS = 1e-6
ALPHA = (2 * DEPTH) ** 0.25
AB_RET = H_A * DK
AB_FOX = H_B * DH
FOX_PAD = H_B * 128
LANES = 128
LOG2E = math.log2(math.e)
VMEM_LIMIT = 56 * 1024 * 1024
NEG_INF = float("-inf")


def _cparams(*sem):
    return pltpu.CompilerParams(dimension_semantics=sem, vmem_limit_bytes=VMEM_LIMIT)


def _dot(a, b):
    return jnp.dot(a, b, preferred_element_type=F32)


def _dot_nt(a, b):
    return lax.dot_general(a, b, (((1,), (1,)), ((), ())), preferred_element_type=F32)


def _layer_norm(z, g, b):
    mu = jnp.mean(z, -1, keepdims=True)
    var = jnp.mean(jnp.square(z - mu), -1, keepdims=True)
    return (z - mu) * lax.rsqrt(var + LN_EPS) * g + b


def _rms_norm(z, g):
    return z * lax.rsqrt(jnp.mean(z * z, -1, keepdims=True) + RMS_EPS) * g


def _log_sigmoid(x):
    return jnp.minimum(x, 0.0) - jnp.log1p(jnp.exp(-jnp.abs(x)))


def _full(shape):
    nd = len(shape)
    return pl.BlockSpec(shape, lambda *_: (0,) * nd)


def _proj_ab_kernel(x_ref, w_ref, wfox_ref, wf_ref, wkvt_ref, cos_ref, sin_ref, bf_ref, ones_ref,
                    rq_ref, rk_ref, rv_ref, rg_ref, fq_ref, fk16_ref, fv16_ref, lf_ref, fk_ref, fv_ref,
                    *, seq_minor):
    xb = x_ref[...].astype(BF16)
    cos = cos_ref[...]
    sin = sin_ref[...]

    def slab(j):
        return _dot(xb, w_ref[:, j * AB_RET:(j + 1) * AB_RET])

    def fox_slab(j):
        return _dot(xb, wfox_ref[:, j * FOX_PAD:(j + 1) * FOX_PAD])

    def rope(h):
        parts = []
        for g in range(H_A):
            seg = h[:, g * DK:(g + 1) * DK]
            parts.append(seg * cos + pltpu.roll(seg, DK // 2, 1) * sin)
        return jnp.concatenate(parts, axis=1)

    rq_ref[...] = rope(slab(0))
    rk_ref[...] = rope(slab(1)) * (DK ** -0.5)
    rv_ref[...] = slab(2)
    rg_ref[...] = slab(3)
    fq_ref[...] = (fox_slab(0) * (DH ** -0.5 * LOG2E)).astype(BF16)
    fk = fox_slab(1)
    fv = fox_slab(2)
    fk16_ref[...] = fk.astype(BF16)
    fv16_ref[...] = (fv + ones_ref[...]).astype(BF16)
    ff = _dot(xb, wf_ref[...]) + bf_ref[...]
    lf_ref[...] = _log_sigmoid(ff)[:, :H_B]
    if seq_minor:
        kvt = _dot_nt(wkvt_ref[...], xb)
        fk_ref[0] = kvt[:AB_FOX]
        fv_ref[0] = kvt[AB_FOX:]
    else:
        fk_ref[...] = fk
        fv_ref[...] = fv


def _proj_ab(x, w_main, w_fox, w_f, w_kvt, cos, sin, bf_row, ones_row, tm, seq):
    m = x.shape[0]
    ntab = cos.shape[0] // tm
    row = lambda i: (i, 0)
    tab = lambda i: (i % ntab, 0)
    ret = lambda dt: jax.ShapeDtypeStruct((m, AB_RET), dt)
    fox = lambda dt: jax.ShapeDtypeStruct((m, FOX_PAD), dt)
    rspec = pl.BlockSpec((tm, AB_RET), row)
    fspec = pl.BlockSpec((tm, FOX_PAD), row)
    if seq is None:
        kv_shape, kv_spec = fox(F32), fspec
    else:
        per = seq // tm
        kv_shape = jax.ShapeDtypeStruct((m // seq, AB_FOX, seq), F32)
        kv_spec = pl.BlockSpec((1, AB_FOX, tm), lambda i: (i // per, 0, i % per))
    outs = (ret(F32), ret(F32), ret(F32), ret(F32), fox(BF16), fox(BF16), fox(BF16),
            jax.ShapeDtypeStruct((m, H_B), F32), kv_shape, kv_shape)
    return pl.pallas_call(
        functools.partial(_proj_ab_kernel, seq_minor=seq is not None),
        grid=(m // tm,),
        in_specs=[pl.BlockSpec((tm, D_MODEL), row), _full(w_main.shape), _full(w_fox.shape), _full(w_f.shape),
                  _full(w_kvt.shape), pl.BlockSpec((tm, LANES), tab), pl.BlockSpec((tm, LANES), tab),
                  _full(bf_row.shape), _full(ones_row.shape)],
        out_specs=[rspec] * 4 + [fspec] * 3 + [pl.BlockSpec((tm, H_B), row), kv_spec, kv_spec],
        out_shape=outs,
        compiler_params=_cparams("parallel"),
    )(x, w_main, w_fox, w_f, w_kvt, cos, sin, bf_row, ones_row)


def _split3(x):
    x1 = x.astype(BF16).astype(F32)
    x2 = (x - x1).astype(BF16).astype(F32)
    x3 = (x - x1 - x2).astype(BF16).astype(F32)
    return x1, x2, x3


def _fox_bias_kernel(lf_ref, q_ref, k_ref, qo_ref, ko_ref, car_ref):
    @pl.when(pl.program_id(1) == 0)
    def _():
        car_ref[...] = jnp.zeros_like(car_ref)

    tc = lf_ref.shape[0]
    r = lax.broadcasted_iota(jnp.int32, (tc, tc), 0)
    c = lax.broadcasted_iota(jnp.int32, (tc, tc), 1)
    lower = jnp.where(c <= r, 1.0, 0.0).astype(BF16)
    cs = car_ref[...]
    for part in _split3(lf_ref[...]):
        cs = cs + _dot(lower, part.astype(BF16))
    car_ref[...] = cs[tc - 1:tc, :]
    cs = cs * LOG2E
    lane = lax.broadcasted_iota(jnp.int32, (tc, LANES), 1)
    ones_q = jnp.where((lane >= DH + 3) & (lane < DH + 6), 1.0, 0.0)
    ones_k = jnp.where((lane >= DH) & (lane < DH + 3), 1.0, 0.0)
    for h in range(H_B):
        lanes = slice(h * LANES, (h + 1) * LANES)
        parts = _split3(cs[:, h:h + 1])
        q_add = ones_q
        k_add = ones_k
        for t, part in enumerate(parts):
            q_add = q_add + jnp.where(lane == DH + t, part, 0.0)
            k_add = k_add - jnp.where(lane == DH + 3 + t, part, 0.0)
        qo_ref[:, lanes] = (q_ref[:, lanes].astype(F32) + q_add).astype(BF16)
        ko_ref[:, lanes] = (k_ref[:, lanes].astype(F32) + k_add).astype(BF16)


def _fox_bias(lf, q, k, batch, seq, tc):
    nc = seq // tc
    row = lambda b, j: (b * nc + j, 0)
    wide = pl.BlockSpec((tc, FOX_PAD), row)
    return pl.pallas_call(
        _fox_bias_kernel,
        grid=(batch, nc),
        in_specs=[pl.BlockSpec((tc, H_B), row), wide, wide],
        out_specs=[wide, wide],
        out_shape=(jax.ShapeDtypeStruct(q.shape, BF16), jax.ShapeDtypeStruct(k.shape, BF16)),
        scratch_shapes=[pltpu.VMEM((1, H_B), F32)],
        compiler_params=_cparams("arbitrary", "arbitrary"),
    )(lf, q, k)


def _gn_gate(o, gate, gn):
    mu = jnp.mean(o, -1, keepdims=True)
    var = jnp.mean(jnp.square(o - mu), -1, keepdims=True)
    y = (o - mu) * lax.rsqrt(var + GN_EPS) * gn
    return (y * (gate / (1.0 + jnp.exp(-gate)))).astype(BF16)


def _ret_prompt_kernel(q_ref, k_ref, v_ref, g_ref, dec_ref, qd_ref, kd_ref, dl_ref, gn_ref,
                       y_ref, sfin_ref, st_ref, *, nsub):
    @pl.when(pl.program_id(1) == 0)
    def _():
        st_ref[...] = jnp.zeros_like(st_ref)

    for s in range(nsub):
        rows = slice(s * RET_CHUNK, (s + 1) * RET_CHUNK)
        for h in range(H_A):
            cols = slice(h * DK, (h + 1) * DK)
            q16 = q_ref[rows, cols].astype(BF16)
            k = k_ref[rows, cols]
            v16 = v_ref[rows, cols].astype(BF16)
            st = st_ref[h]
            sc = _dot_nt(q16, k.astype(BF16)) * dec_ref[h]
            o = _dot(sc.astype(BF16), v16) + qd_ref[h] * _dot(q16, st.astype(BF16))
            kdt = (k * kd_ref[h]).T.astype(BF16)
            st_ref[h] = dl_ref[h] * st + _dot(kdt, v16)
            y_ref[rows, cols] = _gn_gate(o, g_ref[rows, cols], gn_ref[:, cols])
    sfin_ref[0] = st_ref[...]


def _ret_prompt(rq, rk, rv, rg, tabs, gn, batch, seq, nsub):
    tr = nsub * RET_CHUNK
    nc = seq // tr
    row = lambda b, c: (b * nc + c, 0)
    blk = pl.BlockSpec((tr, AB_RET), row)
    dec, qd, kd, dl = tabs
    return pl.pallas_call(
        functools.partial(_ret_prompt_kernel, nsub=nsub),
        grid=(batch, nc),
        in_specs=[blk, blk, blk, blk, _full(dec.shape), _full(qd.shape), _full(kd.shape),
                  _full(dl.shape), _full(gn.shape)],
        out_specs=[blk, pl.BlockSpec((1, H_A, DK, DV), lambda b, c: (b, 0, 0, 0))],
        out_shape=(jax.ShapeDtypeStruct(rq.shape, BF16),
                   jax.ShapeDtypeStruct((batch, H_A, DK, DV), F32)),
        scratch_shapes=[pltpu.VMEM((H_A, DK, DV), F32)],
        compiler_params=_cparams("arbitrary", "arbitrary"),
    )(rq, rk, rv, rg, dec, qd, kd, dl, gn)


def _ret_sample_kernel(q_ref, k_ref, v_ref, g_ref, st_ref, dec_ref, qd_ref, kd_ref, dl_ref, gn_ref,
                       y_ref, snew_ref, *, bb, ds):
    n = bb * ds
    rb = lax.broadcasted_iota(jnp.int32, (n, DV), 0) // ds
    for h in range(H_A):
        cols = slice(h * DK, (h + 1) * DK)
        q = q_ref[:, cols]
        k = k_ref[:, cols]
        v = v_ref[:, cols]
        v16 = v.astype(BF16)
        sc = _dot_nt(q.astype(BF16), k.astype(BF16)) * dec_ref[h]
        o_intra = _dot(sc.astype(BF16), v16)
        kdt = (k * kd_ref[h]).T.astype(BF16)
        dl = dl_ref[h]

        def per_seq(b, o_cross, h=h, q=q, v=v, kdt=kdt, dl=dl):
            st = st_ref[b, h]
            qb = jnp.where(rb == b, q, 0.0).astype(BF16)
            vb = jnp.where(rb == b, v, 0.0).astype(BF16)
            snew_ref[b, h] = dl * st + _dot(kdt, vb)
            return o_cross + _dot(qb, st.astype(BF16))

        o_cross = lax.fori_loop(0, bb, per_seq, jnp.zeros((n, DV), F32), unroll=4)
        o = o_intra + qd_ref[h] * o_cross
        y_ref[:, cols] = _gn_gate(o, g_ref[:, cols], gn_ref[:, cols])


def _ret_sample(rq, rk, rv, rg, state, tabs, gn, bb, ds):
    nb = state.shape[0]
    n = bb * ds
    row = lambda i: (i, 0)
    blk = pl.BlockSpec((n, AB_RET), row)
    sblk = pl.BlockSpec((bb, H_A, DK, DV), lambda i: (i, 0, 0, 0))
    dec, qd, kd, dl = tabs
    return pl.pallas_call(
        functools.partial(_ret_sample_kernel, bb=bb, ds=ds),
        grid=(nb // bb,),
        in_specs=[blk, blk, blk, blk, sblk, _full(dec.shape), _full(qd.shape), _full(kd.shape),
                  _full(dl.shape), _full(gn.shape)],
        out_specs=[blk, sblk],
        out_shape=(jax.ShapeDtypeStruct(rq.shape, BF16), jax.ShapeDtypeStruct(state.shape, F32)),
        compiler_params=_cparams("parallel"),
    )(rq, rk, rv, rg, state, dec, qd, kd, dl, gn)


def _flash_kernel(q_ref, k_ref, v_ref, o_ref, *, hg, dout, tq, tk):
    i = pl.program_id(2)
    ratio = tq // tk
    ahead = lax.broadcasted_iota(jnp.int32, (tq, tk), 1) - lax.broadcasted_iota(jnp.int32, (tq, tk), 0)

    def step(j, carry):
        ks = pl.ds(pl.multiple_of(j * tk, tk), tk)
        visible = ahead <= i * tq - j * tk
        out = []
        for h in range(hg):
            lanes = slice(h * LANES, (h + 1) * LANES)
            m, acc = carry[h]
            s = jnp.where(visible, _dot_nt(q_ref[:, lanes], k_ref[ks, lanes]), NEG_INF)
            m_new = jnp.maximum(m, jnp.max(s, -1, keepdims=True))
            p = jnp.exp2((s - m_new).astype(BF16))
            acc = jnp.exp2(m - m_new) * acc + _dot(p, v_ref[ks, lanes])
            out.append((m_new, acc))
        return tuple(out)

    def trip(jj, carry):
        for u in range(FLASH_UNROLL):
            carry = step(FLASH_UNROLL * jj + u, carry)
        return carry

    def tail(jj, carry):
        for u in range(ratio):
            carry = step(ratio * jj + u, carry)
        return carry

    init = (jnp.full((tq, 1), NEG_INF, F32), jnp.zeros((tq, LANES), F32))
    nsteps = (i + 1) * ratio
    ntrip = nsteps // FLASH_UNROLL
    carry = lax.fori_loop(0, ntrip, trip, (init,) * hg)
    carry = lax.fori_loop(ntrip * (FLASH_UNROLL // ratio), i + 1, tail, carry)
    for h in range(hg):
        acc = carry[h][1]
        o_ref[:, h * dout:(h + 1) * dout] = (acc[:, :dout] / acc[:, dout:dout + 1]).astype(BF16)


def _flash(q, k, v, batch, seq, heads, dout, hg, tq, tk):
    assert seq % tq == 0 and tq % tk == 0 and FLASH_UNROLL % (tq // tk) == 0 and heads % hg == 0
    nq = seq // tq
    ngrp = heads // hg
    return pl.pallas_call(
        functools.partial(_flash_kernel, hg=hg, dout=dout, tq=tq, tk=tk),
        grid=(batch, ngrp, nq),
        in_specs=[pl.BlockSpec((tq, hg * LANES), lambda b, g, i: (b * nq + i, g)),
                  pl.BlockSpec((seq, hg * LANES), lambda b, g, i: (b, g)),
                  pl.BlockSpec((seq, hg * LANES), lambda b, g, i: (b, g))],
        out_specs=pl.BlockSpec((tq, hg * dout), lambda b, g, i: (b * nq + i, g)),
        out_shape=jax.ShapeDtypeStruct((batch * seq, heads * dout), BF16),
        compiler_params=_cparams("parallel", "parallel", "arbitrary"),
    )(q, k, v)


def _outproj_ln_kernel(*refs, n_in):
    a_refs = refs[:n_in]
    w_ref, x_ref, g_ref, b_ref, o_ref = refs[n_in:]
    tm = x_ref.shape[0]
    sub = min(tm, OUT_SUB)
    for r in range(tm // sub):
        rows = slice(r * sub, (r + 1) * sub)
        acc = None
        off = 0
        for a_ref in a_refs:
            ka = a_ref.shape[1]
            d = _dot(a_ref[rows, :], w_ref[off:off + ka, :])
            acc = d if acc is None else acc + d
            off += ka
        o_ref[rows, :] = _layer_norm(ALPHA * x_ref[rows, :] + acc, g_ref[...], b_ref[...])


def _outproj_ln(parts, w, x, g, b, tm):
    m = x.shape[0]
    row = lambda i: (i, 0)
    in_specs = [pl.BlockSpec((tm, p.shape[1]), row) for p in parts]
    in_specs += [_full(w.shape), pl.BlockSpec((tm, D_MODEL), row), _full(g.shape), _full(b.shape)]
    return pl.pallas_call(
        functools.partial(_outproj_ln_kernel, n_in=len(parts)),
        grid=(m // tm,),
        in_specs=in_specs,
        out_specs=pl.BlockSpec((tm, D_MODEL), row),
        out_shape=jax.ShapeDtypeStruct((m, D_MODEL), F32),
        compiler_params=_cparams("parallel"),
    )(*parts, w, x, g, b)


def _mlp_ln_kernel(x_ref, w1_ref, w2_ref, g_ref, b_ref, o_ref, xb_ref, acc_ref):
    f = pl.program_id(1)

    @pl.when(f == 0)
    def _():
        xb_ref[...] = x_ref[...].astype(BF16)
        acc_ref[...] = jnp.zeros_like(acc_ref)

    h = jnp.maximum(_dot(xb_ref[...], w1_ref[...]), 0.0)
    acc_ref[...] += _dot((h * h).astype(BF16), w2_ref[...])

    @pl.when(f == pl.num_programs(1) - 1)
    def _():
        o_ref[...] = _layer_norm(ALPHA * x_ref[...] + acc_ref[...], g_ref[...], b_ref[...])


def _mlp_ln(x, w1, w2, g, b, tm, tf):
    m = x.shape[0]
    return pl.pallas_call(
        _mlp_ln_kernel,
        grid=(m // tm, D_FF // tf),
        in_specs=[pl.BlockSpec((tm, D_MODEL), lambda i, f: (i, 0)),
                  pl.BlockSpec((D_MODEL, tf), lambda i, f: (0, f)),
                  pl.BlockSpec((tf, D_MODEL), lambda i, f: (f, 0)),
                  _full(g.shape), _full(b.shape)],
        out_specs=pl.BlockSpec((tm, D_MODEL), lambda i, f: (i, 0)),
        out_shape=jax.ShapeDtypeStruct((m, D_MODEL), F32),
        scratch_shapes=[pltpu.VMEM((tm, D_MODEL), BF16), pltpu.VMEM((tm, D_MODEL), F32)],
        compiler_params=_cparams("parallel", "arbitrary"),
    )(x, w1, w2, g, b)


def _rot_lanes(x, c, s1, s2):
    return x * c + pltpu.roll(x, LANES - ROPE_D // 2, 1) * s1 + pltpu.roll(x, ROPE_D // 2, 1) * s2


def _proj_c_kernel(x_ref, win_ref, wpe_ref, qg_ref, kvg_ref, wqb_ref, wk_ref, wv_ref, ones_ref,
                   qc_ref, qs1_ref, qs2_ref, kc_ref, ks1_ref, ks2_ref,
                   q_ref, k_ref, v_ref, ckv_ref, kpe_ref):
    xb = x_ref[...].astype(BF16)
    qa = _dot(xb, win_ref[:, :Q_LORA])
    kva = _dot(xb, win_ref[:, Q_LORA:Q_LORA + KV_LORA])
    kpe = _dot(xb, wpe_ref[...])
    qn = _rms_norm(qa, qg_ref[...]).astype(BF16)
    qc, qs1, qs2 = qc_ref[...], qs1_ref[...], qs2_ref[...]
    for pair in range(H_C // 2):
        both = _dot(qn, wqb_ref[:, pair * 2 * LANES:(pair + 1) * 2 * LANES])
        for e in range(2):
            h = 2 * pair + e
            seg = both[:, e * LANES:(e + 1) * LANES]
            q_ref[:, h * LANES:(h + 1) * LANES] = _rot_lanes(seg, qc, qs1, qs2).astype(BF16)
    ckv = _rms_norm(kva, kvg_ref[...])
    ckv_ref[...] = ckv
    ckv16 = ckv.astype(BF16)
    kpe_rot = _rot_lanes(kpe, kc_ref[...], ks1_ref[...], ks2_ref[...])
    kpe_ref[...] = kpe_rot[:, :ROPE_D]
    kshift = pltpu.roll(kpe_rot, NOPE, 1)
    k_ref[...] = (_dot(ckv16, wk_ref[...]) + jnp.concatenate([kshift] * H_C, axis=1)).astype(BF16)
    v_ref[...] = (_dot(ckv16, wv_ref[...]) + ones_ref[...]).astype(BF16)


def _proj_c(x, wts, tabs, tm):
    m = x.shape[0]
    ntab = tabs[0].shape[0] // tm
    row = lambda i: (i, 0)
    tab = pl.BlockSpec((tm, LANES), lambda i: (i % ntab, 0))
    return pl.pallas_call(
        _proj_c_kernel,
        grid=(m // tm,),
        in_specs=[pl.BlockSpec((tm, D_MODEL), row)] + [_full(w.shape) for w in wts] + [tab] * 6,
        out_specs=[pl.BlockSpec((tm, H_C * LANES), row)] * 3
        + [pl.BlockSpec((tm, KV_LORA), row), pl.BlockSpec((tm, ROPE_D), row)],
        out_shape=(jax.ShapeDtypeStruct((m, H_C * LANES), BF16),) * 3
        + (jax.ShapeDtypeStruct((m, KV_LORA), F32), jax.ShapeDtypeStruct((m, ROPE_D), F32)),
        compiler_params=_cparams("parallel"),
    )(x, *wts, *tabs)


def _head_mm_kernel(x_ref, w_ref, o_ref, *, hp, din, dout):
    for j in range(hp):
        o_ref[:, j * dout:(j + 1) * dout] = _dot(x_ref[:, j * din:(j + 1) * din], w_ref[j]).astype(BF16)


def _head_mm(x, w, hp):
    m = x.shape[0]
    heads, din, dout = w.shape
    return pl.pallas_call(
        functools.partial(_head_mm_kernel, hp=hp, din=din, dout=dout),
        grid=(heads // hp,),
        in_specs=[pl.BlockSpec((m, hp * din), lambda h: (0, h)),
                  pl.BlockSpec((hp, din, dout), lambda h: (h, 0, 0))],
        out_specs=pl.BlockSpec((m, hp * dout), lambda h: (0, h)),
        out_shape=jax.ShapeDtypeStruct((m, heads * dout), BF16),
        compiler_params=_cparams("parallel"),
    )(x, w)


def _fox_decode_kernel(pt_ref, q_ref, kn_ref, vn_ref, lfr_ref, lft_ref, sel_ref, cache_k, cache_v, cache_lf,
                       o_ref, kbuf, vbuf, lbuf, ksem, vsem, lsem,
                       qbd_ref, m_ref, l_ref, acc_ref, car_ref, cnq_ref, *, gp, ds, layer):
    b = pl.program_id(0)
    pg = pl.program_id(1)
    t = b * pl.num_programs(1) + pg
    slot = lax.rem(t, 2)
    nrow = H_B * ds
    page = lbuf.shape[3]
    npages = gp * pl.num_programs(1)

    def gather(bb, gg, sl):
        copies = []
        for j in range(gp):
            pid = pt_ref[bb, npages - 1 - (gg * gp + j)]
            copies.append((pltpu.make_async_copy(cache_k.at[layer, pid], kbuf.at[sl, j], ksem.at[sl]), 0))
            copies.append((pltpu.make_async_copy(cache_v.at[layer, pid], vbuf.at[sl, j], vsem.at[sl]), 1))
            copies.append((pltpu.make_async_copy(cache_lf.at[layer, pid], lbuf.at[sl, j], lsem.at[sl]), 0))
        return copies

    @pl.when(t == 0)
    def _():
        for cp, prio in gather(0, 0, 0):
            cp.start(priority=prio)

    @pl.when(t + 1 < pl.num_programs(0) * pl.num_programs(1))
    def _():
        nb_, ng_ = _step_after(b, pg)
        for cp, prio in gather(nb_, ng_, 1 - slot):
            cp.start(priority=prio)

    for cp, _ in gather(b, pg, slot):
        cp.wait()

    def expand(x):
        return jnp.concatenate([jnp.broadcast_to(x[h:h + 1, :], (ds, x.shape[1])) for h in range(H_B)], axis=0)

    @pl.when(pg == 0)
    def _():
        q = q_ref[0].astype(F32)
        rh = lax.broadcasted_iota(jnp.int32, (nrow, AB_FOX), 0) // ds
        ch = lax.broadcasted_iota(jnp.int32, (nrow, AB_FOX), 1) // DH
        qbd = jnp.where(rh == ch, jnp.concatenate([q] * H_B, axis=0), 0.0).astype(BF16)
        qbd_ref[...] = qbd
        r = lax.broadcasted_iota(jnp.int32, (nrow, nrow), 0)
        c = lax.broadcasted_iota(jnp.int32, (nrow, nrow), 1)
        m_row = jnp.where((r // ds == c // ds) & (c <= r), 1.0, 0.0)
        cn_col = jnp.sum(m_row * lfr_ref[0], axis=1, keepdims=True)
        cnq_ref[...] = cn_col * LOG2E
        lft = lft_ref[0]
        npad = lft.shape[1]
        kk = lax.broadcasted_iota(jnp.int32, (1, npad), 1)
        cnt = jnp.zeros((H_B, npad), F32)
        for k2 in range(ds):
            cnt = cnt + lft[:, k2:k2 + 1] * jnp.where(kk >= k2, 1.0, 0.0)
        s = _dot_nt(qbd, kn_ref[0]) + (cn_col - expand(cnt)) * LOG2E
        rq = lax.broadcasted_iota(jnp.int32, (nrow, npad), 0) % ds
        ck = lax.broadcasted_iota(jnp.int32, (nrow, npad), 1)
        s = jnp.where(ck <= rq, s, NEG_INF)
        m = jnp.max(s, -1, keepdims=True)
        p = jnp.exp2(s - m)
        m_ref[...] = m
        l_ref[...] = jnp.sum(p, -1, keepdims=True)
        acc_ref[...] = _dot(p.astype(BF16), vn_ref[0])
        car_ref[...] = jnp.zeros_like(car_ref)

    qbd = qbd_ref[...]
    xs = []
    for j in range(gp):
        xs += list(_split3(lbuf[slot, j]))
    yz = _dot(jnp.concatenate(xs, axis=0).astype(BF16), sel_ref[...])
    car = car_ref[...]
    revs = []
    for j in range(gp):
        o = j * 3 * H_B
        y = yz[o:o + H_B] + yz[o + H_B:o + 2 * H_B] + yz[o + 2 * H_B:o + 3 * H_B]
        revs.append(car + y[:, :page])
        car = car + y[:, page:]
    car_ref[...] = car
    kt_all = jnp.concatenate([kbuf[slot, j].astype(BF16) for j in range(gp)], axis=1)
    vt_all = jnp.concatenate([vbuf[slot, j].astype(BF16) for j in range(gp)], axis=1)
    s = _dot(qbd, kt_all) + expand(jnp.concatenate(revs, axis=1) * LOG2E) + cnq_ref[...]
    m = m_ref[...]
    m_new = jnp.maximum(m, jnp.max(s, -1, keepdims=True))
    alpha = jnp.exp2(m - m_new)
    p = jnp.exp2(s - m_new)
    m_ref[...] = m_new
    l_ref[...] = alpha * l_ref[...] + jnp.sum(p, -1, keepdims=True)
    acc_ref[...] = alpha * acc_ref[...] + _dot_nt(p.astype(BF16), vt_all)

    @pl.when(pg == pl.num_programs(1) - 1)
    def _():
        acc = acc_ref[...] / l_ref[...]
        ch = lax.broadcasted_iota(jnp.int32, (ds, AB_FOX), 1) // DH
        out = jnp.zeros((ds, AB_FOX), F32)
        for h in range(H_B):
            out = out + jnp.where(ch == h, acc[h * ds:(h + 1) * ds, :], 0.0)
        o_ref[0] = out.astype(BF16)


def _fox_decode(page_table, layer, q, kn, vn, lf_row, lf_t, sel, cache_kt, cache_vt, cache_lft, gp):
    nb, ds, _ = q.shape
    npages = page_table.shape[1]
    page = cache_kt.shape[3]
    nrow = H_B * ds

    per_b = lambda b, pg, pt: (b, 0, 0)
    in_specs = [pl.BlockSpec((1,) + a.shape[1:], per_b) for a in (q, kn, vn, lf_row, lf_t)]
    in_specs += [pl.BlockSpec(sel.shape, lambda b, pg, pt: (0, 0))]
    in_specs += [pl.BlockSpec(memory_space=pl.ANY)] * 3
    grid_spec = pltpu.PrefetchScalarGridSpec(
        num_scalar_prefetch=1,
        grid=(nb, npages // gp),
        in_specs=in_specs,
        out_specs=pl.BlockSpec((1, ds, AB_FOX), per_b),
        scratch_shapes=[pltpu.VMEM((2, gp, AB_FOX, page), F32), pltpu.VMEM((2, gp, AB_FOX, page), F32),
                        pltpu.VMEM((2, gp, H_B, page), F32),
                        pltpu.SemaphoreType.DMA((2,)), pltpu.SemaphoreType.DMA((2,)), pltpu.SemaphoreType.DMA((2,)),
                        pltpu.VMEM((nrow, AB_FOX), BF16), pltpu.VMEM((nrow, 1), F32),
                        pltpu.VMEM((nrow, 1), F32), pltpu.VMEM((nrow, AB_FOX), F32),
                        pltpu.VMEM((H_B, page), F32), pltpu.VMEM((nrow, 1), F32)])
    return pl.pallas_call(
        functools.partial(_fox_decode_kernel, gp=gp, ds=ds, layer=layer),
        grid_spec=grid_spec,
        out_shape=jax.ShapeDtypeStruct((nb, ds, AB_FOX), BF16),
        compiler_params=_cparams("arbitrary", "arbitrary"),
    )(page_table, q, kn, vn, lf_row, lf_t, sel, cache_kt, cache_vt, cache_lft)


def _logf_selectors(page):
    a = jnp.arange(page)[:, None]
    b = jnp.arange(page)[None, :]
    return jnp.concatenate([a > b, jnp.ones((page, page), bool)], axis=1).astype(BF16)


def _step_after(b, g):
    last = g == pl.num_programs(1) - 1
    return jnp.where(last, b + 1, b), jnp.where(last, 0, g + 1)


def _mla_decode_kernel(pt_ref, ql_ref, qp_ref, cn_ref, pn_ref, cache_c, cache_p, o_ref,
                       cbuf, pbuf, csem, psem, m_ref, l_ref, acc_ref, *, gp, base):
    b = pl.program_id(0)
    pg = pl.program_id(1)
    t = b * pl.num_programs(1) + pg
    slot = lax.rem(t, 2)
    page = cbuf.shape[1] // gp
    ql = ql_ref[0]
    qp = qp_ref[0]
    nrow = ql.shape[0]

    def gather(bb, gg, sl):
        copies = []
        for j in range(gp):
            pid = base + pt_ref[bb, gg * gp + j]
            rows = pl.ds(j * page, page)
            copies.append((pltpu.make_async_copy(cache_c.at[pid], cbuf.at[sl, rows, :], csem.at[sl]), 0))
            copies.append((pltpu.make_async_copy(cache_p.at[pid], pbuf.at[sl, :, rows], psem.at[sl]), 1))
        return copies

    @pl.when(t == 0)
    def _():
        for cp, prio in gather(0, 0, 0):
            cp.start(priority=prio)

    @pl.when(t + 1 < pl.num_programs(0) * pl.num_programs(1))
    def _():
        nb_, ng_ = _step_after(b, pg)
        for cp, prio in gather(nb_, ng_, 1 - slot):
            cp.start(priority=prio)

    for cp, _ in gather(b, pg, slot):
        cp.wait()

    @pl.when(pg == 0)
    def _():
        cn = cn_ref[0]
        npad = cn.shape[0]
        s = _dot_nt(ql, cn) + _dot_nt(qp, pn_ref[0])
        rq = lax.broadcasted_iota(jnp.int32, (nrow, npad), 0) // H_C
        ck = lax.broadcasted_iota(jnp.int32, (nrow, npad), 1)
        s = jnp.where(ck <= rq, s, NEG_INF)
        m = jnp.max(s, -1, keepdims=True)
        p = jnp.exp2(s - m)
        m_ref[...] = m
        l_ref[...] = jnp.sum(p, -1, keepdims=True)
        acc_ref[...] = _dot(p.astype(BF16), cn)

    c_all = cbuf[slot].astype(BF16)
    kp_all = pbuf[slot].astype(BF16)
    s = _dot_nt(ql, c_all) + _dot(qp, kp_all)
    m = m_ref[...]
    m_new = jnp.maximum(m, jnp.max(s, -1, keepdims=True))
    alpha = jnp.exp2(m - m_new)
    p = jnp.exp2(s - m_new)
    m_ref[...] = m_new
    l_ref[...] = alpha * l_ref[...] + jnp.sum(p, -1, keepdims=True)
    acc_ref[...] = alpha * acc_ref[...] + _dot(p.astype(BF16), c_all)

    @pl.when(pg == pl.num_programs(1) - 1)
    def _():
        o_ref[0] = (acc_ref[...] / l_ref[...]).astype(BF16)


def _mla_decode(page_table, base, ql, qp, cn, pn, cache_c, cache_pt, gp):
    nb, nrow, _ = ql.shape
    npages = page_table.shape[1]
    page = cache_c.shape[1]

    per_b = lambda b, pg, pt: (b, 0, 0)
    in_specs = [pl.BlockSpec((1,) + a.shape[1:], per_b) for a in (ql, qp, cn, pn)]
    in_specs += [pl.BlockSpec(memory_space=pl.ANY)] * 2
    grid_spec = pltpu.PrefetchScalarGridSpec(
        num_scalar_prefetch=1,
        grid=(nb, npages // gp),
        in_specs=in_specs,
        out_specs=pl.BlockSpec((1, nrow, KV_LORA), per_b),
        scratch_shapes=[pltpu.VMEM((2, gp * page, KV_LORA), F32), pltpu.VMEM((2, ROPE_D, gp * page), F32),
                        pltpu.SemaphoreType.DMA((2,)), pltpu.SemaphoreType.DMA((2,)),
                        pltpu.VMEM((nrow, 1), F32), pltpu.VMEM((nrow, 1), F32),
                        pltpu.VMEM((nrow, KV_LORA), F32)])
    return pl.pallas_call(
        functools.partial(_mla_decode_kernel, gp=gp, base=base),
        grid_spec=grid_spec,
        out_shape=jax.ShapeDtypeStruct((nb, nrow, KV_LORA), BF16),
        compiler_params=_cparams("arbitrary", "arbitrary"),
    )(page_table, ql, qp, cn, pn, cache_c, cache_pt)


def _ret_rope_tables(pos):
    half = DK // 2
    inv = ROPE_BASE ** (-jnp.arange(half, dtype=F32) / half)
    ang = pos.astype(F32)[:, None] * inv[None, :]
    cos, sin = jnp.cos(ang), jnp.sin(ang)
    return jnp.concatenate([cos, cos], -1), jnp.concatenate([-sin, sin], -1)


def _mla_rope_tables(pos, lane0, passthrough, scale):
    half = ROPE_D // 2
    inv = ROPE_BASE ** (-jnp.arange(half, dtype=F32) / half)
    ang = pos.astype(F32)[:, None] * inv[None, :]
    cos, sin = jnp.cos(ang) * scale, jnp.sin(ang) * scale
    n = pos.shape[0]
    z = lambda w: jnp.zeros((n, w), F32)
    lead = jnp.full((n, lane0), scale if passthrough else 0.0, F32)
    tail = z(LANES - lane0 - ROPE_D)
    c = jnp.concatenate([lead, cos, cos, tail], -1)
    s1 = jnp.concatenate([z(lane0), -sin, z(half), tail], -1)
    s2 = jnp.concatenate([z(lane0), z(half), sin, tail], -1)
    return c, s1, s2


def _retention_tables(length, reps):
    lg = jnp.log(1.0 - 2.0 ** (-5.0 - jnp.arange(H_A, dtype=F32)))
    idx = jnp.arange(length, dtype=F32)
    diff = idx[:, None] - idx[None, :]
    decay = jnp.where(diff >= 0, jnp.exp(lg[:, None, None] * jnp.maximum(diff, 0.0)[None]), 0.0)
    qd = jnp.exp((idx[:, None] + 1.0) * lg[None, :]).T
    kd = jnp.exp((length - 1.0 - idx)[:, None] * lg[None, :]).T
    dl = jnp.exp(length * lg)
    if reps > 1:
        eye = jnp.eye(reps, dtype=F32)
        decay = jnp.einsum('ab,hij->haibj', eye, decay).reshape(H_A, reps * length, reps * length)
        qd = jnp.tile(qd, (1, reps))
        kd = jnp.tile(kd, (1, reps))
    n = reps * length
    qd = jnp.broadcast_to(qd[:, :, None], (H_A, n, DV))
    kd = jnp.broadcast_to(kd[:, :, None], (H_A, n, DK))
    dl = jnp.broadcast_to(dl[:, None, None], (H_A, DK, DV))
    return decay, qd, kd, dl


def _ones_row(heads, lane):
    return (jnp.arange(heads * LANES) % LANES == lane).astype(F32)[None, :]


def _prep_ab_weights(w_in, b_f):
    main = w_in[:, :4 * AB_RET].astype(BF16)
    fox = w_in[:, 4 * AB_RET:4 * AB_RET + 3 * AB_FOX].reshape(D_MODEL, 3 * H_B, DH)
    fox = jnp.pad(fox, ((0, 0), (0, 0), (0, LANES - DH))).reshape(D_MODEL, 3 * FOX_PAD).astype(BF16)
    w_f = jnp.pad(w_in[:, 4 * AB_RET + 3 * AB_FOX:], ((0, 0), (0, LANES - H_B))).astype(BF16)
    w_kvt = w_in[:, 4 * AB_RET + AB_FOX:4 * AB_RET + 3 * AB_FOX].T.astype(BF16)
    bf_row = jnp.pad(b_f, (0, LANES - H_B))[None, :]
    return main, fox, w_f, w_kvt, bf_row, _ones_row(H_B, DH)


def _prep_c_weights(w_in, q_g, w_qb, kv_g, w_kvb):
    win = w_in[:, :Q_LORA + KV_LORA].astype(BF16)
    wpe = jnp.pad(w_in[:, Q_LORA + KV_LORA:], ((0, 0), (0, LANES - ROPE_D))).astype(BF16)
    wqb = w_qb.reshape(Q_LORA, H_C, NOPE + ROPE_D)
    wqb = jnp.pad(wqb, ((0, 0), (0, 0), (0, LANES - NOPE - ROPE_D))).reshape(Q_LORA, H_C * LANES).astype(BF16)
    wkv = w_kvb.reshape(KV_LORA, H_C, NOPE + V_DIM)
    w_uk, w_uv = wkv[..., :NOPE], wkv[..., NOPE:]
    wk = jnp.pad(w_uk, ((0, 0), (0, 0), (0, LANES - NOPE))).reshape(KV_LORA, H_C * LANES).astype(BF16)
    wv = jnp.pad(w_uv, ((0, 0), (0, 0), (0, LANES - V_DIM))).reshape(KV_LORA, H_C * LANES).astype(BF16)
    w_abs = jnp.pad(w_uk.transpose(1, 2, 0), ((0, 0), (0, LANES - NOPE), (0, 0))).astype(BF16)
    w_val = w_uv.transpose(1, 0, 2).astype(BF16)
    return (win, wpe, q_g[None, :], kv_g[None, :], wqb, wk, wv, _ones_row(H_C, V_DIM)), w_abs, w_val


TM = 512
TM_MLP = 1024
TF_MLP = 2048
TQ, TK = 512, 256
OUT_SUB = 128
FLASH_UNROLL = 4
TC = 512
RET_SUB = 8
RET_BB = 16
FOX_HG = 4
MLA_HG = 4
PAGES_PER_STEP = 16
NEW_PAD = 16


def kernel(x_prompt, x_sample, state_ret, cache_fox_k, cache_fox_v, cache_fox_logf, cache_mla_ckv,
           cache_mla_kpe, page_table, w_in_ab, ret_gn_g, fox_b_f, w_out_ab, w_in_c, mla_q_norm_g,
           mla_w_qb, mla_kv_norm_g, mla_w_kvb, w_out_c, ln_mix_g, ln_mix_b, ln_mlp_g, ln_mlp_b,
           mlp_w1, mlp_w2):
    batch, seq, _ = x_prompt.shape
    nb, ds, _ = x_sample.shape
    n_pool, page = cache_fox_k.shape[1], cache_fox_k.shape[2]
    past_len = page_table.shape[1] * page
    mp, ms = batch * seq, nb * ds
    pos_p = jnp.arange(seq)
    pos_s = past_len + jnp.arange(ds)
    pos_s_rows = jnp.tile(pos_s, nb)

    hp = x_prompt.reshape(mp, D_MODEL)
    hs = x_sample.reshape(ms, D_MODEL)
    tms = min(TM, ms)
    outs = {k: [] for k in ("ret_p", "ret_s", "fk_p", "fv_p", "fl_p", "fk_s", "fv_s", "fl_s",
                            "ck_p", "kp_p", "ck_s", "kp_s")}

    for l in range(DEPTH):
        i = l // 2
        row = lambda a: a[None, :]
        if l % 2 == 0:
            wts = _prep_ab_weights(w_in_ab[i], fox_b_f[i])
            w_out = w_out_ab[i].astype(BF16)
            gn = ret_gn_g[i][None, :]
            cos, sin = _ret_rope_tables(pos_p)
            rq, rk, rv, rg, fq, fk16, fv16, lf, fkt, fvt = _proj_ab(hp, *wts[:4], cos, sin, *wts[4:], TM, seq)
            y, s_fin = _ret_prompt(rq, rk, rv, rg, _retention_tables(RET_CHUNK, 1), gn, batch, seq, RET_SUB)
            fqb, fkb = _fox_bias(lf, fq, fk16, batch, seq, TC)
            fo = _flash(fqb, fkb, fv16, batch, seq, H_B, DH, FOX_HG, TQ, TK)
            mix_parts_p = [y, fo]
            unpad = lambda a, n: a.reshape(n, H_B, LANES)[:, :, :DH]
            seq_major = lambda a: a.reshape(batch, H_B, DH, seq).transpose(0, 3, 1, 2)
            outs["ret_p"].append(s_fin)
            outs["fk_p"].append(seq_major(fkt))
            outs["fv_p"].append(seq_major(fvt))
            outs["fl_p"].append(lf.reshape(batch, seq, H_B))
            cos, sin = _ret_rope_tables(pos_s_rows)
            rq, rk, rv, rg, fq, fk16, fv16, lf, fk, fv = _proj_ab(hs, *wts[:4], cos, sin, *wts[4:], tms, None)
            y, s_new = _ret_sample(rq, rk, rv, rg, state_ret[i], _retention_tables(ds, RET_BB), gn, RET_BB, ds)
            padn = lambda a: jnp.pad(unpad(a, ms).reshape(nb, ds, AB_FOX), ((0, 0), (0, NEW_PAD - ds), (0, 0)))
            lf_hq = lf.reshape(nb, ds, H_B).transpose(0, 2, 1)
            fo = _fox_decode(page_table, i, unpad(fq, ms).reshape(nb, ds, AB_FOX), padn(fk16), padn(fv16),
                             lf_hq.reshape(nb, 1, H_B * ds),
                             jnp.pad(lf_hq, ((0, 0), (0, 0), (0, NEW_PAD - ds))),
                             _logf_selectors(page),
                             cache_fox_k.transpose(0, 1, 3, 4, 2).reshape(-1, n_pool, AB_FOX, page),
                             cache_fox_v.transpose(0, 1, 3, 4, 2).reshape(-1, n_pool, AB_FOX, page),
                             cache_fox_logf.transpose(0, 1, 3, 2), PAGES_PER_STEP)
            mix_parts_s = [y, fo.reshape(ms, AB_FOX)]
            outs["ret_s"].append(s_new)
            outs["fk_s"].append(unpad(fk, ms).reshape(nb, ds, H_B, DH))
            outs["fv_s"].append(unpad(fv, ms).reshape(nb, ds, H_B, DH))
            outs["fl_s"].append(lf.reshape(nb, ds, H_B))
        else:
            wts, w_abs, w_val = _prep_c_weights(w_in_c[i], mla_q_norm_g[i], mla_w_qb[i],
                                                mla_kv_norm_g[i], mla_w_kvb[i])
            w_out = w_out_c[i].astype(BF16)
            qscale = (NOPE + ROPE_D) ** -0.5 * LOG2E
            tabs = _mla_rope_tables(pos_p, NOPE, True, qscale) + _mla_rope_tables(pos_p, 0, False, 1.0)
            q, k, v, ckv, kpe = _proj_c(hp, wts, tabs, TM)
            o = _flash(q, k, v, batch, seq, H_C, V_DIM, MLA_HG, TQ, TK)
            mix_parts_p = [o]
            outs["ck_p"].append(ckv.reshape(batch, seq, KV_LORA))
            outs["kp_p"].append(kpe.reshape(batch, seq, ROPE_D))
            tabs = _mla_rope_tables(pos_s_rows, NOPE, True, qscale) + _mla_rope_tables(pos_s_rows, 0, False, 1.0)
            q, _, _, ckv, kpe = _proj_c(hs, wts, tabs, tms)
            q_lat = _head_mm(q, w_abs, 1).reshape(nb, ds * H_C, KV_LORA)
            q_pe = q.reshape(nb, ds * H_C, LANES)[:, :, NOPE:NOPE + ROPE_D]
            padn = lambda a: jnp.pad(a.reshape(nb, ds, -1), ((0, 0), (0, NEW_PAD - ds), (0, 0))).astype(BF16)
            o_lat = _mla_decode(page_table, i * n_pool, q_lat, q_pe, padn(ckv), padn(kpe),
                                cache_mla_ckv.reshape(-1, page, KV_LORA),
                                cache_mla_kpe.transpose(0, 1, 3, 2).reshape(-1, ROPE_D, page), PAGES_PER_STEP)
            o = _head_mm(o_lat.reshape(ms, H_C * KV_LORA), w_val, 2)
            mix_parts_s = [o]
            outs["ck_s"].append(ckv.reshape(nb, ds, KV_LORA))
            outs["kp_s"].append(kpe.reshape(nb, ds, ROPE_D))

        w1 = mlp_w1[l].astype(BF16)
        w2 = mlp_w2[l].astype(BF16)
        hp = _outproj_ln(mix_parts_p, w_out, hp, row(ln_mix_g[l]), row(ln_mix_b[l]), TM)
        hs = _outproj_ln(mix_parts_s, w_out, hs, row(ln_mix_g[l]), row(ln_mix_b[l]), tms)
        hp = _mlp_ln(hp, w1, w2, row(ln_mlp_g[l]), row(ln_mlp_b[l]), TM_MLP, TF_MLP)
        hs = _mlp_ln(hs, w1, w2, row(ln_mlp_g[l]), row(ln_mlp_b[l]), min(TM_MLP, ms), TF_MLP)

    st = lambda k: jnp.stack(outs[k])
    return (hp.reshape(batch, seq, D_MODEL), hs.reshape(nb, ds, D_MODEL),
            st("ret_p"), st("ret_s"), st("fk_p"), st("fv_p"), st("fl_p"),
            st("fk_s"), st("fv_s"), st("fl_s"), st("ck_p"), st("kp_p"), st("ck_s"), st("kp_s"))
```

```python
import functools
import math

import jax
import jax.numpy as jnp
from jax import lax
from jax.experimental import pallas as pl
from jax.experimental.pallas import tpu as pltpu

F32 = jnp.float32
BF16 = jnp.bfloat16

D_MODEL = 1024
DEPTH = 2
H_A, DK, DV = 4, 128, 128
RET_CHUNK = 128
H_B, DH = 8, 64
H_C, NOPE, ROPE_D, V_DIM = 16, 64, 32, 64
Q_LORA, KV_LORA = 768, 256
D_FF = 4 * D_MODEL
ROPE_BASE = 10000.0
LN_EPS = 1e-5
RMS_EPS = 1e-6
GN_EPS = 1e-6
ALPHA = (2 * DEPTH) ** 0.25
AB_RET = H_A * DK
AB_FOX = H_B * DH
FOX_PAD = H_B * 128
LANES = 128
LOG2E = math.log2(math.e)
VMEM_LIMIT = 56 * 1024 * 1024
NEG_INF = float("-inf")


def _cparams(*sem):
    return pltpu.CompilerParams(dimension_semantics=sem, vmem_limit_bytes=VMEM_LIMIT)


def _dot(a, b):
    return jnp.dot(a, b, preferred_element_type=F32)


def _dot_nt(a, b):
    return lax.dot_general(a, b, (((1,), (1,)), ((), ())), preferred_element_type=F32)


def _layer_norm(z, g, b):
    mu = jnp.mean(z, -1, keepdims=True)
    var = jnp.mean(jnp.square(z - mu), -1, keepdims=True)
    return (z - mu) * lax.rsqrt(var + LN_EPS) * g + b


def _rms_norm(z, g):
    return z * lax.rsqrt(jnp.mean(z * z, -1, keepdims=True) + RMS_EPS) * g


def _log_sigmoid(x):
    return jnp.minimum(x, 0.0) - jnp.log1p(jnp.exp(-jnp.abs(x)))


def _full(shape):
    nd = len(shape)
    return pl.BlockSpec(shape, lambda *_: (0,) * nd)


def _proj_ab_kernel(x_ref, w_ref, wfox_ref, wf_ref, wkvt_ref, cos_ref, sin_ref, bf_ref, ones_ref,
                    rq_ref, rk_ref, rv_ref, rg_ref, fq_ref, fk16_ref, fv16_ref, lf_ref, fk_ref, fv_ref,
                    *, seq_minor):
    xb = x_ref[...].astype(BF16)
    cos = cos_ref[...]
    sin = sin_ref[...]

    def slab(j):
        return _dot(xb, w_ref[:, j * AB_RET:(j + 1) * AB_RET])

    def fox_slab(j):
        return _dot(xb, wfox_ref[:, j * FOX_PAD:(j + 1) * FOX_PAD])

    def rope(h):
        parts = []
        for g in range(H_A):
            seg = h[:, g * DK:(g + 1) * DK]
            parts.append(seg * cos + pltpu.roll(seg, DK // 2, 1) * sin)
        return jnp.concatenate(parts, axis=1)

    rq_ref[...] = rope(slab(0))
    rk_ref[...] = rope(slab(1)) * (DK ** -0.5)
    rv_ref[...] = slab(2)
    rg_ref[...] = slab(3)
    fq_ref[...] = (fox_slab(0) * (DH ** -0.5 * LOG2E)).astype(BF16)
    fk = fox_slab(1)
    fv = fox_slab(2)
    fk16_ref[...] = fk.astype(BF16)
    fv16_ref[...] = (fv + ones_ref[...]).astype(BF16)
    ff = _dot(xb, wf_ref[...]) + bf_ref[...]
    lf_ref[...] = _log_sigmoid(ff)[:, :H_B]
    if seq_minor:
        kvt = _dot_nt(wkvt_ref[...], xb)
        fk_ref[0] = kvt[:AB_FOX]
        fv_ref[0] = kvt[AB_FOX:]
    else:
        fk_ref[...] = fk
        fv_ref[...] = fv


def _proj_ab(x, w_main, w_fox, w_f, w_kvt, cos, sin, bf_row, ones_row, tm, seq):
    m = x.shape[0]
    ntab = cos.shape[0] // tm
    row = lambda i: (i, 0)
    tab = lambda i: (i % ntab, 0)
    ret = lambda dt: jax.ShapeDtypeStruct((m, AB_RET), dt)
    fox = lambda dt: jax.ShapeDtypeStruct((m, FOX_PAD), dt)
    rspec = pl.BlockSpec((tm, AB_RET), row)
    fspec = pl.BlockSpec((tm, FOX_PAD), row)
    if seq is None:
        kv_shape, kv_spec = fox(F32), fspec
    else:
        per = seq // tm
        kv_shape = jax.ShapeDtypeStruct((m // seq, AB_FOX, seq), F32)
        kv_spec = pl.BlockSpec((1, AB_FOX, tm), lambda i: (i // per, 0, i % per))
    outs = (ret(F32), ret(F32), ret(F32), ret(F32), fox(BF16), fox(BF16), fox(BF16),
            jax.ShapeDtypeStruct((m, H_B), F32), kv_shape, kv_shape)
    return pl.pallas_call(
        functools.partial(_proj_ab_kernel, seq_minor=seq is not None),
        grid=(m // tm,),
        in_specs=[pl.BlockSpec((tm, D_MODEL), row), _full(w_main.shape), _full(w_fox.shape), _full(w_f.shape),
                  _full(w_kvt.shape), pl.BlockSpec((tm, LANES), tab), pl.BlockSpec((tm, LANES), tab),
                  _full(bf_row.shape), _full(ones_row.shape)],
        out_specs=[rspec] * 4 + [fspec] * 3 + [pl.BlockSpec((tm, H_B), row), kv_spec, kv_spec],
        out_shape=outs,
        compiler_params=_cparams("parallel"),
    )(x, w_main, w_fox, w_f, w_kvt, cos, sin, bf_row, ones_row)


def _split3(x):
    x1 = x.astype(BF16).astype(F32)
    x2 = (x - x1).astype(BF16).astype(F32)
    x3 = (x - x1 - x2).astype(BF16).astype(F32)
    return x1, x2, x3


def _fox_bias_kernel(lf_ref, q_ref, k_ref, qo_ref, ko_ref, car_ref):
    @pl.when(pl.program_id(1) == 0)
    def _():
        car_ref[...] = jnp.zeros_like(car_ref)

    tc = lf_ref.shape[0]
    r = lax.broadcasted_iota(jnp.int32, (tc, tc), 0)
    c = lax.broadcasted_iota(jnp.int32, (tc, tc), 1)
    lower = jnp.where(c <= r, 1.0, 0.0).astype(BF16)
    cs = car_ref[...]
    for part in _split3(lf_ref[...]):
        cs = cs + _dot(lower, part.astype(BF16))
    car_ref[...] = cs[tc - 1:tc, :]
    cs = cs * LOG2E
    lane = lax.broadcasted_iota(jnp.int32, (tc, LANES), 1)
    ones_q = jnp.where((lane >= DH + 3) & (lane < DH + 6), 1.0, 0.0)
    ones_k = jnp.where((lane >= DH) & (lane < DH + 3), 1.0, 0.0)
    for h in range(H_B):
        lanes = slice(h * LANES, (h + 1) * LANES)
        parts = _split3(cs[:, h:h + 1])
        q_add = ones_q
        k_add = ones_k
        for t, part in enumerate(parts):
            q_add = q_add + jnp.where(lane == DH + t, part, 0.0)
            k_add = k_add - jnp.where(lane == DH + 3 + t, part, 0.0)
        qo_ref[:, lanes] = (q_ref[:, lanes].astype(F32) + q_add).astype(BF16)
        ko_ref[:, lanes] = (k_ref[:, lanes].astype(F32) + k_add).astype(BF16)


def _fox_bias(lf, q, k, batch, seq, tc):
    nc = seq // tc
    row = lambda b, j: (b * nc + j, 0)
    wide = pl.BlockSpec((tc, FOX_PAD), row)
    return pl.pallas_call(
        _fox_bias_kernel,
        grid=(batch, nc),
        in_specs=[pl.BlockSpec((tc, H_B), row), wide, wide],
        out_specs=[wide, wide],
        out_shape=(jax.ShapeDtypeStruct(q.shape, BF16), jax.ShapeDtypeStruct(k.shape, BF16)),
        scratch_shapes=[pltpu.VMEM((1, H_B), F32)],
        compiler_params=_cparams("arbitrary", "arbitrary"),
    )(lf, q, k)


def _gn_gate(o, gate, gn):
    mu = jnp.mean(o, -1, keepdims=True)
    var = jnp.mean(jnp.square(o - mu), -1, keepdims=True)
    y = (o - mu) * lax.rsqrt(var + GN_EPS) * gn
    return (y * (gate / (1.0 + jnp.exp(-gate)))).astype(BF16)


def _ret_prompt_kernel(q_ref, k_ref, v_ref, g_ref, dec_ref, qd_ref, kd_ref, dl_ref, gn_ref,
                       y_ref, sfin_ref, st_ref, *, nsub):
    @pl.when(pl.program_id(1) == 0)
    def _():
        st_ref[...] = jnp.zeros_like(st_ref)

    for s in range(nsub):
        rows = slice(s * RET_CHUNK, (s + 1) * RET_CHUNK)
        for h in range(H_A):
            cols = slice(h * DK, (h + 1) * DK)
            q16 = q_ref[rows, cols].astype(BF16)
            k = k_ref[rows, cols]
            v16 = v_ref[rows, cols].astype(BF16)
            st = st_ref[h]
            sc = _dot_nt(q16, k.astype(BF16)) * dec_ref[h]
            o = _dot(sc.astype(BF16), v16) + qd_ref[h] * _dot(q16, st.astype(BF16))
            kdt = (k * kd_ref[h]).T.astype(BF16)
            st_ref[h] = dl_ref[h] * st + _dot(kdt, v16)
            y_ref[rows, cols] = _gn_gate(o, g_ref[rows, cols], gn_ref[:, cols])
    sfin_ref[0] = st_ref[...]


def _ret_prompt(rq, rk, rv, rg, tabs, gn, batch, seq, nsub):
    tr = nsub * RET_CHUNK
    nc = seq // tr
    row = lambda b, c: (b * nc + c, 0)
    blk = pl.BlockSpec((tr, AB_RET), row)
    dec, qd, kd, dl = tabs
    return pl.pallas_call(
        functools.partial(_ret_prompt_kernel, nsub=nsub),
        grid=(batch, nc),
        in_specs=[blk, blk, blk, blk, _full(dec.shape), _full(qd.shape), _full(kd.shape),
                  _full(dl.shape), _full(gn.shape)],
        out_specs=[blk, pl.BlockSpec((1, H_A, DK, DV), lambda b, c: (b, 0, 0, 0))],
        out_shape=(jax.ShapeDtypeStruct(rq.shape, BF16),
                   jax.ShapeDtypeStruct((batch, H_A, DK, DV), F32)),
        scratch_shapes=[pltpu.VMEM((H_A, DK, DV), F32)],
        compiler_params=_cparams("arbitrary", "arbitrary"),
    )(rq, rk, rv, rg, dec, qd, kd, dl, gn)


def _ret_sample_kernel(q_ref, k_ref, v_ref, g_ref, st_ref, dec_ref, qd_ref, kd_ref, dl_ref, gn_ref,
                       y_ref, snew_ref, *, bb, ds):
    n = bb * ds
    rb = lax.broadcasted_iota(jnp.int32, (n, DV), 0) // ds
    for h in range(H_A):
        cols = slice(h * DK, (h + 1) * DK)
        q = q_ref[:, cols]
        k = k_ref[:, cols]
        v = v_ref[:, cols]
        v16 = v.astype(BF16)
        sc = _dot_nt(q.astype(BF16), k.astype(BF16)) * dec_ref[h]
        o_intra = _dot(sc.astype(BF16), v16)
        kdt = (k * kd_ref[h]).T.astype(BF16)
        dl = dl_ref[h]

        def per_seq(b, o_cross, h=h, q=q, v=v, kdt=kdt, dl=dl):
            st = st_ref[b, h]
            qb = jnp.where(rb == b, q, 0.0).astype(BF16)
            vb = jnp.where(rb == b, v, 0.0).astype(BF16)
            snew_ref[b, h] = dl * st + _dot(kdt, vb)
            return o_cross + _dot(qb, st.astype(BF16))

        o_cross = lax.fori_loop(0, bb, per_seq, jnp.zeros((n, DV), F32), unroll=4)
        o = o_intra + qd_ref[h] * o_cross
        y_ref[:, cols] = _gn_gate(o, g_ref[:, cols], gn_ref[:, cols])


def _ret_sample(rq, rk, rv, rg, state, tabs, gn, bb, ds):
    nb = state.shape[0]
    n = bb * ds
    row = lambda i: (i, 0)
    blk = pl.BlockSpec((n, AB_RET), row)
    sblk = pl.BlockSpec((bb, H_A, DK, DV), lambda i: (i, 0, 0, 0))
    dec, qd, kd, dl = tabs
    return pl.pallas_call(
        functools.partial(_ret_sample_kernel, bb=bb, ds=ds),
        grid=(nb // bb,),
        in_specs=[blk, blk, blk, blk, sblk, _full(dec.shape), _full(qd.shape), _full(kd.shape),
                  _full(dl.shape), _full(gn.shape)],
        out_specs=[blk, sblk],
        out_shape=(jax.ShapeDtypeStruct(rq.shape, BF16), jax.ShapeDtypeStruct(state.shape, F32)),
        compiler_params=_cparams("parallel"),
    )(rq, rk, rv, rg, state, dec, qd, kd, dl, gn)


def _flash_kernel(q_ref, k_ref, v_ref, o_ref, *, hg, dout, tq, tk):
    i = pl.program_id(2)
    ratio = tq // tk
    ahead = lax.broadcasted_iota(jnp.int32, (tq, tk), 1) - lax.broadcasted_iota(jnp.int32, (tq, tk), 0)

    def step(j, carry):
        ks = pl.ds(pl.multiple_of(j * tk, tk), tk)
        visible = ahead <= i * tq - j * tk
        out = []
        for h in range(hg):
            lanes = slice(h * LANES, (h + 1) * LANES)
            m, acc = carry[h]
            s = jnp.where(visible, _dot_nt(q_ref[:, lanes], k_ref[ks, lanes]), NEG_INF)
            m_new = jnp.maximum(m, jnp.max(s, -1, keepdims=True))
            p = jnp.exp2((s - m_new).astype(BF16))
            acc = jnp.exp2(m - m_new) * acc + _dot(p, v_ref[ks, lanes])
            out.append((m_new, acc))
        return tuple(out)

    def trip(jj, carry):
        for u in range(FLASH_UNROLL):
            carry = step(FLASH_UNROLL * jj + u, carry)
        return carry

    def tail(jj, carry):
        for u in range(ratio):
            carry = step(ratio * jj + u, carry)
        return carry

    init = (jnp.full((tq, 1), NEG_INF, F32), jnp.zeros((tq, LANES), F32))
    nsteps = (i + 1) * ratio
    ntrip = nsteps // FLASH_UNROLL
    carry = lax.fori_loop(0, ntrip, trip, (init,) * hg)
    carry = lax.fori_loop(ntrip * (FLASH_UNROLL // ratio), i + 1, tail, carry)
    for h in range(hg):
        acc = carry[h][1]
        o_ref[:, h * dout:(h + 1) * dout] = (acc[:, :dout] / acc[:, dout:dout + 1]).astype(BF16)


def _flash(q, k, v, batch, seq, heads, dout, hg, tq, tk):
    assert seq % tq == 0 and tq % tk == 0 and FLASH_UNROLL % (tq // tk) == 0 and heads % hg == 0
    nq = seq // tq
    ngrp = heads // hg
    return pl.pallas_call(
        functools.partial(_flash_kernel, hg=hg, dout=dout, tq=tq, tk=tk),
        grid=(batch, ngrp, nq),
        in_specs=[pl.BlockSpec((tq, hg * LANES), lambda b, g, i: (b * nq + i, g)),
                  pl.BlockSpec((seq, hg * LANES), lambda b, g, i: (b, g)),
                  pl.BlockSpec((seq, hg * LANES), lambda b, g, i: (b, g))],
        out_specs=pl.BlockSpec((tq, hg * dout), lambda b, g, i: (b * nq + i, g)),
        out_shape=jax.ShapeDtypeStruct((batch * seq, heads * dout), BF16),
        compiler_params=_cparams("parallel", "parallel", "arbitrary"),
    )(q, k, v)


def _outproj_ln_kernel(*refs, n_in):
    a_refs = refs[:n_in]
    w_ref, x_ref, g_ref, b_ref, o_ref = refs[n_in:]
    tm = x_ref.shape[0]
    sub = min(tm, OUT_SUB)
    for r in range(tm // sub):
        rows = slice(r * sub, (r + 1) * sub)
        acc = None
        off = 0
        for a_ref in a_refs:
            ka = a_ref.shape[1]
            d = _dot(a_ref[rows, :], w_ref[off:off + ka, :])
            acc = d if acc is None else acc + d
            off += ka
        o_ref[rows, :] = _layer_norm(ALPHA * x_ref[rows, :] + acc, g_ref[...], b_ref[...])


def _outproj_ln(parts, w, x, g, b, tm):
    m = x.shape[0]
    row = lambda i: (i, 0)
    in_specs = [pl.BlockSpec((tm, p.shape[1]), row) for p in parts]
    in_specs += [_full(w.shape), pl.BlockSpec((tm, D_MODEL), row), _full(g.shape), _full(b.shape)]
    return pl.pallas_call(
        functools.partial(_outproj_ln_kernel, n_in=len(parts)),
        grid=(m // tm,),
        in_specs=in_specs,
        out_specs=pl.BlockSpec((tm, D_MODEL), row),
        out_shape=jax.ShapeDtypeStruct((m, D_MODEL), F32),
        compiler_params=_cparams("parallel"),
    )(*parts, w, x, g, b)


def _mlp_ln_kernel(x_ref, w1_ref, w2_ref, g_ref, b_ref, o_ref, xb_ref, acc_ref):
    f = pl.program_id(1)

    @pl.when(f == 0)
    def _():
        xb_ref[...] = x_ref[...].astype(BF16)
        acc_ref[...] = jnp.zeros_like(acc_ref)

    h = jnp.maximum(_dot(xb_ref[...], w1_ref[...]), 0.0)
    acc_ref[...] += _dot((h * h).astype(BF16), w2_ref[...])

    @pl.when(f == pl.num_programs(1) - 1)
    def _():
        o_ref[...] = _layer_norm(ALPHA * x_ref[...] + acc_ref[...], g_ref[...], b_ref[...])


def _mlp_ln(x, w1, w2, g, b, tm, tf):
    m = x.shape[0]
    return pl.pallas_call(
        _mlp_ln_kernel,
        grid=(m // tm, D_FF // tf),
        in_specs=[pl.BlockSpec((tm, D_MODEL), lambda i, f: (i, 0)),
                  pl.BlockSpec((D_MODEL, tf), lambda i, f: (0, f)),
                  pl.BlockSpec((tf, D_MODEL), lambda i, f: (f, 0)),
                  _full(g.shape), _full(b.shape)],
        out_specs=pl.BlockSpec((tm, D_MODEL), lambda i, f: (i, 0)),
        out_shape=jax.ShapeDtypeStruct((m, D_MODEL), F32),
        scratch_shapes=[pltpu.VMEM((tm, D_MODEL), BF16), pltpu.VMEM((tm, D_MODEL), F32)],
        compiler_params=_cparams("parallel", "arbitrary"),
    )(x, w1, w2, g, b)


def _rot_lanes(x, c, s1, s2):
    return x * c + pltpu.roll(x, LANES - ROPE_D // 2, 1) * s1 + pltpu.roll(x, ROPE_D // 2, 1) * s2


def _proj_c_kernel(x_ref, win_ref, wpe_ref, qg_ref, kvg_ref, wqb_ref, wk_ref, wv_ref, ones_ref,
                   qc_ref, qs1_ref, qs2_ref, kc_ref, ks1_ref, ks2_ref,
                   q_ref, k_ref, v_ref, ckv_ref, kpe_ref):
    xb = x_ref[...].astype(BF16)
    qa = _dot(xb, win_ref[:, :Q_LORA])
    kva = _dot(xb, win_ref[:, Q_LORA:Q_LORA + KV_LORA])
    kpe = _dot(xb, wpe_ref[...])
    qn = _rms_norm(qa, qg_ref[...]).astype(BF16)
    qc, qs1, qs2 = qc_ref[...], qs1_ref[...], qs2_ref[...]
    for pair in range(H_C // 2):
        both = _dot(qn, wqb_ref[:, pair * 2 * LANES:(pair + 1) * 2 * LANES])
        for e in range(2):
            h = 2 * pair + e
            seg = both[:, e * LANES:(e + 1) * LANES]
            q_ref[:, h * LANES:(h + 1) * LANES] = _rot_lanes(seg, qc, qs1, qs2).astype(BF16)
    ckv = _rms_norm(kva, kvg_ref[...])
    ckv_ref[...] = ckv
    ckv16 = ckv.astype(BF16)
    kpe_rot = _rot_lanes(kpe, kc_ref[...], ks1_ref[...], ks2_ref[...])
    kpe_ref[...] = kpe_rot[:, :ROPE_D]
    kshift = pltpu.roll(kpe_rot, NOPE, 1)
    k_ref[...] = (_dot(ckv16, wk_ref[...]) + jnp.concatenate([kshift] * H_C, axis=1)).astype(BF16)
    v_ref[...] = (_dot(ckv16, wv_ref[...]) + ones_ref[...]).astype(BF16)


def _proj_c(x, wts, tabs, tm):
    m = x.shape[0]
    ntab = tabs[0].shape[0] // tm
    row = lambda i: (i, 0)
    tab = pl.BlockSpec((tm, LANES), lambda i: (i % ntab, 0))
    return pl.pallas_call(
        _proj_c_kernel,
        grid=(m // tm,),
        in_specs=[pl.BlockSpec((tm, D_MODEL), row)] + [_full(w.shape) for w in wts] + [tab] * 6,
        out_specs=[pl.BlockSpec((tm, H_C * LANES), row)] * 3
        + [pl.BlockSpec((tm, KV_LORA), row), pl.BlockSpec((tm, ROPE_D), row)],
        out_shape=(jax.ShapeDtypeStruct((m, H_C * LANES), BF16),) * 3
        + (jax.ShapeDtypeStruct((m, KV_LORA), F32), jax.ShapeDtypeStruct((m, ROPE_D), F32)),
        compiler_params=_cparams("parallel"),
    )(x, *wts, *tabs)


def _head_mm_kernel(x_ref, w_ref, o_ref, *, hp, din, dout):
    for j in range(hp):
        o_ref[:, j * dout:(j + 1) * dout] = _dot(x_ref[:, j * din:(j + 1) * din], w_ref[j]).astype(BF16)


def _head_mm(x, w, hp):
    m = x.shape[0]
    heads, din, dout = w.shape
    return pl.pallas_call(
        functools.partial(_head_mm_kernel, hp=hp, din=din, dout=dout),
        grid=(heads // hp,),
        in_specs=[pl.BlockSpec((m, hp * din), lambda h: (0, h)),
                  pl.BlockSpec((hp, din, dout), lambda h: (h, 0, 0))],
        out_specs=pl.BlockSpec((m, hp * dout), lambda h: (0, h)),
        out_shape=jax.ShapeDtypeStruct((m, heads * dout), BF16),
        compiler_params=_cparams("parallel"),
    )(x, w)


def _fox_decode_kernel(pt_ref, q_ref, kn_ref, vn_ref, lfr_ref, lft_ref, sel_ref, cache_k, cache_v, cache_lf,
                       o_ref, kbuf, vbuf, lbuf, ksem, vsem, lsem,
                       qbd_ref, m_ref, l_ref, acc_ref, car_ref, cnq_ref, *, gp, ds, layer):
    b = pl.program_id(0)
    pg = pl.program_id(1)
    t = b * pl.num_programs(1) + pg
    slot = lax.rem(t, 2)
    nrow = H_B * ds
    page = lbuf.shape[3]
    npages = gp * pl.num_programs(1)

    def gather(bb, gg, sl):
        copies = []
        for j in range(gp):
            pid = pt_ref[bb, npages - 1 - (gg * gp + j)]
            copies.append((pltpu.make_async_copy(cache_k.at[layer, pid], kbuf.at[sl, j], ksem.at[sl]), 0))
            copies.append((pltpu.make_async_copy(cache_v.at[layer, pid], vbuf.at[sl, j], vsem.at[sl]), 0))
            copies.append((pltpu.make_async_copy(cache_lf.at[layer, pid], lbuf.at[sl, j], lsem.at[sl]), 0))
        return copies

    @pl.when(t == 0)
    def _():
        for cp, prio in gather(0, 0, 0):
            cp.start(priority=prio)

    @pl.when(t + 1 < pl.num_programs(0) * pl.num_programs(1))
    def _():
        nb_, ng_ = _step_after(b, pg)
        for cp, prio in gather(nb_, ng_, 1 - slot):
            cp.start(priority=prio)

    for cp, _ in gather(b, pg, slot):
        cp.wait()

    def expand(x):
        return jnp.concatenate([jnp.broadcast_to(x[h:h + 1, :], (ds, x.shape[1])) for h in range(H_B)], axis=0)

    @pl.when(pg == 0)
    def _():
        q = q_ref[0].astype(F32)
        rh = lax.broadcasted_iota(jnp.int32, (nrow, AB_FOX), 0) // ds
        ch = lax.broadcasted_iota(jnp.int32, (nrow, AB_FOX), 1) // DH
        qbd = jnp.where(rh == ch, jnp.concatenate([q] * H_B, axis=0), 0.0).astype(BF16)
        qbd_ref[...] = qbd
        r = lax.broadcasted_iota(jnp.int32, (nrow, nrow), 0)
        c = lax.broadcasted_iota(jnp.int32, (nrow, nrow), 1)
        m_row = jnp.where((r // ds == c // ds) & (c <= r), 1.0, 0.0)
        cn_col = jnp.sum(m_row * lfr_ref[0], axis=1, keepdims=True)
        cnq_ref[...] = cn_col * LOG2E
        lft = lft_ref[0]
        npad = lft.shape[1]
        kk = lax.broadcasted_iota(jnp.int32, (1, npad), 1)
        cnt = jnp.zeros((H_B, npad), F32)
        for k2 in range(ds):
            cnt = cnt + lft[:, k2:k2 + 1] * jnp.where(kk >= k2, 1.0, 0.0)
        s = _dot_nt(qbd, kn_ref[0]) + (cn_col - expand(cnt)) * LOG2E
        rq = lax.broadcasted_iota(jnp.int32, (nrow, npad), 0) % ds
        ck = lax.broadcasted_iota(jnp.int32, (nrow, npad), 1)
        s = jnp.where(ck <= rq, s, NEG_INF)
        m = jnp.max(s, -1, keepdims=True)
        p = jnp.exp2(s - m)
        m_ref[...] = m
        l_ref[...] = jnp.sum(p, -1, keepdims=True)
        acc_ref[...] = _dot(p.astype(BF16), vn_ref[0])
        car_ref[...] = jnp.zeros_like(car_ref)

    qbd = qbd_ref[...]
    xs = []
    for j in range(gp):
        xs += list(_split3(lbuf[slot, j]))
    yz = _dot(jnp.concatenate(xs, axis=0).astype(BF16), sel_ref[...])
    car = car_ref[...]
    revs = []
    for j in range(gp):
        o = j * 3 * H_B
        y = yz[o:o + H_B] + yz[o + H_B:o + 2 * H_B] + yz[o + 2 * H_B:o + 3 * H_B]
        revs.append(car + y[:, :page])
        car = car + y[:, page:]
    car_ref[...] = car
    kt_all = jnp.concatenate([kbuf[slot, j].astype(BF16) for j in range(gp)], axis=1)
    vt_all = jnp.concatenate([vbuf[slot, j].astype(BF16) for j in range(gp)], axis=1)
    s = _dot(qbd, kt_all) + expand(jnp.concatenate(revs, axis=1) * LOG2E) + cnq_ref[...]
    m = m_ref[...]
    m_new = jnp.maximum(m, jnp.max(s, -1, keepdims=True))
    alpha = jnp.exp2(m - m_new)
    p = jnp.exp2(s - m_new)
    m_ref[...] = m_new
    l_ref[...] = alpha * l_ref[...] + jnp.sum(p, -1, keepdims=True)
    acc_ref[...] = alpha * acc_ref[...] + _dot_nt(p.astype(BF16), vt_all)

    @pl.when(pg == pl.num_programs(1) - 1)
    def _():
        acc = acc_ref[...] / l_ref[...]
        ch = lax.broadcasted_iota(jnp.int32, (ds, AB_FOX), 1) // DH
        out = jnp.zeros((ds, AB_FOX), F32)
        for h in range(H_B):
            out = out + jnp.where(ch == h, acc[h * ds:(h + 1) * ds, :], 0.0)
        o_ref[0] = out.astype(BF16)


def _fox_decode(page_table, layer, q, kn, vn, lf_row, lf_t, sel, cache_kt, cache_vt, cache_lft, gp):
    nb, ds, _ = q.shape
    npages = page_table.shape[1]
    page = cache_kt.shape[3]
    nrow = H_B * ds

    per_b = lambda b, pg, pt: (b, 0, 0)
    in_specs = [pl.BlockSpec((1,) + a.shape[1:], per_b) for a in (q, kn, vn, lf_row, lf_t)]
    in_specs += [pl.BlockSpec(sel.shape, lambda b, pg, pt: (0, 0))]
    in_specs += [pl.BlockSpec(memory_space=pl.ANY)] * 3
    grid_spec = pltpu.PrefetchScalarGridSpec(
        num_scalar_prefetch=1,
        grid=(nb, npages // gp),
        in_specs=in_specs,
        out_specs=pl.BlockSpec((1, ds, AB_FOX), per_b),
        scratch_shapes=[pltpu.VMEM((2, gp, AB_FOX, page), F32), pltpu.VMEM((2, gp, AB_FOX, page), F32),
                        pltpu.VMEM((2, gp, H_B, page), F32),
                        pltpu.SemaphoreType.DMA((2,)), pltpu.SemaphoreType.DMA((2,)), pltpu.SemaphoreType.DMA((2,)),
                        pltpu.VMEM((nrow, AB_FOX), BF16), pltpu.VMEM((nrow, 1), F32),
                        pltpu.VMEM((nrow, 1), F32), pltpu.VMEM((nrow, AB_FOX), F32),
                        pltpu.VMEM((H_B, page), F32), pltpu.VMEM((nrow, 1), F32)])
    return pl.pallas_call(
        functools.partial(_fox_decode_kernel, gp=gp, ds=ds, layer=layer),
        grid_spec=grid_spec,
        out_shape=jax.ShapeDtypeStruct((nb, ds, AB_FOX), BF16),
        compiler_params=_cparams("arbitrary", "arbitrary"),
    )(page_table, q, kn, vn, lf_row, lf_t, sel, cache_kt, cache_vt, cache_lft)


def _logf_selectors(page):
    a = jnp.arange(page)[:, None]
    b = jnp.arange(page)[None, :]
    return jnp.concatenate([a > b, jnp.ones((page, page), bool)], axis=1).astype(BF16)


def _step_after(b, g):
    last = g == pl.num_programs(1) - 1
    return jnp.where(last, b + 1, b), jnp.where(last, 0, g + 1)


def _mla_decode_kernel(pt_ref, ql_ref, qp_ref, cn_ref, pn_ref, cache_c, cache_p, o_ref,
                       cbuf, pbuf, csem, psem, m_ref, l_ref, acc_ref, *, gp, base):
    b = pl.program_id(0)
    pg = pl.program_id(1)
    t = b * pl.num_programs(1) + pg
    slot = lax.rem(t, 2)
    page = cbuf.shape[1] // gp
    ql = ql_ref[0]
    qp = qp_ref[0]
    nrow = ql.shape[0]

    def gather(bb, gg, sl):
        copies = []
        for j in range(gp):
            pid = base + pt_ref[bb, gg * gp + j]
            rows = pl.ds(j * page, page)
            copies.append((pltpu.make_async_copy(cache_c.at[pid], cbuf.at[sl, rows, :], csem.at[sl]), 0))
            copies.append((pltpu.make_async_copy(cache_p.at[pid], pbuf.at[sl, :, rows], psem.at[sl]), 0))
        return copies

    @pl.when(t == 0)
    def _():
        for cp, prio in gather(0, 0, 0):
            cp.start(priority=prio)

    @pl.when(t + 1 < pl.num_programs(0) * pl.num_programs(1))
    def _():
        nb_, ng_ = _step_after(b, pg)
        for cp, prio in gather(nb_, ng_, 1 - slot):
            cp.start(priority=prio)

    for cp, _ in gather(b, pg, slot):
        cp.wait()

    @pl.when(pg == 0)
    def _():
        cn = cn_ref[0]
        npad = cn.shape[0]
        s = _dot_nt(ql, cn) + _dot_nt(qp, pn_ref[0])
        rq = lax.broadcasted_iota(jnp.int32, (nrow, npad), 0) // H_C
        ck = lax.broadcasted_iota(jnp.int32, (nrow, npad), 1)
        s = jnp.where(ck <= rq, s, NEG_INF)
        m = jnp.max(s, -1, keepdims=True)
        p = jnp.exp2(s - m)
        m_ref[...] = m
        l_ref[...] = jnp.sum(p, -1, keepdims=True)
        acc_ref[...] = _dot(p.astype(BF16), cn)

    c_all = cbuf[slot].astype(BF16)
    kp_all = pbuf[slot].astype(BF16)
    s = _dot_nt(ql, c_all) + _dot(qp, kp_all)
    m = m_ref[...]
    m_new = jnp.maximum(m, jnp.max(s, -1, keepdims=True))
    alpha = jnp.exp2(m - m_new)
    p = jnp.exp2(s - m_new)
    m_ref[...] = m_new
    l_ref[...] = alpha * l_ref[...] + jnp.sum(p, -1, keepdims=True)
    acc_ref[...] = alpha * acc_ref[...] + _dot(p.astype(BF16), c_all)

    @pl.when(pg == pl.num_programs(1) - 1)
    def _():
        o_ref[0] = (acc_ref[...] / l_ref[...]).astype(BF16)


def _mla_decode(page_table, base, ql, qp, cn, pn, cache_c, cache_pt, gp):
    nb, nrow, _ = ql.shape
    npages = page_table.shape[1]
    page = cache_c.shape[1]

    per_b = lambda b, pg, pt: (b, 0, 0)
    in_specs = [pl.BlockSpec((1,) + a.shape[1:], per_b) for a in (ql, qp, cn, pn)]
    in_specs += [pl.BlockSpec(memory_space=pl.ANY)] * 2
    grid_spec = pltpu.PrefetchScalarGridSpec(
        num_scalar_prefetch=1,
        grid=(nb, npages // gp),
        in_specs=in_specs,
        out_specs=pl.BlockSpec((1, nrow, KV_LORA), per_b),
        scratch_shapes=[pltpu.VMEM((2, gp * page, KV_LORA), F32), pltpu.VMEM((2, ROPE_D, gp * page), F32),
                        pltpu.SemaphoreType.DMA((2,)), pltpu.SemaphoreType.DMA((2,)),
                        pltpu.VMEM((nrow, 1), F32), pltpu.VMEM((nrow, 1), F32),
                        pltpu.VMEM((nrow, KV_LORA), F32)])
    return pl.pallas_call(
        functools.partial(_mla_decode_kernel, gp=gp, base=base),
        grid_spec=grid_spec,
        out_shape=jax.ShapeDtypeStruct((nb, nrow, KV_LORA), BF16),
        compiler_params=_cparams("arbitrary", "arbitrary"),
    )(page_table, ql, qp, cn, pn, cache_c, cache_pt)


def _ret_rope_tables(pos):
    half = DK // 2
    inv = ROPE_BASE ** (-jnp.arange(half, dtype=F32) / half)
    ang = pos.astype(F32)[:, None] * inv[None, :]
    cos, sin = jnp.cos(ang), jnp.sin(ang)
    return jnp.concatenate([cos, cos], -1), jnp.concatenate([-sin, sin], -1)


def _mla_rope_tables(pos, lane0, passthrough, scale):
    half = ROPE_D // 2
    inv = ROPE_BASE ** (-jnp.arange(half, dtype=F32) / half)
    ang = pos.astype(F32)[:, None] * inv[None, :]
    cos, sin = jnp.cos(ang) * scale, jnp.sin(ang) * scale
    n = pos.shape[0]
    z = lambda w: jnp.zeros((n, w), F32)
    lead = jnp.full((n, lane0), scale if passthrough else 0.0, F32)
    tail = z(LANES - lane0 - ROPE_D)
    c = jnp.concatenate([lead, cos, cos, tail], -1)
    s1 = jnp.concatenate([z(lane0), -sin, z(half), tail], -1)
    s2 = jnp.concatenate([z(lane0), z(half), sin, tail], -1)
    return c, s1, s2


def _retention_tables(length, reps):
    lg = jnp.log(1.0 - 2.0 ** (-5.0 - jnp.arange(H_A, dtype=F32)))
    idx = jnp.arange(length, dtype=F32)
    diff = idx[:, None] - idx[None, :]
    decay = jnp.where(diff >= 0, jnp.exp(lg[:, None, None] * jnp.maximum(diff, 0.0)[None]), 0.0)
    qd = jnp.exp((idx[:, None] + 1.0) * lg[None, :]).T
    kd = jnp.exp((length - 1.0 - idx)[:, None] * lg[None, :]).T
    dl = jnp.exp(length * lg)
    if reps > 1:
        eye = jnp.eye(reps, dtype=F32)
        decay = jnp.einsum('ab,hij->haibj', eye, decay).reshape(H_A, reps * length, reps * length)
        qd = jnp.tile(qd, (1, reps))
        kd = jnp.tile(kd, (1, reps))
    n = reps * length
    qd = jnp.broadcast_to(qd[:, :, None], (H_A, n, DV))
    kd = jnp.broadcast_to(kd[:, :, None], (H_A, n, DK))
    dl = jnp.broadcast_to(dl[:, None, None], (H_A, DK, DV))
    return decay, qd, kd, dl


def _ones_row(heads, lane):
    return (jnp.arange(heads * LANES) % LANES == lane).astype(F32)[None, :]


def _prep_ab_weights(w_in, b_f):
    main = w_in[:, :4 * AB_RET].astype(BF16)
    fox = w_in[:, 4 * AB_RET:4 * AB_RET + 3 * AB_FOX].reshape(D_MODEL, 3 * H_B, DH)
    fox = jnp.pad(fox, ((0, 0), (0, 0), (0, LANES - DH))).reshape(D_MODEL, 3 * FOX_PAD).astype(BF16)
    w_f = jnp.pad(w_in[:, 4 * AB_RET + 3 * AB_FOX:], ((0, 0), (0, LANES - H_B))).astype(BF16)
    w_kvt = w_in[:, 4 * AB_RET + AB_FOX:4 * AB_RET + 3 * AB_FOX].T.astype(BF16)
    bf_row = jnp.pad(b_f, (0, LANES - H_B))[None, :]
    return main, fox, w_f, w_kvt, bf_row, _ones_row(H_B, DH)


def _prep_c_weights(w_in, q_g, w_qb, kv_g, w_kvb):
    win = w_in[:, :Q_LORA + KV_LORA].astype(BF16)
    wpe = jnp.pad(w_in[:, Q_LORA + KV_LORA:], ((0, 0), (0, LANES - ROPE_D))).astype(BF16)
    wqb = w_qb.reshape(Q_LORA, H_C, NOPE + ROPE_D)
    wqb = jnp.pad(wqb, ((0, 0), (0, 0), (0, LANES - NOPE - ROPE_D))).reshape(Q_LORA, H_C * LANES).astype(BF16)
    wkv = w_kvb.reshape(KV_LORA, H_C, NOPE + V_DIM)
    w_uk, w_uv = wkv[..., :NOPE], wkv[..., NOPE:]
    wk = jnp.pad(w_uk, ((0, 0), (0, 0), (0, LANES - NOPE))).reshape(KV_LORA, H_C * LANES).astype(BF16)
    wv = jnp.pad(w_uv, ((0, 0), (0, 0), (0, LANES - V_DIM))).reshape(KV_LORA, H_C * LANES).astype(BF16)
    w_abs = jnp.pad(w_uk.transpose(1, 2, 0), ((0, 0), (0, LANES - NOPE), (0, 0))).astype(BF16)
    w_val = w_uv.transpose(1, 0, 2).astype(BF16)
    return (win, wpe, q_g[None, :], kv_g[None, :], wqb, wk, wv, _ones_row(H_C, V_DIM)), w_abs, w_val


TM = 512
TM_MLP = 1024
TF_MLP = 2048
TQ, TK = 512, 256
OUT_SUB = 128
FLASH_UNROLL = 4
TC = 512
RET_SUB = 8
RET_BB = 16
FOX_HG = 4
MLA_HG = 4
PAGES_PER_STEP = 16
NEW_PAD = 16


def kernel(x_prompt, x_sample, state_ret, cache_fox_k, cache_fox_v, cache_fox_logf, cache_mla_ckv,
           cache_mla_kpe, page_table, w_in_ab, ret_gn_g, fox_b_f, w_out_ab, w_in_c, mla_q_norm_g,
           mla_w_qb, mla_kv_norm_g, mla_w_kvb, w_out_c, ln_mix_g, ln_mix_b, ln_mlp_g, ln_mlp_b,
           mlp_w1, mlp_w2):
    batch, seq, _ = x_prompt.shape
    nb, ds, _ = x_sample.shape
    n_pool, page = cache_fox_k.shape[1], cache_fox_k.shape[2]
    past_len = page_table.shape[1] * page
    mp, ms = batch * seq, nb * ds
    pos_p = jnp.arange(seq)
    pos_s = past_len + jnp.arange(ds)
    pos_s_rows = jnp.tile(pos_s, nb)

    hp = x_prompt.reshape(mp, D_MODEL)
    hs = x_sample.reshape(ms, D_MODEL)
    tms = min(TM, ms)
    outs = {k: [] for k in ("ret_p", "ret_s", "fk_p", "fv_p", "fl_p", "fk_s", "fv_s", "fl_s",
                            "ck_p", "kp_p", "ck_s", "kp_s")}

    for l in range(DEPTH):
        i = l // 2
        row = lambda a: a[None, :]
        if l % 2 == 0:
            wts = _prep_ab_weights(w_in_ab[i], fox_b_f[i])
            w_out = w_out_ab[i].astype(BF16)
            gn = ret_gn_g[i][None, :]
            cos, sin = _ret_rope_tables(pos_p)
            rq, rk, rv, rg, fq, fk16, fv16, lf, fkt, fvt = _proj_ab(hp, *wts[:4], cos, sin, *wts[4:], TM, seq)
            y, s_fin = _ret_prompt(rq, rk, rv, rg, _retention_tables(RET_CHUNK, 1), gn, batch, seq, RET_SUB)
            fqb, fkb = _fox_bias(lf, fq, fk16, batch, seq, TC)
            fo = _flash(fqb, fkb, fv16, batch, seq, H_B, DH, FOX_HG, TQ, TK)
            mix_parts_p = [y, fo]
            unpad = lambda a, n: a.reshape(n, H_B, LANES)[:, :, :DH]
            seq_major = lambda a: a.reshape(batch, H_B, DH, seq).transpose(0, 3, 1, 2)
            outs["ret_p"].append(s_fin)
            outs["fk_p"].append(seq_major(fkt))
            outs["fv_p"].append(seq_major(fvt))
            outs["fl_p"].append(lf.reshape(batch, seq, H_B))
            cos, sin = _ret_rope_tables(pos_s_rows)
            rq, rk, rv, rg, fq, fk16, fv16, lf, fk, fv = _proj_ab(hs, *wts[:4], cos, sin, *wts[4:], tms, None)
            y, s_new = _ret_sample(rq, rk, rv, rg, state_ret[i], _retention_tables(ds, RET_BB), gn, RET_BB, ds)
            padn = lambda a: jnp.pad(unpad(a, ms).reshape(nb, ds, AB_FOX), ((0, 0), (0, NEW_PAD - ds), (0, 0)))
            lf_hq = lf.reshape(nb, ds, H_B).transpose(0, 2, 1)
            fo = _fox_decode(page_table, i, unpad(fq, ms).reshape(nb, ds, AB_FOX), padn(fk16), padn(fv16),
                             lf_hq.reshape(nb, 1, H_B * ds),
                             jnp.pad(lf_hq, ((0, 0), (0, 0), (0, NEW_PAD - ds))),
                             _logf_selectors(page),
                             cache_fox_k.transpose(0, 1, 3, 4, 2).reshape(-1, n_pool, AB_FOX, page),
                             cache_fox_v.transpose(0, 1, 3, 4, 2).reshape(-1, n_pool, AB_FOX, page),
                             cache_fox_logf.transpose(0, 1, 3, 2), PAGES_PER_STEP)
            mix_parts_s = [y, fo.reshape(ms, AB_FOX)]
            outs["ret_s"].append(s_new)
            outs["fk_s"].append(unpad(fk, ms).reshape(nb, ds, H_B, DH))
            outs["fv_s"].append(unpad(fv, ms).reshape(nb, ds, H_B, DH))
            outs["fl_s"].append(lf.reshape(nb, ds, H_B))
        else:
            wts, w_abs, w_val = _prep_c_weights(w_in_c[i], mla_q_norm_g[i], mla_w_qb[i],
                                                mla_kv_norm_g[i], mla_w_kvb[i])
            w_out = w_out_c[i].astype(BF16)
            qscale = (NOPE + ROPE_D) ** -0.5 * LOG2E
            tabs = _mla_rope_tables(pos_p, NOPE, True, qscale) + _mla_rope_tables(pos_p, 0, False, 1.0)
            q, k, v, ckv, kpe = _proj_c(hp, wts, tabs, TM)
            o = _flash(q, k, v, batch, seq, H_C, V_DIM, MLA_HG, TQ, TK)
            mix_parts_p = [o]
            outs["ck_p"].append(ckv.reshape(batch, seq, KV_LORA))
            outs["kp_p"].append(kpe.reshape(batch, seq, ROPE_D))
            tabs = _mla_rope_tables(pos_s_rows, NOPE, True, qscale) + _mla_rope_tables(pos_s_rows, 0, False, 1.0)
            q, _, _, ckv, kpe = _proj_c(hs, wts, tabs, tms)
            q_lat = _head_mm(q, w_abs, 1).reshape(nb, ds * H_C, KV_LORA)
            q_pe = q.reshape(nb, ds * H_C, LANES)[:, :, NOPE:NOPE + ROPE_D]
            padn = lambda a: jnp.pad(a.reshape(nb, ds, -1), ((0, 0), (0, NEW_PAD - ds), (0, 0))).astype(BF16)
            o_lat = _mla_decode(page_table, i * n_pool, q_lat, q_pe, padn(ckv), padn(kpe),
                                cache_mla_ckv.reshape(-1, page, KV_LORA),
                                cache_mla_kpe.transpose(0, 1, 3, 2).reshape(-1, ROPE_D, page), PAGES_PER_STEP)
            o = _head_mm(o_lat.reshape(ms, H_C * KV_LORA), w_val, 2)
            mix_parts_s = [o]
            outs["ck_s"].append(ckv.reshape(nb, ds, KV_LORA))
            outs["kp_s"].append(kpe.reshape(nb, ds, ROPE_D))

        w1 = mlp_w1[l].astype(BF16)
        w2 = mlp_w2[l].astype(BF16)
        hp = _outproj_ln(mix_parts_p, w_out, hp, row(ln_mix_g[l]), row(ln_mix_b[l]), TM)
        hs = _outproj_ln(mix_parts_s, w_out, hs, row(ln_mix_g[l]), row(ln_mix_b[l]), tms)
        hp = _mlp_ln(hp, w1, w2, row(ln_mlp_g[l]), row(ln_mlp_b[l]), TM_MLP, TF_MLP)
        hs = _mlp_ln(hs, w1, w2, row(ln_mlp_g[l]), row(ln_mlp_b[l]), min(TM_MLP, ms), TF_MLP)

    st = lambda k: jnp.stack(outs[k])
    return (hp.reshape(batch, seq, D_MODEL), hs.reshape(nb, ds, D_MODEL),
            st("ret_p"), st("ret_s"), st("fk_p"), st("fv_p"), st("fl_p"),
            st("fk_s"), st("fv_s"), st("fl_s"), st("ck_p"), st("kp_p"), st("ck_s"), st("kp_s"))
```
